```python
import math
import jax, jax.numpy as jnp
from jax import lax
import numpy as np

D_MODEL = 2048
BATCH = 2
SEQ = 16384
DEPTH = 1

ATT_HEADS = 8
ATT_HEAD_DIM = 64
ATT_V_DIM = 2 * ATT_HEAD_DIM
ATT_QK_COLS = ATT_HEADS * 2 * ATT_HEAD_DIM
ATT_WIDTH = ATT_HEADS * ATT_V_DIM
ROT_DIM = ATT_HEAD_DIM // 4
ROPE_THETA = 500000.0
Q_BLOCK = 128
SUBLN_EPS = 1e-5
RWKV_HEADS = 16
RWKV_HEAD_DIM = 64
RWKV_WIDTH = RWKV_HEADS * RWKV_HEAD_DIM
DECAY_LORA = 64
ICLR_LORA = 64
GATE_LORA = 160
RWKV_SHIFT_COLS = 3 * RWKV_WIDTH + DECAY_LORA + ICLR_LORA + GATE_LORA
GN_EPS = 64e-5
N_BRANCH = 2
BRANCH_WIDTH = 1024
GATE_COLS = N_BRANCH * D_MODEL
IN_COLS = 2 * ATT_QK_COLS + ATT_WIDTH + RWKV_SHIFT_COLS + GATE_COLS
D_FF = 5632
CONV_WIDTH = 3
NORM_EPS = 1e-6

kernel_name = "hybrid_diffattn_rwkv7_convglu_adaln"


def rms_norm(x, gain, eps=NORM_EPS):
    xf = x.astype(jnp.float32)
    xf = xf * lax.rsqrt(jnp.mean(xf * xf, axis=-1, keepdims=True) + eps)
    return xf.astype(x.dtype) * gain


def token_shift(p):
    return jnp.pad(p[:, :-1], ((0, 0), (1, 0), (0, 0)))


def rope_tables(seq):
    pos = jnp.arange(seq, dtype=jnp.float32)
    inv = ROPE_THETA ** (-jnp.arange(0, ROT_DIM, 2, dtype=jnp.float32) / ROT_DIM)
    ang = pos[:, None] * inv[None, :]
    return jnp.cos(ang), jnp.sin(ang)


def partial_rope(t, cos, sin):
    half = ROT_DIM // 2
    x1, x2 = t[..., :half], t[..., half:ROT_DIM]
    rot = jnp.concatenate([x1 * cos - x2 * sin, x2 * cos + x1 * sin], axis=-1)
    return jnp.concatenate([rot, t[..., ROT_DIM:]], axis=-1)


def diff_attention(q, k, v, lam, subln_gain, lambda_init):
    b_, h_, _, s_, d_ = q.shape
    n_blocks = s_ // Q_BLOCK
    offs = jnp.arange(Q_BLOCK)

    def one_query_block(i):
        q0 = i * Q_BLOCK
        qb = lax.dynamic_slice_in_dim(q, q0, Q_BLOCK, axis=3)
        q_pos = q0 + offs

        def body(j, carry):
            m, l, acc = carry
            k0 = j * Q_BLOCK
            kb = lax.dynamic_slice_in_dim(k, k0, Q_BLOCK, axis=3)
            vb = lax.dynamic_slice_in_dim(v, k0, Q_BLOCK, axis=2)
            s = jnp.einsum('bhmqd,bhmkd->bhmqk', qb, kb)
            causal = (k0 + offs)[None, :] <= q_pos[:, None]
            s = jnp.where(causal, s, -jnp.inf)
            m_new = jnp.maximum(m, s.max(-1))
            p = jnp.exp(s - m_new[..., None])
            corr = jnp.exp(m - m_new)
            l = l * corr + p.sum(-1)
            acc = acc * corr[..., None] + jnp.einsum('bhmqk,bhkv->bhmqv', p, vb)
            return m_new, l, acc

        init = (jnp.full((b_, h_, 2, Q_BLOCK), -jnp.inf, jnp.float32),
                jnp.zeros((b_, h_, 2, Q_BLOCK), jnp.float32),
                jnp.zeros((b_, h_, 2, Q_BLOCK, ATT_V_DIM), jnp.float32))
        _, l, acc = lax.fori_loop(jnp.zeros_like(i), i + 1, body, init)
        return acc / l[..., None]

    o = lax.map(one_query_block, jnp.arange(n_blocks))
    o = o.transpose(1, 2, 3, 0, 4, 5).reshape(b_, h_, 2, s_, ATT_V_DIM)
    o = o[:, :, 0] - lam * o[:, :, 1]
    o = rms_norm(o, subln_gain, SUBLN_EPS) * (1.0 - lambda_init)
    return o.transpose(0, 2, 1, 3).reshape(b_, s_, ATT_WIDTH)


def rwkv7_mix(r, k, v, xw, xa, xg, w0, w_lora_up, a0, a_lora_up, g_lora_up,
              k_k, k_a, r_k, lnx_w, lnx_b):
    b_, s_, c_ = r.shape
    H, N = RWKV_HEADS, RWKV_HEAD_DIM
    w_logit = -jax.nn.softplus(-(w0 + jnp.tanh(xw) @ w_lora_up)) - 0.5
    decay = jnp.exp(-jnp.exp(w_logit))
    a = jax.nn.sigmoid(a0 + xa @ a_lora_up)
    g = jax.nn.sigmoid(xg) @ g_lora_up
    kk = (k * k_k).reshape(b_, s_, H, N)
    kk = kk / jnp.maximum(jnp.linalg.norm(kk, axis=-1, keepdims=True), 1e-12)
    kk = kk.reshape(b_, s_, c_)
    k = k * (1.0 + (a - 1.0) * k_a)

    def heads_tm(t):
        return t.reshape(b_, s_, H, N).transpose(1, 0, 2, 3)

    def step(state, inp):
        r_t, w_t, k_t, v_t, a_t, b_t = inp
        sa = jnp.einsum('bhij,bhj->bhi', state, a_t)
        state = (state * w_t[:, :, None, :] + sa[..., None] * b_t[:, :, None, :]
                 + v_t[..., None] * k_t[:, :, None, :])
        return state, jnp.einsum('bhij,bhj->bhi', state, r_t)

    state0 = jnp.zeros((b_, H, N, N), jnp.float32)
    _, y = lax.scan(step, state0, (heads_tm(r), heads_tm(decay), heads_tm(k),
                                   heads_tm(v), heads_tm(-kk), heads_tm(kk * a)))
    y = y.transpose(1, 0, 2, 3)
    mu = y.mean(-1, keepdims=True)
    var = jnp.mean((y - mu) ** 2, axis=-1, keepdims=True)
    y = ((y - mu) * lax.rsqrt(var + GN_EPS)).reshape(b_, s_, c_) * lnx_w + lnx_b
    rh = r.reshape(b_, s_, H, N)
    kh = k.reshape(b_, s_, H, N)
    vh = v.reshape(b_, s_, H, N)
    bonus = (jnp.sum(rh * kh * r_k, axis=-1, keepdims=True) * vh).reshape(b_, s_, c_)
    return (y + bonus) * g


def causal_dwconv(u, w, b):
    s_ = u.shape[1]
    up = jnp.pad(u, ((0, 0), (CONV_WIDTH - 1, 0), (0, 0)))
    y = b
    for tap in range(CONV_WIDTH):
        y = y + up[:, tap:tap + s_] * w[tap]
    return y


def setup_inputs(seed: int = 0) -> dict:
    key = jax.random.key(seed)
    ks = jax.random.split(key, 32)
    L = DEPTH
    f32 = jnp.float32

    def nrm(k, shape, scale):
        return jax.random.normal(k, shape, f32) * scale

    return {
        "x": nrm(ks[0], (BATCH, SEQ, D_MODEL), 1.0),
        "c": nrm(ks[1], (BATCH, D_MODEL), 1.0),
        "w_ada": nrm(ks[2], (L, D_MODEL, 6 * D_MODEL), D_MODEL ** -0.5),
        "b_ada": nrm(ks[3], (L, 6 * D_MODEL), 0.02),
        "norm1_gain": 1.0 + nrm(ks[4], (L, D_MODEL), 0.02),
        "w_in": nrm(ks[5], (L, D_MODEL, IN_COLS), D_MODEL ** -0.5),
        "lambda_q1": nrm(ks[6], (L, ATT_HEAD_DIM), 0.1),
        "lambda_k1": nrm(ks[7], (L, ATT_HEAD_DIM), 0.1),
        "lambda_q2": nrm(ks[8], (L, ATT_HEAD_DIM), 0.1),
        "lambda_k2": nrm(ks[9], (L, ATT_HEAD_DIM), 0.1),
        "subln_gain": 1.0 + nrm(ks[10], (L, ATT_V_DIM), 0.02),
        "mu_shift": jax.random.uniform(ks[11], (L, RWKV_SHIFT_COLS), f32, 0.0, 1.0),
        "w0": jax.random.uniform(ks[12], (L, RWKV_WIDTH), f32, -5.0, 1.0),
        "w_lora_up": nrm(ks[13], (L, DECAY_LORA, RWKV_WIDTH), DECAY_LORA ** -0.5),
        "a0": nrm(ks[14], (L, RWKV_WIDTH), 0.5),
        "a_lora_up": nrm(ks[15], (L, ICLR_LORA, RWKV_WIDTH), 0.5 * ICLR_LORA ** -0.5),
        "g_lora_up": nrm(ks[16], (L, GATE_LORA, RWKV_WIDTH), GATE_LORA ** -0.5),
        "k_k": 0.85 + nrm(ks[17], (L, RWKV_WIDTH), 0.05),
        "k_a": 1.0 + nrm(ks[18], (L, RWKV_WIDTH), 0.05),
        "r_k": nrm(ks[19], (L, RWKV_HEADS, RWKV_HEAD_DIM), 0.1),
        "lnx_w": 1.0 + nrm(ks[20], (L, RWKV_WIDTH), 0.02),
        "lnx_b": nrm(ks[21], (L, RWKV_WIDTH), 0.02),
        "w_branch": nrm(ks[22], (L, N_BRANCH, BRANCH_WIDTH, D_MODEL), BRANCH_WIDTH ** -0.5),
        "w_out": nrm(ks[23], (L, D_MODEL, D_MODEL), D_MODEL ** -0.5),
        "norm2_gain": 1.0 + nrm(ks[24], (L, D_MODEL), 0.02),
        "w_up": nrm(ks[25], (L, D_MODEL, 2 * D_FF), D_MODEL ** -0.5),
        "conv_w": nrm(ks[26], (L, CONV_WIDTH, D_FF), CONV_WIDTH ** -0.5),
        "conv_b": nrm(ks[27], (L, D_FF), 0.02),
        "w_down": nrm(ks[28], (L, D_FF, D_MODEL), D_FF ** -0.5),
        "final_gain": 1.0 + nrm(ks[29], (D_MODEL,), 0.02),
    }


def reference(x, c, w_ada, b_ada, norm1_gain, w_in, lambda_q1, lambda_k1,
              lambda_q2, lambda_k2, subln_gain, mu_shift, w0, w_lora_up, a0,
              a_lora_up, g_lora_up, k_k, k_a, r_k, lnx_w, lnx_b, w_branch,
              w_out, norm2_gain, w_up, conv_w, conv_b, w_down, final_gain):
    b_, s_, _ = x.shape
    f32 = jnp.float32
    cos, sin = rope_tables(s_)
    split_in = list(np.cumsum([ATT_QK_COLS, ATT_QK_COLS, ATT_WIDTH, RWKV_SHIFT_COLS]))
    split_rwkv = list(np.cumsum([RWKV_WIDTH, RWKV_WIDTH, RWKV_WIDTH, DECAY_LORA, ICLR_LORA]))
    for l in range(DEPTH):
        lambda_init = 0.8 - 0.6 * math.exp(-0.3 * l)
        mod = (jax.nn.silu(c) @ w_ada[l] + b_ada[l])[:, None, :]
        sh1, sc1, gt1, sh2, sc2, gt2 = jnp.split(mod, 6, axis=-1)

        h = rms_norm(x, norm1_gain[l]) * (1.0 + sc1) + sh1
        p = h @ w_in[l]
        q, k_att, v_att, rwkv_in, gate_logits = jnp.split(p, split_in, axis=-1)

        q = q.astype(f32).reshape(b_, s_, ATT_HEADS, 2, ATT_HEAD_DIM).transpose(0, 2, 3, 1, 4)
        k_att = k_att.astype(f32).reshape(b_, s_, ATT_HEADS, 2, ATT_HEAD_DIM).transpose(0, 2, 3, 1, 4)
        v_att = v_att.astype(f32).reshape(b_, s_, ATT_HEADS, ATT_V_DIM).transpose(0, 2, 1, 3)
        q = partial_rope(q, cos, sin) * (ATT_HEAD_DIM ** -0.5)
        k_att = partial_rope(k_att, cos, sin)
        lam = (jnp.exp(jnp.sum(lambda_q1[l].astype(f32) * lambda_k1[l].astype(f32)))
               - jnp.exp(jnp.sum(lambda_q2[l].astype(f32) * lambda_k2[l].astype(f32)))
               + lambda_init)
        y_att = diff_attention(q, k_att, v_att, lam, subln_gain[l].astype(f32), lambda_init)

        rwkv_in = rwkv_in + mu_shift[l] * (token_shift(rwkv_in) - rwkv_in)
        r_r, k_r, v_r, xw, xa, xg = jnp.split(rwkv_in.astype(f32), split_rwkv, axis=-1)
        y_rwkv = rwkv7_mix(r_r, k_r, v_r, xw, xa, xg,
                           w0[l].astype(f32), w_lora_up[l].astype(f32), a0[l].astype(f32),
                           a_lora_up[l].astype(f32), g_lora_up[l].astype(f32),
                           k_k[l].astype(f32), k_a[l].astype(f32), r_k[l].astype(f32),
                           lnx_w[l].astype(f32), lnx_b[l].astype(f32))

        gates = jax.nn.sigmoid(gate_logits).reshape(b_, s_, N_BRANCH, D_MODEL)
        merged = (gates[:, :, 0] * (y_att.astype(x.dtype) @ w_branch[l, 0])
                  + gates[:, :, 1] * (y_rwkv.astype(x.dtype) @ w_branch[l, 1]))
        x = x + gt1 * (merged @ w_out[l])

        h = rms_norm(x, norm2_gain[l]) * (1.0 + sc2) + sh2
        u_gate, u_val = jnp.split(h @ w_up[l], 2, axis=-1)
        u_gate = causal_dwconv(u_gate, conv_w[l], conv_b[l])
        x = x + gt2 * ((jax.nn.gelu(u_gate, approximate=False) * u_val) @ w_down[l])
    return rms_norm(x, final_gain)
```

```python
import functools
import math

import jax
import jax.numpy as jnp
from jax import lax
from jax.experimental import pallas as pl
from jax.experimental.pallas import tpu as pltpu

F32 = jnp.float32
BF16 = jnp.bfloat16
HIGHEST = lax.Precision.HIGHEST

ATT_HEADS = 8
ATT_HEAD_DIM = 64
ROT_DIM = ATT_HEAD_DIM // 4
ROPE_THETA = 500000.0
SUBLN_EPS = 1e-5
RWKV_HEADS = 16
RWKV_HEAD_DIM = 64
DECAY_LORA = 64
ICLR_LORA = 64
GATE_LORA = 160
GN_EPS = 64e-5
NORM_EPS = 1e-6
LAMBDA_INIT = 0.8 - 0.6 * math.exp(-0.3 * 0)

LANES = 128
VMEM_LIMIT = 48 * 1024 * 1024
RWKV_CHUNK = 64
LORA_PAD = 384


def _params(*sem):
    return pltpu.CompilerParams(dimension_semantics=sem, vmem_limit_bytes=VMEM_LIMIT)


def _dot(a, b, precision=None):
    return jnp.dot(a, b, preferred_element_type=F32, precision=precision)


def _dot_t(a, b, precision=None):
    return lax.dot_general(a, b, (((1,), (1,)), ((), ())),
                           preferred_element_type=F32, precision=precision)


def _mod_kernel(c_ref, w_ref, b_ref, o_ref):
    c = c_ref[...]
    s = c * jax.nn.sigmoid(c)
    o_ref[...] = _dot(s, w_ref[...], HIGHEST) + b_ref[...]


def ada_modulation(c8, w_ada, b_ada, tn=1024):
    m, d = c8.shape
    n = w_ada.shape[1]
    return pl.pallas_call(
        _mod_kernel,
        grid=(n // tn,),
        in_specs=[pl.BlockSpec((m, d), lambda j: (0, 0)),
                  pl.BlockSpec((d, tn), lambda j: (0, j)),
                  pl.BlockSpec((1, tn), lambda j: (0, j))],
        out_specs=pl.BlockSpec((m, tn), lambda j: (0, j)),
        out_shape=jax.ShapeDtypeStruct((m, n), F32),
        compiler_params=_params("arbitrary"),
        name="ada_mod",
    )(c8, w_ada, b_ada.reshape(1, n))


def _norm_mod_kernel(x_ref, mod_ref, gain_ref, o_ref, *, shift_row, scale_row):
    x = x_ref[...]
    xn = x * lax.rsqrt(jnp.mean(x * x, axis=-1, keepdims=True) + NORM_EPS)
    h = xn * gain_ref[...] * (1.0 + mod_ref[scale_row:scale_row + 1, :]) + mod_ref[shift_row:shift_row + 1, :]
    o_ref[...] = h.astype(o_ref.dtype)


def norm_mod(x, mod, gain, shift_row, scale_row, tm=512):
    b, s, d = x.shape
    tm = min(tm, s)
    return pl.pallas_call(
        functools.partial(_norm_mod_kernel, shift_row=shift_row, scale_row=scale_row),
        grid=(b, s // tm),
        in_specs=[pl.BlockSpec((None, tm, d), lambda bi, i: (bi, i, 0)),
                  pl.BlockSpec((None, 8, d), lambda bi, i: (bi, 0, 0)),
                  pl.BlockSpec((1, d), lambda bi, i: (0, 0))],
        out_specs=pl.BlockSpec((None, tm, d), lambda bi, i: (bi, i, 0)),
        out_shape=jax.ShapeDtypeStruct((b, s, d), BF16),
        compiler_params=_params("parallel", "parallel"),
        name="norm_mod",
    )(x, mod, gain.reshape(1, d))


def _mm_kernel(a_ref, w_ref, o_ref):
    o_ref[...] = _dot(a_ref[...], w_ref[...]).astype(o_ref.dtype)


def _mm_rope_kernel(a_ref, w_ref, cos_ref, sina_ref, sinb_ref, o_ref, *, tn, q_cols):
    acc = _dot(a_ref[...], w_ref[...])
    scale = jnp.where(pl.program_id(2) * tn < q_cols, ATT_HEAD_DIM ** -0.5, 1.0).astype(F32)
    cos, sina, sinb = cos_ref[...], sina_ref[...], sinb_ref[...]
    for j in range(tn // LANES):
        t = acc[:, j * LANES:(j + 1) * LANES]
        half = ROT_DIM // 2
        r = t * cos + pltpu.roll(t, LANES - half, 1) * sina + pltpu.roll(t, half, 1) * sinb
        o_ref[:, j * LANES:(j + 1) * LANES] = (r * scale).astype(o_ref.dtype)


def matmul(a, w, out_dtype, tm=1024, tn=512, rope=None):
    b, s, k = a.shape
    n = w.shape[1]
    tm = min(tm, s)
    tn = min(tn, n)
    assert s % tm == 0 and n % tn == 0
    in_specs = [pl.BlockSpec((None, tm, k), lambda bi, i, j: (bi, i, 0)),
                pl.BlockSpec((k, tn), lambda bi, i, j: (0, j))]
    args = [a, w]
    if rope is None:
        body = _mm_kernel
        name = "matmul"
    else:
        cos, sina, sinb, q_cols = rope
        body = functools.partial(_mm_rope_kernel, tn=tn, q_cols=q_cols)
        in_specs += [pl.BlockSpec((tm, LANES), lambda bi, i, j: (i, 0))] * 3
        args += [cos, sina, sinb]
        name = "matmul_rope"
    return pl.pallas_call(
        body,
        grid=(b, s // tm, n // tn),
        in_specs=in_specs,
        out_specs=pl.BlockSpec((None, tm, tn), lambda bi, i, j: (bi, i, j)),
        out_shape=jax.ShapeDtypeStruct((b, s, n), out_dtype),
        compiler_params=_params("parallel", "parallel", "arbitrary"),
        name=name,
    )(*args)


def rope_tables(seq):
    half = ROT_DIM // 2
    pos = jnp.arange(seq, dtype=F32)
    inv = ROPE_THETA ** (-jnp.arange(0, ROT_DIM, 2, dtype=F32) / ROT_DIM)
    ang = pos[:, None] * inv[None, :]
    cos, sin = jnp.cos(ang), jnp.sin(ang)
    ones = jnp.ones((seq, ATT_HEAD_DIM - ROT_DIM), F32)
    zeros = jnp.zeros((seq, ATT_HEAD_DIM - ROT_DIM), F32)
    zh = jnp.zeros((seq, half), F32)
    cos64 = jnp.concatenate([cos, cos, ones], axis=1)
    sina64 = jnp.concatenate([-sin, zh, zeros], axis=1)
    sinb64 = jnp.concatenate([zh, sin, zeros], axis=1)
    rep = LANES // ATT_HEAD_DIM
    return (jnp.tile(cos64, (1, rep)), jnp.tile(sina64, (1, rep)), jnp.tile(sinb64, (1, rep)))


def _attn_kernel(q_ref, k_ref, v_ref, lq1_ref, lk1_ref, lq2_ref, lk2_ref, gain_ref, o_ref,
                 m_s, l_s, acc_s, *, tq, tk):
    qi = pl.program_id(2)
    q = q_ref[...]
    lane = lax.broadcasted_iota(jnp.int32, q.shape, 1)
    zero = jnp.zeros_like(q)
    q2 = jnp.concatenate([jnp.where(lane < ATT_HEAD_DIM, q, zero),
                          jnp.where(lane >= ATT_HEAD_DIM, q, zero)], axis=0)

    m_s[...] = jnp.full(m_s.shape, -jnp.inf, F32)
    l_s[...] = jnp.zeros(l_s.shape, F32)
    acc_s[...] = jnp.zeros(acc_s.shape, F32)

    def step(j, masked):
        k0 = pl.multiple_of(j * tk, tk)
        kb = k_ref[pl.ds(k0, tk), :]
        vb = v_ref[pl.ds(k0, tk), :]
        s = _dot_t(q2, kb)
        if masked:
            row = lax.broadcasted_iota(jnp.int32, (tq, tk), 0)
            col = lax.broadcasted_iota(jnp.int32, (tq, tk), 1)
            ok = (k0 + col) <= (qi * tq + row)
            ok = jnp.concatenate([ok, ok], axis=0)
            s = jnp.where(ok, s, -jnp.inf)
        m_old = m_s[...]
        m_new = jnp.maximum(m_old, jnp.max(s, axis=-1, keepdims=True))
        p = jnp.exp(s - m_new)
        corr = jnp.exp(m_old - m_new)
        l_s[...] = l_s[...] * corr + jnp.sum(p, axis=-1, keepdims=True)
        acc_s[...] = acc_s[...] * corr + _dot(p.astype(BF16), vb)
        m_s[...] = m_new

    n_full = (qi * tq) // tk

    def full_body(j, c):
        step(j, False)
        return c

    lax.fori_loop(0, n_full, full_body, 0)
    for d in range(tq // tk):
        step(n_full + d, True)

    lam = (jnp.exp(jnp.sum(lq1_ref[...] * lk1_ref[...], axis=-1, keepdims=True))
           - jnp.exp(jnp.sum(lq2_ref[...] * lk2_ref[...], axis=-1, keepdims=True))
           + LAMBDA_INIT)
    o = acc_s[...] / l_s[...]
    o = o[:tq] - lam * o[tq:]
    o = o * lax.rsqrt(jnp.mean(o * o, axis=-1, keepdims=True) + SUBLN_EPS)
    o_ref[...] = (o * gain_ref[...] * (1.0 - LAMBDA_INIT)).astype(o_ref.dtype)


def diff_attention(qk, v, v_col0, lq1, lk1, lq2, lk2, subln_gain, tq=512, tk=512):
    b, s, _ = qk.shape
    tq = min(tq, s)
    tk = min(tk, tq)
    h = ATT_HEADS
    vec = lambda a: a.reshape(1, -1).astype(F32)
    small = lambda n: pl.BlockSpec((1, n), lambda bi, hi, i: (0, 0))
    return pl.pallas_call(
        functools.partial(_attn_kernel, tq=tq, tk=tk),
        grid=(b, h, s // tq),
        in_specs=[pl.BlockSpec((None, tq, LANES), lambda bi, hi, i: (bi, i, hi)),
                  pl.BlockSpec((None, s, LANES), lambda bi, hi, i: (bi, 0, h + hi)),
                  pl.BlockSpec((None, s, LANES), lambda bi, hi, i: (bi, 0, v_col0 + hi)),
                  small(ATT_HEAD_DIM), small(ATT_HEAD_DIM), small(ATT_HEAD_DIM), small(ATT_HEAD_DIM),
                  small(LANES)],
        out_specs=pl.BlockSpec((None, tq, LANES), lambda bi, hi, i: (bi, i, hi)),
        out_shape=jax.ShapeDtypeStruct((b, s, h * LANES), BF16),
        scratch_shapes=[pltpu.VMEM((2 * tq, 1), F32), pltpu.VMEM((2 * tq, 1), F32),
                        pltpu.VMEM((2 * tq, LANES), F32)],
        compiler_params=_params("parallel", "parallel", "arbitrary"),
        name="diff_attention",
    )(qk, qk, v, vec(lq1), vec(lk1), vec(lq2), vec(lk2), vec(subln_gain))


def _tri_inverse(n_mat, c):
    row = lax.broadcasted_iota(jnp.int32, (c, c), 0)
    col = lax.broadcasted_iota(jnp.int32, (c, c), 1)
    t = (row == col).astype(F32)
    m = 1
    while m < c:
        sel = ((row // (2 * m)) == (col // (2 * m))) & ((row // m) % 2 == 1) & ((col // m) % 2 == 0)
        nl = jnp.where(sel, n_mat, 0.0)
        t = t + _dot(_dot(t, nl, HIGHEST), t, HIGHEST)
        m *= 2
    return t


def _rwkv_head(r, k, v, lw, ai, g, kk_w, ka_w, rk_w, lnw, lnb, h0, c):
    n = RWKV_HEAD_DIM
    kk = k * kk_w
    kk = kk / jnp.maximum(jnp.sqrt(jnp.sum(kk * kk, axis=-1, keepdims=True)), 1e-12)
    k2 = k * (1.0 + (ai - 1.0) * ka_w)
    a = -kk
    b = kk * ai

    row = lax.broadcasted_iota(jnp.int32, (c, c), 0)
    col = lax.broadcasted_iota(jnp.int32, (c, c), 1)
    lower = row >= col
    strict = row > col
    cum = _dot(lower.astype(F32), lw, HIGHEST)
    cum_last = cum[c - 1:c, :]
    e_cum = jnp.exp(cum)
    e_prev = jnp.exp(cum - lw)
    e_neg = jnp.exp(-cum)
    e_tail = jnp.exp(cum_last - cum)

    qq = jnp.concatenate([a * e_prev, r * e_cum], axis=0)
    kq = jnp.concatenate([b * e_neg, k2 * e_neg], axis=0)
    s = _dot_t(qq, kq, HIGHEST)
    a_ab = jnp.where(strict, s[:c, :c], 0.0)
    a_ak = jnp.where(strict, s[:c, c:], 0.0)
    a_rb = jnp.where(lower, s[c:, :c], 0.0)
    a_rk = jnp.where(lower, s[c:, c:], 0.0)

    t = _tri_inverse(a_ab, c)
    w_mat = _dot(t, qq[:c], HIGHEST)
    u0 = _dot(t, _dot(a_ak, v, HIGHEST), HIGHEST)
    p_mat = qq[c:] + _dot(a_rb, w_mat, HIGHEST)
    y0 = _dot(a_rb, u0, HIGHEST) + _dot(a_rk, v, HIGHEST)
    y = _dot(p_mat, h0, HIGHEST) + y0

    bhat_t = (b * e_tail).T
    khat_t = (k2 * e_tail).T
    eye = (lax.broadcasted_iota(jnp.int32, (n, n), 0) == lax.broadcasted_iota(jnp.int32, (n, n), 1)).astype(F32)
    m_mat = eye * e_cum[c - 1:c, :] + _dot(bhat_t, w_mat, HIGHEST)
    n_mat = _dot(bhat_t, u0, HIGHEST) + _dot(khat_t, v, HIGHEST)
    h_next = _dot(m_mat, h0, HIGHEST) + n_mat

    mu = jnp.mean(y, axis=-1, keepdims=True)
    var = jnp.mean((y - mu) ** 2, axis=-1, keepdims=True)
    yn = (y - mu) * lax.rsqrt(var + GN_EPS) * lnw + lnb
    bonus = jnp.sum(r * k2 * rk_w, axis=-1, keepdims=True) * v
    return (yn + bonus) * g, h_next


def _rwkv_kernel(rkv_ref, lora_ref, mu_rkv_ref, mu_lora_ref, w0_ref, wl_ref, a0_ref, al_ref, gl_ref,
                 kk_ref, ka_ref, rk_ref, lnw_ref, lnb_ref, o_ref,
                 state_s, prev_rkv_s, prev_lora_s, r_s, k_s, v_s, lw_s, ai_s, g_s, *, c):
    ci = pl.program_id(1)
    width = RWKV_HEADS * RWKV_HEAD_DIM

    @pl.when(ci == 0)
    def _():
        state_s[...] = jnp.zeros(state_s.shape, F32)
        prev_rkv_s[...] = jnp.zeros(prev_rkv_s.shape, F32)
        prev_lora_s[...] = jnp.zeros(prev_lora_s.shape, F32)

    def shift_mix(p, prev_s, mu):
        rolled = pltpu.roll(p, 1, 0)
        first = lax.broadcasted_iota(jnp.int32, p.shape, 0) == 0
        shifted = jnp.where(first, prev_s[7:8, :], rolled)
        prev_s[...] = p[c - 8:, :]
        return p + mu * (shifted - p)

    z = shift_mix(rkv_ref[...], prev_rkv_s, mu_rkv_ref[...])
    zl = shift_mix(lora_ref[...], prev_lora_s, mu_lora_ref[...])
    r_s[...] = z[:, :width]
    k_s[...] = z[:, width:2 * width]
    v_s[...] = z[:, 2 * width:]
    xw = zl[:, :DECAY_LORA]
    xa = zl[:, DECAY_LORA:DECAY_LORA + ICLR_LORA]
    xg = zl[:, DECAY_LORA + ICLR_LORA:DECAY_LORA + ICLR_LORA + GATE_LORA]
    w_logit = -jax.nn.softplus(-(w0_ref[...] + _dot(jnp.tanh(xw), wl_ref[...], HIGHEST))) - 0.5
    lw_s[...] = -jnp.exp(w_logit)
    ai_s[...] = jax.nn.sigmoid(a0_ref[...] + _dot(xa, al_ref[...], HIGHEST))
    g_s[...] = _dot(jax.nn.sigmoid(xg), gl_ref[...], HIGHEST)

    def pair_body(hp, carry):
        sl = pl.ds(pl.multiple_of(hp * LANES, LANES), LANES)
        blocks = [ref[:, sl] for ref in (r_s, k_s, v_s, lw_s, ai_s, g_s)]
        rows = [ref[:, sl] for ref in (kk_ref, ka_ref, rk_ref, lnw_ref, lnb_ref)]
        outs = []
        for half in range(LANES // RWKV_HEAD_DIM):
            lo = half * RWKV_HEAD_DIM
            hi = lo + RWKV_HEAD_DIM
            hidx = hp * (LANES // RWKV_HEAD_DIM) + half
            out, h_next = _rwkv_head(*[x[:, lo:hi] for x in blocks], *[x[:, lo:hi] for x in rows],
                                     state_s[hidx], c)
            state_s[hidx] = h_next
            outs.append(out)
        o_ref[:, sl] = jnp.concatenate(outs, axis=1).astype(o_ref.dtype)
        return carry

    lax.fori_loop(0, RWKV_HEADS * RWKV_HEAD_DIM // LANES, pair_body, 0)


def rwkv7(rkv, lora, mu_rkv, mu_lora, w0, wl, a0, al, gl, k_k, k_a, r_k, lnx_w, lnx_b):
    b, s, _ = rkv.shape
    c = min(RWKV_CHUNK, s)
    width = RWKV_HEADS * RWKV_HEAD_DIM
    n = RWKV_HEAD_DIM
    row = lambda a: a.reshape(1, -1).astype(F32)
    const = lambda shape: pl.BlockSpec(shape, lambda bi, ci: (0,) * len(shape))
    return pl.pallas_call(
        functools.partial(_rwkv_kernel, c=c),
        grid=(b, s // c),
        in_specs=[pl.BlockSpec((None, c, 3 * width), lambda bi, ci: (bi, ci, 0)),
                  pl.BlockSpec((None, c, LORA_PAD), lambda bi, ci: (bi, ci, 0)),
                  const((1, 3 * width)), const((1, LORA_PAD)),
                  const((1, width)), const((DECAY_LORA, width)),
                  const((1, width)), const((ICLR_LORA, width)), const((GATE_LORA, width)),
                  const((1, width)), const((1, width)), const((1, width)),
                  const((1, width)), const((1, width))],
        out_specs=pl.BlockSpec((None, c, width), lambda bi, ci: (bi, ci, 0)),
        out_shape=jax.ShapeDtypeStruct((b, s, width), BF16),
        scratch_shapes=[pltpu.VMEM((RWKV_HEADS, n, n), F32),
                        pltpu.VMEM((8, 3 * width), F32), pltpu.VMEM((8, LORA_PAD), F32)]
                       + [pltpu.VMEM((c, width), F32)] * 6,
        compiler_params=_params("parallel", "arbitrary"),
        name="rwkv7",
    )(rkv, lora, row(mu_rkv), row(mu_lora), row(w0), wl.astype(F32), row(a0), al.astype(F32),
      gl.astype(F32), row(k_k), row(k_a), row(r_k), row(lnx_w), row(lnx_b))


def _merge_kernel(ya_ref, yr_ref, g0_ref, g1_ref, wa_ref, wr_ref, o_ref):
    za = _dot(ya_ref[...], wa_ref[...])
    zr = _dot(yr_ref[...], wr_ref[...])
    ga = jax.nn.sigmoid(g0_ref[...].astype(F32))
    gr = jax.nn.sigmoid(g1_ref[...].astype(F32))
    o_ref[...] = (ga * za + gr * zr).astype(o_ref.dtype)


def gated_merge(y_att, y_rwkv, vg, gate_col0, wa, wr, tm=1024, tn=512):
    b, s, kdim = y_att.shape
    n = wa.shape[1]
    tm = min(tm, s)
    nb = n // tn
    return pl.pallas_call(
        _merge_kernel,
        grid=(b, s // tm, nb),
        in_specs=[pl.BlockSpec((None, tm, kdim), lambda bi, i, j: (bi, i, 0)),
                  pl.BlockSpec((None, tm, kdim), lambda bi, i, j: (bi, i, 0)),
                  pl.BlockSpec((None, tm, tn), lambda bi, i, j: (bi, i, gate_col0 + j)),
                  pl.BlockSpec((None, tm, tn), lambda bi, i, j: (bi, i, gate_col0 + nb + j)),
                  pl.BlockSpec((kdim, tn), lambda bi, i, j: (0, j)),
                  pl.BlockSpec((kdim, tn), lambda bi, i, j: (0, j))],
        out_specs=pl.BlockSpec((None, tm, tn), lambda bi, i, j: (bi, i, j)),
        out_shape=jax.ShapeDtypeStruct((b, s, n), BF16),
        compiler_params=_params("parallel", "parallel", "arbitrary"),
        name="gated_merge",
    )(y_att, y_rwkv, vg, vg, wa, wr)


def _out_proj_kernel(m_ref, w_ref, x_ref, mod_ref, gain_ref, x1_ref, h2_ref):
    x1 = x_ref[...] + mod_ref[2:3, :] * _dot(m_ref[...], w_ref[...])
    x1_ref[...] = x1
    xn = x1 * lax.rsqrt(jnp.mean(x1 * x1, axis=-1, keepdims=True) + NORM_EPS)
    h2_ref[...] = (xn * gain_ref[...] * (1.0 + mod_ref[4:5, :]) + mod_ref[3:4, :]).astype(h2_ref.dtype)


def out_proj(merged, w_out, x, mod, gain2, tm=256):
    b, s, d = x.shape
    tm = min(tm, s)
    return pl.pallas_call(
        _out_proj_kernel,
        grid=(b, s // tm),
        in_specs=[pl.BlockSpec((None, tm, d), lambda bi, i: (bi, i, 0)),
                  pl.BlockSpec((d, d), lambda bi, i: (0, 0)),
                  pl.BlockSpec((None, tm, d), lambda bi, i: (bi, i, 0)),
                  pl.BlockSpec((None, 8, d), lambda bi, i: (bi, 0, 0)),
                  pl.BlockSpec((1, d), lambda bi, i: (0, 0))],
        out_specs=[pl.BlockSpec((None, tm, d), lambda bi, i: (bi, i, 0)),
                   pl.BlockSpec((None, tm, d), lambda bi, i: (bi, i, 0))],
        out_shape=[jax.ShapeDtypeStruct((b, s, d), F32), jax.ShapeDtypeStruct((b, s, d), BF16)],
        compiler_params=_params("parallel", "parallel"),
        name="out_proj",
    )(merged, w_out, x, mod, gain2.reshape(1, d))


def _glu_down_kernel(ug_ref, halo_ref, uv_ref, cw_ref, cb_ref, wd_ref, x1_ref, mod_ref, gain_ref, o_ref,
                     acc_s, *, tm):
    i = pl.program_id(1)
    f = pl.program_id(2)

    @pl.when(f == 0)
    def _():
        acc_s[...] = jnp.zeros(acc_s.shape, F32)

    ug = ug_ref[...].astype(F32)
    halo = halo_ref[...].astype(F32) * (i > 0).astype(F32)
    row8 = lax.broadcasted_iota(jnp.int32, halo.shape, 0)

    def shifted(k):
        rolled = pltpu.roll(ug, k, 0)
        top = jnp.where(row8 < k, pltpu.roll(halo, k, 0), rolled[:8])
        return jnp.concatenate([top, rolled[8:]], axis=0)

    cw = cw_ref[...]
    conv = cb_ref[...] + shifted(2) * cw[0:1, :] + shifted(1) * cw[1:2, :] + ug * cw[2:3, :]
    gelu = 0.5 * conv * (1.0 + lax.erf(conv * (2.0 ** -0.5)))
    act = (gelu * uv_ref[...].astype(F32)).astype(BF16)
    acc_s[...] += _dot(act, wd_ref[...])

    @pl.when(f == pl.num_programs(2) - 1)
    def _():
        x2 = x1_ref[...] + mod_ref[5:6, :] * acc_s[...]
        o_ref[...] = x2 * lax.rsqrt(jnp.mean(x2 * x2, axis=-1, keepdims=True) + NORM_EPS) * gain_ref[...]


def glu_down(u, conv_w, conv_b, w_down, x1, mod, final_gain, tm=512, tf=512):
    b, s, d = x1.shape
    ff = w_down.shape[0]
    tm = min(tm, s)
    nf = ff // tf
    cw8 = jnp.zeros((8, ff), F32).at[:conv_w.shape[0]].set(conv_w)
    hb = tm // 8
    return pl.pallas_call(
        functools.partial(_glu_down_kernel, tm=tm),
        grid=(b, s // tm, nf),
        in_specs=[pl.BlockSpec((None, tm, tf), lambda bi, i, f: (bi, i, f)),
                  pl.BlockSpec((None, 8, tf), lambda bi, i, f: (bi, jnp.maximum(i * hb - 1, 0), f)),
                  pl.BlockSpec((None, tm, tf), lambda bi, i, f: (bi, i, nf + f)),
                  pl.BlockSpec((8, tf), lambda bi, i, f: (0, f)),
                  pl.BlockSpec((1, tf), lambda bi, i, f: (0, f)),
                  pl.BlockSpec((tf, d), lambda bi, i, f: (f, 0)),
                  pl.BlockSpec((None, tm, d), lambda bi, i, f: (bi, i, 0)),
                  pl.BlockSpec((None, 8, d), lambda bi, i, f: (bi, 0, 0)),
                  pl.BlockSpec((1, d), lambda bi, i, f: (0, 0))],
        out_specs=pl.BlockSpec((None, tm, d), lambda bi, i, f: (bi, i, 0)),
        out_shape=jax.ShapeDtypeStruct((b, s, d), F32),
        scratch_shapes=[pltpu.VMEM((tm, d), F32)],
        compiler_params=_params("parallel", "parallel", "arbitrary"),
        name="glu_down",
    )(u, u, u, cw8, conv_b.reshape(1, ff), w_down, x1, mod, final_gain.reshape(1, d))


def kernel(x, c, w_ada, b_ada, norm1_gain, w_in, lambda_q1, lambda_k1, lambda_q2, lambda_k2, subln_gain, mu_shift, w0, w_lora_up, a0, a_lora_up, g_lora_up, k_k, k_a, r_k, lnx_w, lnx_b, w_branch, w_out, norm2_gain, w_up, conv_w, conv_b, w_down, final_gain):
    b, s, d = x.shape
    depth = w_in.shape[0]
    assert depth == 1
    l = 0
    qk_cols = 2 * ATT_HEADS * 2 * ATT_HEAD_DIM
    v_cols = ATT_HEADS * 2 * ATT_HEAD_DIM
    width = RWKV_HEADS * RWKV_HEAD_DIM
    n_lora = DECAY_LORA + ICLR_LORA + GATE_LORA
    rkv0 = qk_cols + v_cols
    lora0 = rkv0 + 3 * width
    gate0 = lora0 + n_lora

    c8 = jnp.zeros((8, d), F32).at[:b].set(c)
    mod = ada_modulation(c8, w_ada[l], b_ada[l])[:b]
    mod = jnp.pad(mod.reshape(b, 6, d), ((0, 0), (0, 2), (0, 0)))

    h = norm_mod(x, mod, norm1_gain[l], shift_row=0, scale_row=1)

    w_in_l = w_in[l]
    w_qk = w_in_l[:, :qk_cols].astype(BF16)
    w_vg = jnp.concatenate([w_in_l[:, qk_cols:rkv0], w_in_l[:, gate0:]], axis=1).astype(BF16)
    w_rkv = w_in_l[:, rkv0:lora0].astype(BF16)
    w_lora = jnp.pad(w_in_l[:, lora0:gate0], ((0, 0), (0, LORA_PAD - n_lora))).astype(BF16)

    cos, sina, sinb = rope_tables(s)
    qk = matmul(h, w_qk, BF16, rope=(cos, sina, sinb, qk_cols // 2))
    vg = matmul(h, w_vg, BF16)
    rkv = matmul(h, w_rkv, F32)
    lora = matmul(h, w_lora, F32, tn=LORA_PAD)

    y_att = diff_attention(qk, vg, 0, lambda_q1[l], lambda_k1[l], lambda_q2[l], lambda_k2[l], subln_gain[l])

    mu = mu_shift[l]
    mu_lora = jnp.pad(mu[3 * width:], (0, LORA_PAD - n_lora))
    y_rwkv = rwkv7(rkv, lora, mu[:3 * width], mu_lora, w0[l], w_lora_up[l], a0[l], a_lora_up[l],
                   g_lora_up[l], k_k[l], k_a[l], r_k[l].reshape(-1), lnx_w[l], lnx_b[l])

    tn_merge = 512
    merged = gated_merge(y_att, y_rwkv, vg, v_cols // tn_merge, w_branch[l, 0].astype(BF16),
                         w_branch[l, 1].astype(BF16), tn=tn_merge)
    x1, h2 = out_proj(merged, w_out[l].astype(BF16), x, mod, norm2_gain[l])

    u = matmul(h2, w_up[l].astype(BF16), BF16)
    return glu_down(u, conv_w[l], conv_b[l], w_down[l].astype(BF16), x1, mod, final_gain)
```

```python
import functools
import math

import jax
import jax.numpy as jnp
from jax import lax
from jax.experimental import pallas as pl
from jax.experimental.pallas import tpu as pltpu

F32 = jnp.float32
BF16 = jnp.bfloat16
HIGHEST = lax.Precision.HIGHEST

ATT_HEADS = 8
ATT_HEAD_DIM = 64
ROT_DIM = ATT_HEAD_DIM // 4
ROPE_THETA = 500000.0
SUBLN_EPS = 1e-5
RWKV_HEADS = 16
RWKV_HEAD_DIM = 64
DECAY_LORA = 64
ICLR_LORA = 64
GATE_LORA = 160
GN_EPS = 64e-5
NORM_EPS = 1e-6
LAMBDA_INIT = 0.8 - 0.6 * math.exp(-0.3 * 0)

LANES = 128
VMEM_LIMIT = 48 * 1024 * 1024
RWKV_CHUNK = 64
RWKV_GROUP_BLOCKS = 8
LORA_PAD = 384


def _params(*sem):
    return pltpu.CompilerParams(dimension_semantics=sem, vmem_limit_bytes=VMEM_LIMIT)


def _dot(a, b, precision=None):
    return jnp.dot(a, b, preferred_element_type=F32, precision=precision)


def _dot_t(a, b, precision=None):
    return lax.dot_general(a, b, (((1,), (1,)), ((), ())),
                           preferred_element_type=F32, precision=precision)


def _mod_kernel(c_ref, w_ref, b_ref, o_ref):
    c = c_ref[...]
    s = c * jax.nn.sigmoid(c)
    o_ref[...] = _dot(s, w_ref[...], HIGHEST) + b_ref[...]


def ada_modulation(c8, w_ada, b_ada, tn=1024):
    m, d = c8.shape
    n = w_ada.shape[1]
    return pl.pallas_call(
        _mod_kernel,
        grid=(n // tn,),
        in_specs=[pl.BlockSpec((m, d), lambda j: (0, 0)),
                  pl.BlockSpec((d, tn), lambda j: (0, j)),
                  pl.BlockSpec((1, tn), lambda j: (0, j))],
        out_specs=pl.BlockSpec((m, tn), lambda j: (0, j)),
        out_shape=jax.ShapeDtypeStruct((m, n), F32),
        compiler_params=_params("arbitrary"),
        name="ada_mod",
    )(c8, w_ada, b_ada.reshape(1, n))


def _norm_mod_kernel(x_ref, mod_ref, gain_ref, o_ref, *, shift_row, scale_row):
    x = x_ref[...]
    xn = x * lax.rsqrt(jnp.mean(x * x, axis=-1, keepdims=True) + NORM_EPS)
    h = xn * gain_ref[...] * (1.0 + mod_ref[scale_row:scale_row + 1, :]) + mod_ref[shift_row:shift_row + 1, :]
    o_ref[...] = h.astype(o_ref.dtype)


def norm_mod(x, mod, gain, shift_row, scale_row, tm=512):
    b, s, d = x.shape
    tm = min(tm, s)
    return pl.pallas_call(
        functools.partial(_norm_mod_kernel, shift_row=shift_row, scale_row=scale_row),
        grid=(b, s // tm),
        in_specs=[pl.BlockSpec((None, tm, d), lambda bi, i: (bi, i, 0)),
                  pl.BlockSpec((None, 8, d), lambda bi, i: (bi, 0, 0)),
                  pl.BlockSpec((1, d), lambda bi, i: (0, 0))],
        out_specs=pl.BlockSpec((None, tm, d), lambda bi, i: (bi, i, 0)),
        out_shape=jax.ShapeDtypeStruct((b, s, d), BF16),
        compiler_params=_params("parallel", "parallel"),
        name="norm_mod",
    )(x, mod, gain.reshape(1, d))


def _mm_kernel(a_ref, w_ref, o_ref):
    o_ref[...] = _dot(a_ref[...], w_ref[...]).astype(o_ref.dtype)


def _mm_rope_kernel(a_ref, w_ref, cos_ref, sina_ref, sinb_ref, o_ref, *, tn, q_cols):
    acc = _dot(a_ref[...], w_ref[...])
    scale = jnp.where(pl.program_id(2) * tn < q_cols, ATT_HEAD_DIM ** -0.5 * math.log2(math.e), 1.0).astype(F32)
    cos, sina, sinb = cos_ref[...], sina_ref[...], sinb_ref[...]
    for j in range(tn // LANES):
        t = acc[:, j * LANES:(j + 1) * LANES]
        half = ROT_DIM // 2
        r = t * cos + pltpu.roll(t, LANES - half, 1) * sina + pltpu.roll(t, half, 1) * sinb
        o_ref[:, j * LANES:(j + 1) * LANES] = (r * scale).astype(o_ref.dtype)


def matmul(a, w, out_dtype, tm=1024, tn=512, rope=None):
    b, s, k = a.shape
    n = w.shape[1]
    tm = min(tm, s)
    tn = min(tn, n)
    assert s % tm == 0 and n % tn == 0
    in_specs = [pl.BlockSpec((None, tm, k), lambda bi, i, j: (bi, i, 0)),
                pl.BlockSpec((k, tn), lambda bi, i, j: (0, j))]
    args = [a, w]
    if rope is None:
        body = _mm_kernel
        name = "matmul"
    else:
        cos, sina, sinb, q_cols = rope
        body = functools.partial(_mm_rope_kernel, tn=tn, q_cols=q_cols)
        in_specs += [pl.BlockSpec((tm, LANES), lambda bi, i, j: (i, 0))] * 3
        args += [cos, sina, sinb]
        name = "matmul_rope"
    return pl.pallas_call(
        body,
        grid=(b, s // tm, n // tn),
        in_specs=in_specs,
        out_specs=pl.BlockSpec((None, tm, tn), lambda bi, i, j: (bi, i, j)),
        out_shape=jax.ShapeDtypeStruct((b, s, n), out_dtype),
        compiler_params=_params("parallel", "parallel", "arbitrary"),
        name=name,
    )(*args)


def rope_tables(seq):
    half = ROT_DIM // 2
    pos = jnp.arange(seq, dtype=F32)
    inv = ROPE_THETA ** (-jnp.arange(0, ROT_DIM, 2, dtype=F32) / ROT_DIM)
    ang = pos[:, None] * inv[None, :]
    cos, sin = jnp.cos(ang), jnp.sin(ang)
    ones = jnp.ones((seq, ATT_HEAD_DIM - ROT_DIM), F32)
    zeros = jnp.zeros((seq, ATT_HEAD_DIM - ROT_DIM), F32)
    zh = jnp.zeros((seq, half), F32)
    cos64 = jnp.concatenate([cos, cos, ones], axis=1)
    sina64 = jnp.concatenate([-sin, zh, zeros], axis=1)
    sinb64 = jnp.concatenate([zh, sin, zeros], axis=1)
    rep = LANES // ATT_HEAD_DIM
    return (jnp.tile(cos64, (1, rep)), jnp.tile(sina64, (1, rep)), jnp.tile(sinb64, (1, rep)))


SUBLANES = 8


def _sublane_all(x, op):
    for shift in (4, 2, 1):
        x = op(x, pltpu.roll(x, shift, 0))
    return x


def _attn_kernel(qt_ref, k_ref, vt_ref, lq1_ref, lk1_ref, lq2_ref, lk2_ref, gain_ref, o_ref,
                 acc_s, *, tq, tk):
    qi = pl.program_id(2)
    qt = qt_ref[...]
    row = lax.broadcasted_iota(jnp.int32, qt.shape, 0)
    zero = jnp.zeros_like(qt)
    q2 = jnp.concatenate([jnp.where(row < ATT_HEAD_DIM, qt, zero),
                          jnp.where(row >= ATT_HEAD_DIM, qt, zero)], axis=1)
    n2 = 2 * tq
    acc_s[...] = jnp.zeros(acc_s.shape, F32)

    def step(j, m_old, l_old, masked):
        k0 = pl.multiple_of(j * tk, tk)
        kb = k_ref[pl.ds(k0, tk), :]
        vtb = vt_ref[:, pl.ds(k0, tk)]
        s = _dot(kb, q2)
        if masked:
            kpos = k0 + lax.broadcasted_iota(jnp.int32, (tk, tq), 0)
            qpos = qi * tq + lax.broadcasted_iota(jnp.int32, (tk, tq), 1)
            ok = kpos <= qpos
            s = jnp.where(jnp.concatenate([ok, ok], axis=1), s, -jnp.inf)
        s3 = s.reshape(tk // SUBLANES, SUBLANES, n2)
        m_new = jnp.maximum(m_old, _sublane_all(jnp.max(s3, axis=0), jnp.maximum))
        p3 = jnp.exp2(s3 - m_new[None])
        corr = jnp.exp2(m_old - m_new)
        l_new = l_old * corr + jnp.sum(p3, axis=0)
        pv = _dot(vtb, p3.reshape(tk, n2).astype(BF16))
        acc3 = acc_s[...].reshape(LANES // SUBLANES, SUBLANES, n2)
        acc_s[...] = (acc3 * corr[None]).reshape(LANES, n2) + pv
        return m_new, l_new

    n_full = (qi * tq) // tk
    m0 = jnp.full((SUBLANES, n2), -jnp.inf, F32)
    l0 = jnp.zeros((SUBLANES, n2), F32)
    m, l = lax.fori_loop(0, n_full, lambda j, c: step(j, c[0], c[1], False), (m0, l0))
    for d in range(tq // tk):
        m, l = step(n_full + d, m, l, True)

    lam = (jnp.exp(jnp.sum(lq1_ref[...] * lk1_ref[...], axis=-1, keepdims=True))
           - jnp.exp(jnp.sum(lq2_ref[...] * lk2_ref[...], axis=-1, keepdims=True))
           + LAMBDA_INIT)
    l = _sublane_all(l, jnp.add)
    o3 = acc_s[...].reshape(LANES // SUBLANES, SUBLANES, n2) / l[None]
    d3 = o3[:, :, :tq] - lam * o3[:, :, tq:]
    ms = _sublane_all(jnp.sum(d3 * d3, axis=0), jnp.add) * (1.0 / LANES)
    d3 = d3 * lax.rsqrt(ms + SUBLN_EPS)[None]
    o = d3.reshape(LANES, tq).T
    o_ref[...] = (o * gain_ref[...] * (1.0 - LAMBDA_INIT)).astype(o_ref.dtype)


def diff_attention(qt, kk, k_col0, vt, lq1, lk1, lq2, lk2, subln_gain, tq=512, tk=512):
    b, h, _, s = qt.shape
    tq = min(tq, s)
    tk = min(tk, tq)
    vec = lambda a: a.reshape(1, -1).astype(F32)
    small = lambda n: pl.BlockSpec((1, n), lambda bi, hi, i: (0, 0))
    return pl.pallas_call(
        functools.partial(_attn_kernel, tq=tq, tk=tk),
        grid=(b, h, s // tq),
        in_specs=[pl.BlockSpec((None, None, LANES, tq), lambda bi, hi, i: (bi, hi, 0, i)),
                  pl.BlockSpec((None, s, LANES), lambda bi, hi, i: (bi, 0, k_col0 + hi)),
                  pl.BlockSpec((None, None, LANES, s), lambda bi, hi, i: (bi, hi, 0, 0)),
                  small(ATT_HEAD_DIM), small(ATT_HEAD_DIM), small(ATT_HEAD_DIM), small(ATT_HEAD_DIM),
                  small(LANES)],
        out_specs=pl.BlockSpec((None, tq, LANES), lambda bi, hi, i: (bi, i, hi)),
        out_shape=jax.ShapeDtypeStruct((b, s, h * LANES), BF16),
        scratch_shapes=[pltpu.VMEM((LANES, 2 * tq), F32)],
        compiler_params=_params("parallel", "parallel", "arbitrary"),
        name="diff_attention",
    )(qt, kk, vt, vec(lq1), vec(lk1), vec(lq2), vec(lk2), vec(subln_gain))


def _bdot(a, b):
    return jnp.dot(a.astype(BF16), b.astype(BF16), preferred_element_type=F32)


def _each(f, *lists):
    return [f(*xs) for xs in zip(*lists)]


def _rwkv_heads(heads, h0s, c):
    n = RWKV_HEAD_DIM
    row = lax.broadcasted_iota(jnp.int32, (c, c), 0)
    col = lax.broadcasted_iota(jnp.int32, (c, c), 1)
    lower = row >= col
    strict = row > col
    lower_b = lower.astype(BF16)
    eye_c = (row == col).astype(F32)
    eye_n = (lax.broadcasted_iota(jnp.int32, (n, n), 0) == lax.broadcasted_iota(jnp.int32, (n, n), 1)).astype(F32)

    r, k, v, lw, ai, g, kk_w, ka_w, rk_w, lnw, lnb = [list(x) for x in zip(*heads)]

    def keys(k, ai, kk_w, ka_w):
        kk = k * kk_w
        kk = kk / jnp.maximum(jnp.sqrt(jnp.sum(kk * kk, axis=-1, keepdims=True)), 1e-12)
        return -kk, kk * ai, k * (1.0 + (ai - 1.0) * ka_w)

    a, b, k2 = [list(x) for x in zip(*_each(keys, k, ai, kk_w, ka_w))]

    def cumsum(lw):
        hi = lw.astype(BF16)
        r1 = lw - hi.astype(F32)
        mid = r1.astype(BF16)
        lo = (r1 - mid.astype(F32)).astype(BF16)
        c3 = jnp.dot(lower_b, jnp.concatenate([hi, mid, lo], axis=1), preferred_element_type=F32)
        return c3[:, :n] + c3[:, n:2 * n] + c3[:, 2 * n:]

    cum = _each(cumsum, lw)
    e_cum = _each(jnp.exp, cum)
    e_prev = _each(lambda cm, l: jnp.exp(cm - l), cum, lw)
    e_neg = _each(lambda cm: jnp.exp(-cm), cum)
    e_tail = _each(lambda cm: jnp.exp(cm[c - 1:c, :] - cm), cum)

    at = _each(lambda a, e: a * e, a, e_prev)
    rt = _each(lambda r, e: r * e, r, e_cum)
    qq = _each(lambda x, y: jnp.concatenate([x, y], axis=0).astype(BF16), at, rt)
    kq = _each(lambda b, k2, e: jnp.concatenate([b * e, k2 * e], axis=0).astype(BF16), b, k2, e_neg)
    s = _each(lambda q, kk_: _dot_t(q, kk_), qq, kq)
    a_ab = _each(lambda s: jnp.where(strict, s[:c, :c], 0.0), s)
    a_ak = _each(lambda s: jnp.where(strict, s[:c, c:], 0.0), s)
    row2 = lax.broadcasted_iota(jnp.int32, (c, 2 * c), 0)
    col2 = lax.broadcasted_iota(jnp.int32, (c, 2 * c), 1)
    lower2 = row2 >= jnp.where(col2 >= c, col2 - c, col2)
    lo_blk = _each(lambda s: jnp.where(lower2, s[c:, :], 0.0), s)

    t = [eye_c for _ in heads]
    m = 1
    while m < c:
        sel = ((row // (2 * m)) == (col // (2 * m))) & ((row // m) % 2 == 1) & ((col // m) % 2 == 0)
        nl = _each(lambda x: jnp.where(sel, x, 0.0), a_ab)
        tn = _each(_bdot, t, nl)
        t = _each(lambda t, x: t + _bdot(x, t), t, tn)
        m *= 2

    x1 = _each(_bdot, a_ak, v)
    wu = _each(lambda t, at, x1: _bdot(t, jnp.concatenate([at, x1], axis=1)), t, at, x1)
    bk_t = _each(lambda b, k2, e: jnp.concatenate([(b * e).T, (k2 * e).T], axis=1), b, k2, e_tail)
    lhs_b = _each(lambda lb, bk: jnp.concatenate([lb[:, :c], bk[:, :c]], axis=0), lo_blk, bk_t)
    lhs_k = _each(lambda lb, bk: jnp.concatenate([lb[:, c:], bk[:, c:]], axis=0), lo_blk, bk_t)
    z1 = _each(_bdot, lhs_b, wu)
    z2 = _each(_bdot, lhs_k, v)
    pm = _each(lambda z1, rt, e: jnp.concatenate([rt + z1[:c, :n], eye_n * e[c - 1:c, :] + z1[c:, :n]], axis=0),
               z1, rt, e_cum)
    yn0 = _each(lambda z1, z2: z1[:, n:] + z2, z1, z2)
    yh = _each(lambda pm, h0, yn0: _bdot(pm, h0) + yn0, pm, h0s, yn0)

    def finish(yh, r, k2, v, g, rk_w, lnw, lnb):
        y = yh[:c]
        mu = jnp.mean(y, axis=-1, keepdims=True)
        var = jnp.mean((y - mu) ** 2, axis=-1, keepdims=True)
        yn = (y - mu) * lax.rsqrt(var + GN_EPS) * lnw + lnb
        bonus = jnp.sum(r * k2 * rk_w, axis=-1, keepdims=True) * v
        return (yn + bonus) * g

    outs = _each(finish, yh, r, k2, v, g, rk_w, lnw, lnb)
    return outs, [x[c:] for x in yh]


def _rwkv_kernel(rkv_ref, lora_ref, mu_rkv_ref, mu_lora_ref, w0_ref, wl_ref, a0_ref, al_ref, gl_ref,
                 kk_ref, ka_ref, rk_ref, lnw_ref, lnb_ref, o_ref,
                 state_s, prev_rkv_s, prev_lora_s, r_s, k_s, v_s, lw_s, ai_s, g_s, *, c):
    ci = pl.program_id(1)
    width = RWKV_HEADS * RWKV_HEAD_DIM

    @pl.when(ci == 0)
    def _():
        state_s[...] = jnp.zeros(state_s.shape, F32)
        prev_rkv_s[...] = jnp.zeros(prev_rkv_s.shape, F32)
        prev_lora_s[...] = jnp.zeros(prev_lora_s.shape, F32)

    def shift_mix(p, prev_s, mu):
        rolled = pltpu.roll(p, 1, 0)
        first = lax.broadcasted_iota(jnp.int32, p.shape, 0) == 0
        shifted = jnp.where(first, prev_s[7:8, :], rolled)
        prev_s[...] = p[c - 8:, :]
        return p + mu * (shifted - p)

    z = shift_mix(rkv_ref[...], prev_rkv_s, mu_rkv_ref[...])
    zl = shift_mix(lora_ref[...], prev_lora_s, mu_lora_ref[...])
    r_s[...] = z[:, :width]
    k_s[...] = z[:, width:2 * width]
    v_s[...] = z[:, 2 * width:]
    xw = zl[:, :DECAY_LORA]
    xa = zl[:, DECAY_LORA:DECAY_LORA + ICLR_LORA]
    xg = zl[:, DECAY_LORA + ICLR_LORA:DECAY_LORA + ICLR_LORA + GATE_LORA]
    w_logit = -jax.nn.softplus(-(w0_ref[...] + _dot(jnp.tanh(xw), wl_ref[...], HIGHEST))) - 0.5
    lw_s[...] = -jnp.exp(w_logit)
    ai_s[...] = jax.nn.sigmoid(a0_ref[...] + _dot(xa, al_ref[...], HIGHEST))
    g_s[...] = _dot(jax.nn.sigmoid(xg), gl_ref[...], HIGHEST)

    per_block = LANES // RWKV_HEAD_DIM

    def group_body(gi, carry):
        heads, h0s = [], []
        for blk in range(RWKV_GROUP_BLOCKS):
            sl = pl.ds(pl.multiple_of((gi * RWKV_GROUP_BLOCKS + blk) * LANES, LANES), LANES)
            blocks = [ref[:, sl] for ref in (r_s, k_s, v_s, lw_s, ai_s, g_s,
                                             kk_ref, ka_ref, rk_ref, lnw_ref, lnb_ref)]
            for half in range(per_block):
                lo = half * RWKV_HEAD_DIM
                heads.append(tuple(x[:, lo:lo + RWKV_HEAD_DIM] for x in blocks))
                h0s.append(state_s[(gi * RWKV_GROUP_BLOCKS + blk) * per_block + half])
        outs, h_next = _rwkv_heads(heads, h0s, c)
        for blk in range(RWKV_GROUP_BLOCKS):
            sl = pl.ds(pl.multiple_of((gi * RWKV_GROUP_BLOCKS + blk) * LANES, LANES), LANES)
            o_ref[:, sl] = jnp.concatenate(outs[blk * per_block:(blk + 1) * per_block],
                                           axis=1).astype(o_ref.dtype)
            for half in range(per_block):
                state_s[(gi * RWKV_GROUP_BLOCKS + blk) * per_block + half] = h_next[blk * per_block + half]
        return carry

    lax.fori_loop(0, width // (LANES * RWKV_GROUP_BLOCKS), group_body, 0)


def rwkv7(rkv, lora, mu_rkv, mu_lora, w0, wl, a0, al, gl, k_k, k_a, r_k, lnx_w, lnx_b):
    b, s, _ = rkv.shape
    c = min(RWKV_CHUNK, s)
    width = RWKV_HEADS * RWKV_HEAD_DIM
    n = RWKV_HEAD_DIM
    row = lambda a: a.reshape(1, -1).astype(F32)
    const = lambda shape: pl.BlockSpec(shape, lambda bi, ci: (0,) * len(shape))
    return pl.pallas_call(
        functools.partial(_rwkv_kernel, c=c),
        grid=(b, s // c),
        in_specs=[pl.BlockSpec((None, c, 3 * width), lambda bi, ci: (bi, ci, 0)),
                  pl.BlockSpec((None, c, LORA_PAD), lambda bi, ci: (bi, ci, 0)),
                  const((1, 3 * width)), const((1, LORA_PAD)),
                  const((1, width)), const((DECAY_LORA, width)),
                  const((1, width)), const((ICLR_LORA, width)), const((GATE_LORA, width)),
                  const((1, width)), const((1, width)), const((1, width)),
                  const((1, width)), const((1, width))],
        out_specs=pl.BlockSpec((None, c, width), lambda bi, ci: (bi, ci, 0)),
        out_shape=jax.ShapeDtypeStruct((b, s, width), BF16),
        scratch_shapes=[pltpu.VMEM((RWKV_HEADS, n, n), F32),
                        pltpu.VMEM((8, 3 * width), F32), pltpu.VMEM((8, LORA_PAD), F32)]
                       + [pltpu.VMEM((c, width), F32)] * 6,
        compiler_params=_params("parallel", "arbitrary"),
        name="rwkv7",
    )(rkv, lora, row(mu_rkv), row(mu_lora), row(w0), wl.astype(F32), row(a0), al.astype(F32),
      gl.astype(F32), row(k_k), row(k_a), row(r_k), row(lnx_w), row(lnx_b))


def _merge_kernel(ya_ref, yr_ref, g0_ref, g1_ref, wa_ref, wr_ref, o_ref):
    za = _dot(ya_ref[...], wa_ref[...])
    zr = _dot(yr_ref[...], wr_ref[...])
    ga = jax.nn.sigmoid(g0_ref[...].astype(F32))
    gr = jax.nn.sigmoid(g1_ref[...].astype(F32))
    o_ref[...] = (ga * za + gr * zr).astype(o_ref.dtype)


def gated_merge(y_att, y_rwkv, vg, gate_col0, wa, wr, tm=1024, tn=512):
    b, s, kdim = y_att.shape
    n = wa.shape[1]
    tm = min(tm, s)
    nb = n // tn
    return pl.pallas_call(
        _merge_kernel,
        grid=(b, s // tm, nb),
        in_specs=[pl.BlockSpec((None, tm, kdim), lambda bi, i, j: (bi, i, 0)),
                  pl.BlockSpec((None, tm, kdim), lambda bi, i, j: (bi, i, 0)),
                  pl.BlockSpec((None, tm, tn), lambda bi, i, j: (bi, i, gate_col0 + j)),
                  pl.BlockSpec((None, tm, tn), lambda bi, i, j: (bi, i, gate_col0 + nb + j)),
                  pl.BlockSpec((kdim, tn), lambda bi, i, j: (0, j)),
                  pl.BlockSpec((kdim, tn), lambda bi, i, j: (0, j))],
        out_specs=pl.BlockSpec((None, tm, tn), lambda bi, i, j: (bi, i, j)),
        out_shape=jax.ShapeDtypeStruct((b, s, n), BF16),
        compiler_params=_params("parallel", "parallel", "arbitrary"),
        name="gated_merge",
    )(y_att, y_rwkv, vg, vg, wa, wr)


def _out_proj_kernel(m_ref, w_ref, x_ref, mod_ref, gain_ref, x1_ref, h2_ref):
    x1 = x_ref[...] + mod_ref[2:3, :] * _dot(m_ref[...], w_ref[...])
    x1_ref[...] = x1
    xn = x1 * lax.rsqrt(jnp.mean(x1 * x1, axis=-1, keepdims=True) + NORM_EPS)
    h2_ref[...] = (xn * gain_ref[...] * (1.0 + mod_ref[4:5, :]) + mod_ref[3:4, :]).astype(h2_ref.dtype)


def out_proj(merged, w_out, x, mod, gain2, tm=256):
    b, s, d = x.shape
    tm = min(tm, s)
    return pl.pallas_call(
        _out_proj_kernel,
        grid=(b, s // tm),
        in_specs=[pl.BlockSpec((None, tm, d), lambda bi, i: (bi, i, 0)),
                  pl.BlockSpec((d, d), lambda bi, i: (0, 0)),
                  pl.BlockSpec((None, tm, d), lambda bi, i: (bi, i, 0)),
                  pl.BlockSpec((None, 8, d), lambda bi, i: (bi, 0, 0)),
                  pl.BlockSpec((1, d), lambda bi, i: (0, 0))],
        out_specs=[pl.BlockSpec((None, tm, d), lambda bi, i: (bi, i, 0)),
                   pl.BlockSpec((None, tm, d), lambda bi, i: (bi, i, 0))],
        out_shape=[jax.ShapeDtypeStruct((b, s, d), F32), jax.ShapeDtypeStruct((b, s, d), BF16)],
        compiler_params=_params("parallel", "parallel"),
        name="out_proj",
    )(merged, w_out, x, mod, gain2.reshape(1, d))


def _glu_down_kernel(ug_ref, halo_ref, uv_ref, cw_ref, cb_ref, wd_ref, x1_ref, mod_ref, gain_ref, o_ref,
                     acc_s, *, tm):
    i = pl.program_id(1)
    f = pl.program_id(2)

    @pl.when(f == 0)
    def _():
        acc_s[...] = jnp.zeros(acc_s.shape, F32)

    ug = ug_ref[...].astype(F32)
    halo = halo_ref[...].astype(F32) * (i > 0).astype(F32)
    row8 = lax.broadcasted_iota(jnp.int32, halo.shape, 0)

    def shifted(k):
        rolled = pltpu.roll(ug, k, 0)
        top = jnp.where(row8 < k, pltpu.roll(halo, k, 0), rolled[:8])
        return jnp.concatenate([top, rolled[8:]], axis=0)

    cw = cw_ref[...]
    conv = cb_ref[...] + shifted(2) * cw[0:1, :] + shifted(1) * cw[1:2, :] + ug * cw[2:3, :]
    gelu = 0.5 * conv * (1.0 + lax.erf(conv * (2.0 ** -0.5)))
    act = (gelu * uv_ref[...].astype(F32)).astype(BF16)
    acc_s[...] += _dot(act, wd_ref[...])

    @pl.when(f == pl.num_programs(2) - 1)
    def _():
        x2 = x1_ref[...] + mod_ref[5:6, :] * acc_s[...]
        o_ref[...] = x2 * lax.rsqrt(jnp.mean(x2 * x2, axis=-1, keepdims=True) + NORM_EPS) * gain_ref[...]


def glu_down(u, conv_w, conv_b, w_down, x1, mod, final_gain, tm=512, tf=512):
    b, s, d = x1.shape
    ff = w_down.shape[0]
    tm = min(tm, s)
    nf = ff // tf
    cw8 = jnp.zeros((8, ff), F32).at[:conv_w.shape[0]].set(conv_w)
    hb = tm // 8
    return pl.pallas_call(
        functools.partial(_glu_down_kernel, tm=tm),
        grid=(b, s // tm, nf),
        in_specs=[pl.BlockSpec((None, tm, tf), lambda bi, i, f: (bi, i, f)),
                  pl.BlockSpec((None, 8, tf), lambda bi, i, f: (bi, jnp.maximum(i * hb - 1, 0), f)),
                  pl.BlockSpec((None, tm, tf), lambda bi, i, f: (bi, i, nf + f)),
                  pl.BlockSpec((8, tf), lambda bi, i, f: (0, f)),
                  pl.BlockSpec((1, tf), lambda bi, i, f: (0, f)),
                  pl.BlockSpec((tf, d), lambda bi, i, f: (f, 0)),
                  pl.BlockSpec((None, tm, d), lambda bi, i, f: (bi, i, 0)),
                  pl.BlockSpec((None, 8, d), lambda bi, i, f: (bi, 0, 0)),
                  pl.BlockSpec((1, d), lambda bi, i, f: (0, 0))],
        out_specs=pl.BlockSpec((None, tm, d), lambda bi, i, f: (bi, i, 0)),
        out_shape=jax.ShapeDtypeStruct((b, s, d), F32),
        scratch_shapes=[pltpu.VMEM((tm, d), F32)],
        compiler_params=_params("parallel", "parallel", "arbitrary"),
        name="glu_down",
    )(u, u, u, cw8, conv_b.reshape(1, ff), w_down, x1, mod, final_gain.reshape(1, d))


def kernel(x, c, w_ada, b_ada, norm1_gain, w_in, lambda_q1, lambda_k1, lambda_q2, lambda_k2, subln_gain, mu_shift, w0, w_lora_up, a0, a_lora_up, g_lora_up, k_k, k_a, r_k, lnx_w, lnx_b, w_branch, w_out, norm2_gain, w_up, conv_w, conv_b, w_down, final_gain):
    b, s, d = x.shape
    depth = w_in.shape[0]
    assert depth == 1
    l = 0
    qk_cols = 2 * ATT_HEADS * 2 * ATT_HEAD_DIM
    v_cols = ATT_HEADS * 2 * ATT_HEAD_DIM
    width = RWKV_HEADS * RWKV_HEAD_DIM
    n_lora = DECAY_LORA + ICLR_LORA + GATE_LORA
    rkv0 = qk_cols + v_cols
    lora0 = rkv0 + 3 * width
    gate0 = lora0 + n_lora

    c8 = jnp.zeros((8, d), F32).at[:b].set(c)
    mod = ada_modulation(c8, w_ada[l], b_ada[l])[:b]
    mod = jnp.pad(mod.reshape(b, 6, d), ((0, 0), (0, 2), (0, 0)))

    h = norm_mod(x, mod, norm1_gain[l], shift_row=0, scale_row=1)

    w_in_l = w_in[l]
    w_qk = w_in_l[:, :qk_cols].astype(BF16)
    w_vg = jnp.concatenate([w_in_l[:, qk_cols:rkv0], w_in_l[:, gate0:]], axis=1).astype(BF16)
    w_rkv = w_in_l[:, rkv0:lora0].astype(BF16)
    w_lora = jnp.pad(w_in_l[:, lora0:gate0], ((0, 0), (0, LORA_PAD - n_lora))).astype(BF16)

    cos, sina, sinb = rope_tables(s)
    qk = matmul(h, w_qk, BF16, rope=(cos, sina, sinb, qk_cols // 2))
    vg = matmul(h, w_vg, BF16)
    rkv = matmul(h, w_rkv, F32)
    lora = matmul(h, w_lora, F32, tn=LORA_PAD)

    per_head_t = lambda a: a.reshape(b, s, ATT_HEADS, LANES).transpose(0, 2, 3, 1)
    y_att = diff_attention(per_head_t(qk[..., :qk_cols // 2]), qk, ATT_HEADS, per_head_t(vg[..., :v_cols]),
                           lambda_q1[l], lambda_k1[l], lambda_q2[l], lambda_k2[l], subln_gain[l])

    mu = mu_shift[l]
    mu_lora = jnp.pad(mu[3 * width:], (0, LORA_PAD - n_lora))
    y_rwkv = rwkv7(rkv, lora, mu[:3 * width], mu_lora, w0[l], w_lora_up[l], a0[l], a_lora_up[l],
                   g_lora_up[l], k_k[l], k_a[l], r_k[l].reshape(-1), lnx_w[l], lnx_b[l])

    tn_merge = 512
    merged = gated_merge(y_att, y_rwkv, vg, v_cols // tn_merge, w_branch[l, 0].astype(BF16),
                         w_branch[l, 1].astype(BF16), tn=tn_merge)
    x1, h2 = out_proj(merged, w_out[l].astype(BF16), x, mod, norm2_gain[l])

    u = matmul(h2, w_up[l].astype(BF16), BF16)
    return glu_down(u, conv_w[l], conv_b[l], w_down[l].astype(BF16), x1, mod, final_gain)
```

```python
import functools
import math

import jax
import jax.numpy as jnp
from jax import lax
from jax.experimental import pallas as pl
from jax.experimental.pallas import tpu as pltpu

F32 = jnp.float32
BF16 = jnp.bfloat16
HIGHEST = lax.Precision.HIGHEST

ATT_HEADS = 8
ATT_HEAD_DIM = 64
ROT_DIM = ATT_HEAD_DIM // 4
ROPE_THETA = 500000.0
SUBLN_EPS = 1e-5
RWKV_HEADS = 16
RWKV_HEAD_DIM = 64
DECAY_LORA = 64
ICLR_LORA = 64
GATE_LORA = 160
GN_EPS = 64e-5
NORM_EPS = 1e-6
LAMBDA_INIT = 0.8 - 0.6 * math.exp(-0.3 * 0)

LANES = 128
VMEM_LIMIT = 48 * 1024 * 1024
RWKV_CHUNK = 64
RWKV_GROUP_BLOCKS = 8
LORA_PAD = 384


def _params(*sem):
    return pltpu.CompilerParams(dimension_semantics=sem, vmem_limit_bytes=VMEM_LIMIT)


def _dot(a, b, precision=None):
    return jnp.dot(a, b, preferred_element_type=F32, precision=precision)


def _dot_t(a, b, precision=None):
    return lax.dot_general(a, b, (((1,), (1,)), ((), ())),
                           preferred_element_type=F32, precision=precision)


def _mod_kernel(c_ref, w_ref, b_ref, o_ref):
    c = c_ref[...]
    s = c * jax.nn.sigmoid(c)
    o_ref[...] = _dot(s, w_ref[...], HIGHEST) + b_ref[...]


def ada_modulation(c8, w_ada, b_ada, tn=1024):
    m, d = c8.shape
    n = w_ada.shape[1]
    return pl.pallas_call(
        _mod_kernel,
        grid=(n // tn,),
        in_specs=[pl.BlockSpec((m, d), lambda j: (0, 0)),
                  pl.BlockSpec((d, tn), lambda j: (0, j)),
                  pl.BlockSpec((1, tn), lambda j: (0, j))],
        out_specs=pl.BlockSpec((m, tn), lambda j: (0, j)),
        out_shape=jax.ShapeDtypeStruct((m, n), F32),
        compiler_params=_params("arbitrary"),
        name="ada_mod",
    )(c8, w_ada, b_ada.reshape(1, n))


def _norm_mod_kernel(x_ref, mod_ref, gain_ref, o_ref, *, shift_row, scale_row):
    x = x_ref[...]
    xn = x * lax.rsqrt(jnp.mean(x * x, axis=-1, keepdims=True) + NORM_EPS)
    h = xn * gain_ref[...] * (1.0 + mod_ref[scale_row:scale_row + 1, :]) + mod_ref[shift_row:shift_row + 1, :]
    o_ref[...] = h.astype(o_ref.dtype)


def norm_mod(x, mod, gain, shift_row, scale_row, tm=512):
    b, s, d = x.shape
    tm = min(tm, s)
    return pl.pallas_call(
        functools.partial(_norm_mod_kernel, shift_row=shift_row, scale_row=scale_row),
        grid=(b, s // tm),
        in_specs=[pl.BlockSpec((None, tm, d), lambda bi, i: (bi, i, 0)),
                  pl.BlockSpec((None, 8, d), lambda bi, i: (bi, 0, 0)),
                  pl.BlockSpec((1, d), lambda bi, i: (0, 0))],
        out_specs=pl.BlockSpec((None, tm, d), lambda bi, i: (bi, i, 0)),
        out_shape=jax.ShapeDtypeStruct((b, s, d), BF16),
        compiler_params=_params("parallel", "parallel"),
        name="norm_mod",
    )(x, mod, gain.reshape(1, d))


def _mm_kernel(a_ref, w_ref, o_ref):
    o_ref[...] = _dot(a_ref[...], w_ref[...]).astype(o_ref.dtype)


def _mm_rope_kernel(a_ref, w_ref, cos_ref, sina_ref, sinb_ref, o_ref, *, tn, q_cols):
    acc = _dot(a_ref[...], w_ref[...])
    scale = jnp.where(pl.program_id(2) * tn < q_cols, ATT_HEAD_DIM ** -0.5 * math.log2(math.e), 1.0).astype(F32)
    cos, sina, sinb = cos_ref[...], sina_ref[...], sinb_ref[...]
    for j in range(tn // LANES):
        t = acc[:, j * LANES:(j + 1) * LANES]
        half = ROT_DIM // 2
        r = t * cos + pltpu.roll(t, LANES - half, 1) * sina + pltpu.roll(t, half, 1) * sinb
        o_ref[:, j * LANES:(j + 1) * LANES] = (r * scale).astype(o_ref.dtype)


def matmul(a, w, out_dtype, tm=1024, tn=512, rope=None):
    b, s, k = a.shape
    n = w.shape[1]
    tm = min(tm, s)
    tn = min(tn, n)
    assert s % tm == 0 and n % tn == 0
    in_specs = [pl.BlockSpec((None, tm, k), lambda bi, i, j: (bi, i, 0)),
                pl.BlockSpec((k, tn), lambda bi, i, j: (0, j))]
    args = [a, w]
    if rope is None:
        body = _mm_kernel
        name = "matmul"
    else:
        cos, sina, sinb, q_cols = rope
        body = functools.partial(_mm_rope_kernel, tn=tn, q_cols=q_cols)
        in_specs += [pl.BlockSpec((tm, LANES), lambda bi, i, j: (i, 0))] * 3
        args += [cos, sina, sinb]
        name = "matmul_rope"
    return pl.pallas_call(
        body,
        grid=(b, s // tm, n // tn),
        in_specs=in_specs,
        out_specs=pl.BlockSpec((None, tm, tn), lambda bi, i, j: (bi, i, j)),
        out_shape=jax.ShapeDtypeStruct((b, s, n), out_dtype),
        compiler_params=_params("parallel", "parallel", "arbitrary"),
        name=name,
    )(*args)


def rope_tables(seq):
    half = ROT_DIM // 2
    pos = jnp.arange(seq, dtype=F32)
    inv = ROPE_THETA ** (-jnp.arange(0, ROT_DIM, 2, dtype=F32) / ROT_DIM)
    ang = pos[:, None] * inv[None, :]
    cos, sin = jnp.cos(ang), jnp.sin(ang)
    ones = jnp.ones((seq, ATT_HEAD_DIM - ROT_DIM), F32)
    zeros = jnp.zeros((seq, ATT_HEAD_DIM - ROT_DIM), F32)
    zh = jnp.zeros((seq, half), F32)
    cos64 = jnp.concatenate([cos, cos, ones], axis=1)
    sina64 = jnp.concatenate([-sin, zh, zeros], axis=1)
    sinb64 = jnp.concatenate([zh, sin, zeros], axis=1)
    rep = LANES // ATT_HEAD_DIM
    return (jnp.tile(cos64, (1, rep)), jnp.tile(sina64, (1, rep)), jnp.tile(sinb64, (1, rep)))


SUBLANES = 8


def _sublane_all(x, op):
    for shift in (4, 2, 1):
        x = op(x, pltpu.roll(x, shift, 0))
    return x


def _attn_kernel(qt_ref, k_ref, vt_ref, lq1_ref, lk1_ref, lq2_ref, lk2_ref, gain_ref, o_ref,
                 acc_s, *, tq, tk):
    qi = pl.program_id(2)
    qt = qt_ref[...]
    row = lax.broadcasted_iota(jnp.int32, qt.shape, 0)
    zero = jnp.zeros_like(qt)
    q2 = jnp.concatenate([jnp.where(row < ATT_HEAD_DIM, qt, zero),
                          jnp.where(row >= ATT_HEAD_DIM, qt, zero)], axis=1)
    n2 = 2 * tq
    acc_s[...] = jnp.zeros(acc_s.shape, F32)

    def step(j, m_old, l_old, masked):
        k0 = pl.multiple_of(j * tk, tk)
        kb = k_ref[pl.ds(k0, tk), :]
        vtb = vt_ref[:, pl.ds(k0, tk)]
        s = _dot(kb, q2)
        if masked:
            kpos = k0 + lax.broadcasted_iota(jnp.int32, (tk, tq), 0)
            qpos = qi * tq + lax.broadcasted_iota(jnp.int32, (tk, tq), 1)
            ok = kpos <= qpos
            s = jnp.where(jnp.concatenate([ok, ok], axis=1), s, -jnp.inf)
        s3 = s.reshape(tk // SUBLANES, SUBLANES, n2)
        m_new = jnp.maximum(m_old, _sublane_all(jnp.max(s3, axis=0), jnp.maximum))
        p3 = jnp.exp2(s3 - m_new[None])
        corr = jnp.exp2(m_old - m_new)
        l_new = l_old * corr + jnp.sum(p3, axis=0)
        pv = _dot(vtb, p3.reshape(tk, n2).astype(BF16))
        acc3 = acc_s[...].reshape(LANES // SUBLANES, SUBLANES, n2)
        acc_s[...] = (acc3 * corr[None]).reshape(LANES, n2) + pv
        return m_new, l_new

    n_full = (qi * tq) // tk
    m0 = jnp.full((SUBLANES, n2), -jnp.inf, F32)
    l0 = jnp.zeros((SUBLANES, n2), F32)
    m, l = lax.fori_loop(0, n_full, lambda j, c: step(j, c[0], c[1], False), (m0, l0))
    for d in range(tq // tk):
        m, l = step(n_full + d, m, l, True)

    lam = (jnp.exp(jnp.sum(lq1_ref[...] * lk1_ref[...], axis=-1, keepdims=True))
           - jnp.exp(jnp.sum(lq2_ref[...] * lk2_ref[...], axis=-1, keepdims=True))
           + LAMBDA_INIT)
    l = _sublane_all(l, jnp.add)
    o3 = acc_s[...].reshape(LANES // SUBLANES, SUBLANES, n2) / l[None]
    d3 = o3[:, :, :tq] - lam * o3[:, :, tq:]
    ms = _sublane_all(jnp.sum(d3 * d3, axis=0), jnp.add) * (1.0 / LANES)
    d3 = d3 * lax.rsqrt(ms + SUBLN_EPS)[None]
    o = d3.reshape(LANES, tq).T
    o_ref[...] = (o * gain_ref[...] * (1.0 - LAMBDA_INIT)).astype(o_ref.dtype)


def diff_attention(qt, kk, k_col0, vt, lq1, lk1, lq2, lk2, subln_gain, tq=1024, tk=1024):
    b, h, _, s = qt.shape
    tq = min(tq, s)
    tk = min(tk, tq)
    vec = lambda a: a.reshape(1, -1).astype(F32)
    small = lambda n: pl.BlockSpec((1, n), lambda bi, hi, i: (0, 0))
    return pl.pallas_call(
        functools.partial(_attn_kernel, tq=tq, tk=tk),
        grid=(b, h, s // tq),
        in_specs=[pl.BlockSpec((None, None, LANES, tq), lambda bi, hi, i: (bi, hi, 0, i)),
                  pl.BlockSpec((None, s, LANES), lambda bi, hi, i: (bi, 0, k_col0 + hi)),
                  pl.BlockSpec((None, None, LANES, s), lambda bi, hi, i: (bi, hi, 0, 0)),
                  small(ATT_HEAD_DIM), small(ATT_HEAD_DIM), small(ATT_HEAD_DIM), small(ATT_HEAD_DIM),
                  small(LANES)],
        out_specs=pl.BlockSpec((None, tq, LANES), lambda bi, hi, i: (bi, i, hi)),
        out_shape=jax.ShapeDtypeStruct((b, s, h * LANES), BF16),
        scratch_shapes=[pltpu.VMEM((LANES, 2 * tq), F32)],
        compiler_params=_params("parallel", "parallel", "arbitrary"),
        name="diff_attention",
    )(qt, kk, vt, vec(lq1), vec(lk1), vec(lq2), vec(lk2), vec(subln_gain))


def _bdot(a, b):
    return jnp.dot(a.astype(BF16), b.astype(BF16), preferred_element_type=F32)


def _dot3(a, b):
    a_hi = a.astype(BF16)
    a_lo = (a - a_hi.astype(F32)).astype(BF16)
    b_hi = b.astype(BF16)
    b_lo = (b - b_hi.astype(F32)).astype(BF16)
    d = functools.partial(jnp.dot, preferred_element_type=F32)
    return d(a_hi, b_hi) + (d(a_hi, b_lo) + d(a_lo, b_hi))


def _each(f, *lists):
    return [f(*xs) for xs in zip(*lists)]


def _rwkv_pairs(pairs, states, c):
    n = RWKV_HEAD_DIM
    c2 = 2 * c
    row = lax.broadcasted_iota(jnp.int32, (c2, c2), 0)
    col = lax.broadcasted_iota(jnp.int32, (c2, c2), 1)
    lower = row >= col
    strict = row > col
    eye2 = (row == col).astype(F32)
    rc = lax.broadcasted_iota(jnp.int32, (c, c), 0)
    cc = lax.broadcasted_iota(jnp.int32, (c, c), 1)
    lower_b = (rc >= cc).astype(BF16)
    li = lax.broadcasted_iota(jnp.int32, (LANES, LANES), 0)
    lj = lax.broadcasted_iota(jnp.int32, (LANES, LANES), 1)
    same_head = ((li // n) == (lj // n)).astype(BF16)
    eye_l = (li == lj).astype(F32)
    head1 = lax.broadcasted_iota(jnp.int32, (c, LANES), 1) >= n

    def stack(x):
        return jnp.concatenate([jnp.where(head1, 0.0, x), jnp.where(head1, x, 0.0)], axis=0)

    def head_sum(x):
        return jnp.dot(x.astype(BF16), same_head, preferred_element_type=F32)

    r, k, v, lw, ai, g, kk_w, ka_w, rk_w, lnw, lnb = [list(x) for x in zip(*pairs)]

    kk = _each(lambda k, w: k * w, k, kk_w)
    kk = _each(lambda kk: kk / jnp.maximum(jnp.sqrt(head_sum(kk * kk)), 1e-12), kk)
    k2 = _each(lambda k, ai, w: k * (1.0 + (ai - 1.0) * w), k, ai, ka_w)
    b = _each(lambda kk, ai: kk * ai, kk, ai)

    def cumsum(lw):
        hi = lw.astype(BF16)
        r1 = lw - hi.astype(F32)
        mid = r1.astype(BF16)
        lo = (r1 - mid.astype(F32)).astype(BF16)
        c3 = jnp.dot(lower_b, jnp.concatenate([hi, mid, lo], axis=1), preferred_element_type=F32)
        return c3[:, :LANES] + c3[:, LANES:2 * LANES] + c3[:, 2 * LANES:]

    cum = _each(cumsum, lw)
    e_cum = _each(jnp.exp, cum)
    e_neg = _each(lambda cm: jnp.exp(-cm), cum)
    e_tail = _each(lambda cm: jnp.exp(cm[c - 1:c, :] - cm), cum)
    at = _each(lambda kk, cm, l: stack(-kk * jnp.exp(cm - l)), kk, cum, lw)
    rt = _each(lambda r, e: stack(r * e), r, e_cum)
    vs = _each(stack, v)
    qq = _each(lambda x, y: jnp.concatenate([x, y], axis=0).astype(BF16), at, rt)
    kq = _each(lambda b, k2, e: jnp.concatenate([stack(b * e), stack(k2 * e)], axis=0).astype(BF16),
               b, k2, e_neg)
    s = _each(_dot_t, qq, kq)
    a_ab = _each(lambda s: jnp.where(strict, s[:c2, :c2], 0.0), s)
    a_ak = _each(lambda s: jnp.where(strict, s[:c2, c2:], 0.0), s)
    a_rb = _each(lambda s: jnp.where(lower, s[c2:, :c2], 0.0), s)
    a_rk = _each(lambda s: jnp.where(lower, s[c2:, c2:], 0.0), s)

    def level(m):
        return ((row // (2 * m)) == (col // (2 * m))) & ((row // m) % 2 == 1) & ((col // m) % 2 == 0)

    t = _each(lambda x: eye2 + jnp.where(level(1), x, 0.0), a_ab)
    m = 2
    while m < c:
        sel = level(m)
        tn = _each(lambda t, x: _bdot(t, jnp.where(sel, x, 0.0)), t, a_ab)
        t = _each(lambda t, x: t + _bdot(x, t), t, tn)
        m *= 2

    x1 = _each(_bdot, a_ak, vs)
    wu = _each(lambda t, at, x1: _bdot(t, jnp.concatenate([at, x1], axis=1)), t, at, x1)
    bh_t = _each(lambda b, e: stack(b * e).T, b, e_tail)
    kh_t = _each(lambda k2, e: stack(k2 * e).T, k2, e_tail)
    z1 = _each(lambda a, bt, wu: _bdot(jnp.concatenate([a, bt], axis=0), wu), a_rb, bh_t, wu)
    z2 = _each(lambda a, kt, v: _bdot(jnp.concatenate([a, kt], axis=0), v), a_rk, kh_t, vs)
    pm = _each(lambda z1, rt, e: jnp.concatenate([rt + z1[:c2, :LANES],
                                                  eye_l * e[c - 1:c, :] + z1[c2:, :LANES]], axis=0),
               z1, rt, e_cum)
    yh = _each(lambda pm, h0, z1, z2: _bdot(pm, h0) + z1[:, LANES:] + z2, pm, states, z1, z2)

    def finish(yh, r, k2, v, g, rk_w, lnw, lnb):
        y = yh[:c] + yh[c:c2]
        mu = head_sum(y) * (1.0 / n)
        d = y - mu
        var = head_sum(d * d) * (1.0 / n)
        yn = d * lax.rsqrt(var + GN_EPS) * lnw + lnb
        bonus = head_sum(r * k2 * rk_w) * v
        return (yn + bonus) * g

    outs = _each(finish, yh, r, k2, v, g, rk_w, lnw, lnb)
    return outs, [x[c2:] for x in yh]


def _rwkv_kernel(rkv_ref, lora_ref, mu_rkv_ref, mu_lora_ref, w0_ref, wl_ref, a0_ref, al_ref, gl_ref,
                 kk_ref, ka_ref, rk_ref, lnw_ref, lnb_ref, o_ref,
                 state_s, prev_rkv_s, prev_lora_s, r_s, k_s, v_s, lw_s, ai_s, g_s, *, c):
    ci = pl.program_id(1)
    width = RWKV_HEADS * RWKV_HEAD_DIM

    @pl.when(ci == 0)
    def _():
        state_s[...] = jnp.zeros(state_s.shape, F32)
        prev_rkv_s[...] = jnp.zeros(prev_rkv_s.shape, F32)
        prev_lora_s[...] = jnp.zeros(prev_lora_s.shape, F32)

    def shift_mix(p, prev_s, mu):
        rolled = pltpu.roll(p, 1, 0)
        first = lax.broadcasted_iota(jnp.int32, p.shape, 0) == 0
        shifted = jnp.where(first, prev_s[7:8, :], rolled)
        prev_s[...] = p[c - 8:, :]
        return p + mu * (shifted - p)

    z = shift_mix(rkv_ref[...], prev_rkv_s, mu_rkv_ref[...])
    zl = shift_mix(lora_ref[...], prev_lora_s, mu_lora_ref[...])
    r_s[...] = z[:, :width]
    k_s[...] = z[:, width:2 * width]
    v_s[...] = z[:, 2 * width:]
    xw = zl[:, :DECAY_LORA]
    xa = zl[:, DECAY_LORA:DECAY_LORA + ICLR_LORA]
    xg = zl[:, DECAY_LORA + ICLR_LORA:DECAY_LORA + ICLR_LORA + GATE_LORA]
    w_logit = -jax.nn.softplus(-(w0_ref[...] + _dot3(jnp.tanh(xw), wl_ref[...]))) - 0.5
    lw_s[...] = -jnp.exp(w_logit)
    ai_s[...] = jax.nn.sigmoid(a0_ref[...] + _dot3(xa, al_ref[...]))
    g_s[...] = _dot3(jax.nn.sigmoid(xg), gl_ref[...])

    def group_body(gi, carry):
        slices = [pl.ds(pl.multiple_of((gi * RWKV_GROUP_BLOCKS + blk) * LANES, LANES), LANES)
                  for blk in range(RWKV_GROUP_BLOCKS)]
        pairs = [tuple(ref[:, sl] for ref in (r_s, k_s, v_s, lw_s, ai_s, g_s,
                                              kk_ref, ka_ref, rk_ref, lnw_ref, lnb_ref)) for sl in slices]
        states = [state_s[gi * RWKV_GROUP_BLOCKS + blk] for blk in range(RWKV_GROUP_BLOCKS)]
        outs, h_next = _rwkv_pairs(pairs, states, c)
        for blk, sl in enumerate(slices):
            o_ref[:, sl] = outs[blk].astype(o_ref.dtype)
            state_s[gi * RWKV_GROUP_BLOCKS + blk] = h_next[blk]
        return carry

    lax.fori_loop(0, width // (LANES * RWKV_GROUP_BLOCKS), group_body, 0)


def rwkv7(rkv, lora, mu_rkv, mu_lora, w0, wl, a0, al, gl, k_k, k_a, r_k, lnx_w, lnx_b):
    b, s, _ = rkv.shape
    c = min(RWKV_CHUNK, s)
    width = RWKV_HEADS * RWKV_HEAD_DIM
    n = RWKV_HEAD_DIM
    row = lambda a: a.reshape(1, -1).astype(F32)
    const = lambda shape: pl.BlockSpec(shape, lambda bi, ci: (0,) * len(shape))
    return pl.pallas_call(
        functools.partial(_rwkv_kernel, c=c),
        grid=(b, s // c),
        in_specs=[pl.BlockSpec((None, c, 3 * width), lambda bi, ci: (bi, ci, 0)),
                  pl.BlockSpec((None, c, LORA_PAD), lambda bi, ci: (bi, ci, 0)),
                  const((1, 3 * width)), const((1, LORA_PAD)),
                  const((1, width)), const((DECAY_LORA, width)),
                  const((1, width)), const((ICLR_LORA, width)), const((GATE_LORA, width)),
                  const((1, width)), const((1, width)), const((1, width)),
                  const((1, width)), const((1, width))],
        out_specs=pl.BlockSpec((None, c, width), lambda bi, ci: (bi, ci, 0)),
        out_shape=jax.ShapeDtypeStruct((b, s, width), BF16),
        scratch_shapes=[pltpu.VMEM((width // LANES, LANES, LANES), F32),
                        pltpu.VMEM((8, 3 * width), F32), pltpu.VMEM((8, LORA_PAD), F32)]
                       + [pltpu.VMEM((c, width), F32)] * 6,
        compiler_params=_params("parallel", "arbitrary"),
        name="rwkv7",
    )(rkv, lora, row(mu_rkv), row(mu_lora), row(w0), wl.astype(F32), row(a0), al.astype(F32),
      gl.astype(F32), row(k_k), row(k_a), row(r_k), row(lnx_w), row(lnx_b))


def _merge_kernel(ya_ref, yr_ref, g0_ref, g1_ref, wa_ref, wr_ref, o_ref):
    za = _dot(ya_ref[...], wa_ref[...])
    zr = _dot(yr_ref[...], wr_ref[...])
    ga = jax.nn.sigmoid(g0_ref[...].astype(F32))
    gr = jax.nn.sigmoid(g1_ref[...].astype(F32))
    o_ref[...] = (ga * za + gr * zr).astype(o_ref.dtype)


def gated_merge(y_att, y_rwkv, vg, gate_col0, wa, wr, tm=1024, tn=512):
    b, s, kdim = y_att.shape
    n = wa.shape[1]
    tm = min(tm, s)
    nb = n // tn
    return pl.pallas_call(
        _merge_kernel,
        grid=(b, s // tm, nb),
        in_specs=[pl.BlockSpec((None, tm, kdim), lambda bi, i, j: (bi, i, 0)),
                  pl.BlockSpec((None, tm, kdim), lambda bi, i, j: (bi, i, 0)),
                  pl.BlockSpec((None, tm, tn), lambda bi, i, j: (bi, i, gate_col0 + j)),
                  pl.BlockSpec((None, tm, tn), lambda bi, i, j: (bi, i, gate_col0 + nb + j)),
                  pl.BlockSpec((kdim, tn), lambda bi, i, j: (0, j)),
                  pl.BlockSpec((kdim, tn), lambda bi, i, j: (0, j))],
        out_specs=pl.BlockSpec((None, tm, tn), lambda bi, i, j: (bi, i, j)),
        out_shape=jax.ShapeDtypeStruct((b, s, n), BF16),
        compiler_params=_params("parallel", "parallel", "arbitrary"),
        name="gated_merge",
    )(y_att, y_rwkv, vg, vg, wa, wr)


def _out_proj_kernel(m_ref, w_ref, x_ref, mod_ref, gain_ref, x1_ref, h2_ref):
    x1 = x_ref[...] + mod_ref[2:3, :] * _dot(m_ref[...], w_ref[...])
    x1_ref[...] = x1
    xn = x1 * lax.rsqrt(jnp.mean(x1 * x1, axis=-1, keepdims=True) + NORM_EPS)
    h2_ref[...] = (xn * gain_ref[...] * (1.0 + mod_ref[4:5, :]) + mod_ref[3:4, :]).astype(h2_ref.dtype)


def out_proj(merged, w_out, x, mod, gain2, tm=256):
    b, s, d = x.shape
    tm = min(tm, s)
    return pl.pallas_call(
        _out_proj_kernel,
        grid=(b, s // tm),
        in_specs=[pl.BlockSpec((None, tm, d), lambda bi, i: (bi, i, 0)),
                  pl.BlockSpec((d, d), lambda bi, i: (0, 0)),
                  pl.BlockSpec((None, tm, d), lambda bi, i: (bi, i, 0)),
                  pl.BlockSpec((None, 8, d), lambda bi, i: (bi, 0, 0)),
                  pl.BlockSpec((1, d), lambda bi, i: (0, 0))],
        out_specs=[pl.BlockSpec((None, tm, d), lambda bi, i: (bi, i, 0)),
                   pl.BlockSpec((None, tm, d), lambda bi, i: (bi, i, 0))],
        out_shape=[jax.ShapeDtypeStruct((b, s, d), F32), jax.ShapeDtypeStruct((b, s, d), BF16)],
        compiler_params=_params("parallel", "parallel"),
        name="out_proj",
    )(merged, w_out, x, mod, gain2.reshape(1, d))


def _glu_down_kernel(ug_ref, halo_ref, uv_ref, cw_ref, cb_ref, wd_ref, x1_ref, mod_ref, gain_ref, o_ref,
                     acc_s, *, tm):
    i = pl.program_id(1)
    f = pl.program_id(2)

    @pl.when(f == 0)
    def _():
        acc_s[...] = jnp.zeros(acc_s.shape, F32)

    ug = ug_ref[...].astype(F32)
    halo = halo_ref[...].astype(F32) * (i > 0).astype(F32)
    row8 = lax.broadcasted_iota(jnp.int32, halo.shape, 0)

    def shifted(k):
        rolled = pltpu.roll(ug, k, 0)
        top = jnp.where(row8 < k, pltpu.roll(halo, k, 0), rolled[:8])
        return jnp.concatenate([top, rolled[8:]], axis=0)

    cw = cw_ref[...]
    conv = cb_ref[...] + shifted(2) * cw[0:1, :] + shifted(1) * cw[1:2, :] + ug * cw[2:3, :]
    gelu = 0.5 * conv * (1.0 + lax.erf(conv * (2.0 ** -0.5)))
    act = (gelu * uv_ref[...].astype(F32)).astype(BF16)
    acc_s[...] += _dot(act, wd_ref[...])

    @pl.when(f == pl.num_programs(2) - 1)
    def _():
        x2 = x1_ref[...] + mod_ref[5:6, :] * acc_s[...]
        o_ref[...] = x2 * lax.rsqrt(jnp.mean(x2 * x2, axis=-1, keepdims=True) + NORM_EPS) * gain_ref[...]


def glu_down(u, conv_w, conv_b, w_down, x1, mod, final_gain, tm=512, tf=512):
    b, s, d = x1.shape
    ff = w_down.shape[0]
    tm = min(tm, s)
    nf = ff // tf
    cw8 = jnp.zeros((8, ff), F32).at[:conv_w.shape[0]].set(conv_w)
    hb = tm // 8
    return pl.pallas_call(
        functools.partial(_glu_down_kernel, tm=tm),
        grid=(b, s // tm, nf),
        in_specs=[pl.BlockSpec((None, tm, tf), lambda bi, i, f: (bi, i, f)),
                  pl.BlockSpec((None, 8, tf), lambda bi, i, f: (bi, jnp.maximum(i * hb - 1, 0), f)),
                  pl.BlockSpec((None, tm, tf), lambda bi, i, f: (bi, i, nf + f)),
                  pl.BlockSpec((8, tf), lambda bi, i, f: (0, f)),
                  pl.BlockSpec((1, tf), lambda bi, i, f: (0, f)),
                  pl.BlockSpec((tf, d), lambda bi, i, f: (f, 0)),
                  pl.BlockSpec((None, tm, d), lambda bi, i, f: (bi, i, 0)),
                  pl.BlockSpec((None, 8, d), lambda bi, i, f: (bi, 0, 0)),
                  pl.BlockSpec((1, d), lambda bi, i, f: (0, 0))],
        out_specs=pl.BlockSpec((None, tm, d), lambda bi, i, f: (bi, i, 0)),
        out_shape=jax.ShapeDtypeStruct((b, s, d), F32),
        scratch_shapes=[pltpu.VMEM((tm, d), F32)],
        compiler_params=_params("parallel", "parallel", "arbitrary"),
        name="glu_down",
    )(u, u, u, cw8, conv_b.reshape(1, ff), w_down, x1, mod, final_gain.reshape(1, d))


def kernel(x, c, w_ada, b_ada, norm1_gain, w_in, lambda_q1, lambda_k1, lambda_q2, lambda_k2, subln_gain, mu_shift, w0, w_lora_up, a0, a_lora_up, g_lora_up, k_k, k_a, r_k, lnx_w, lnx_b, w_branch, w_out, norm2_gain, w_up, conv_w, conv_b, w_down, final_gain):
    b, s, d = x.shape
    depth = w_in.shape[0]
    assert depth == 1
    l = 0
    qk_cols = 2 * ATT_HEADS * 2 * ATT_HEAD_DIM
    v_cols = ATT_HEADS * 2 * ATT_HEAD_DIM
    width = RWKV_HEADS * RWKV_HEAD_DIM
    n_lora = DECAY_LORA + ICLR_LORA + GATE_LORA
    rkv0 = qk_cols + v_cols
    lora0 = rkv0 + 3 * width
    gate0 = lora0 + n_lora

    c8 = jnp.zeros((8, d), F32).at[:b].set(c)
    mod = ada_modulation(c8, w_ada[l], b_ada[l])[:b]
    mod = jnp.pad(mod.reshape(b, 6, d), ((0, 0), (0, 2), (0, 0)))

    h = norm_mod(x, mod, norm1_gain[l], shift_row=0, scale_row=1)

    w_in_l = w_in[l]
    w_qk = w_in_l[:, :qk_cols].astype(BF16)
    w_vg = jnp.concatenate([w_in_l[:, qk_cols:rkv0], w_in_l[:, gate0:]], axis=1).astype(BF16)
    w_rkv = w_in_l[:, rkv0:lora0].astype(BF16)
    w_lora = jnp.pad(w_in_l[:, lora0:gate0], ((0, 0), (0, LORA_PAD - n_lora))).astype(BF16)

    cos, sina, sinb = rope_tables(s)
    qk = matmul(h, w_qk, BF16, rope=(cos, sina, sinb, qk_cols // 2))
    vg = matmul(h, w_vg, BF16)
    rkv = matmul(h, w_rkv, F32)
    lora = matmul(h, w_lora, F32, tn=LORA_PAD)

    per_head_t = lambda a: a.reshape(b, s, ATT_HEADS, LANES).transpose(0, 2, 3, 1)
    y_att = diff_attention(per_head_t(qk[..., :qk_cols // 2]), qk, ATT_HEADS, per_head_t(vg[..., :v_cols]),
                           lambda_q1[l], lambda_k1[l], lambda_q2[l], lambda_k2[l], subln_gain[l])

    mu = mu_shift[l]
    mu_lora = jnp.pad(mu[3 * width:], (0, LORA_PAD - n_lora))
    y_rwkv = rwkv7(rkv, lora, mu[:3 * width], mu_lora, w0[l], w_lora_up[l], a0[l], a_lora_up[l],
                   g_lora_up[l], k_k[l], k_a[l], r_k[l].reshape(-1), lnx_w[l], lnx_b[l])

    tn_merge = 512
    merged = gated_merge(y_att, y_rwkv, vg, v_cols // tn_merge, w_branch[l, 0].astype(BF16),
                         w_branch[l, 1].astype(BF16), tn=tn_merge)
    x1, h2 = out_proj(merged, w_out[l].astype(BF16), x, mod, norm2_gain[l])

    u = matmul(h2, w_up[l].astype(BF16), BF16)
    return glu_down(u, conv_w[l], conv_b[l], w_down[l].astype(BF16), x1, mod, final_gain)
```

```python
import functools
import math

import jax
import jax.numpy as jnp
from jax import lax
from jax.experimental import pallas as pl
from jax.experimental.pallas import tpu as pltpu

F32 = jnp.float32
BF16 = jnp.bfloat16
HIGHEST = lax.Precision.HIGHEST

ATT_HEADS = 8
ATT_HEAD_DIM = 64
ROT_DIM = ATT_HEAD_DIM // 4
ROPE_THETA = 500000.0
SUBLN_EPS = 1e-5
RWKV_HEADS = 16
RWKV_HEAD_DIM = 64
DECAY_LORA = 64
ICLR_LORA = 64
GATE_LORA = 160
GN_EPS = 64e-5
NORM_EPS = 1e-6
LAMBDA_INIT = 0.8 - 0.6 * math.exp(-0.3 * 0)

LANES = 128
VMEM_LIMIT = 48 * 1024 * 1024
RWKV_CHUNK = 64
RWKV_STEP_CHUNKS = 2
RWKV_GROUP_BLOCKS = 8
LORA_PAD = 384


def _params(*sem):
    return pltpu.CompilerParams(dimension_semantics=sem, vmem_limit_bytes=VMEM_LIMIT)


def _dot(a, b, precision=None):
    return jnp.dot(a, b, preferred_element_type=F32, precision=precision)


def _dot_t(a, b, precision=None):
    return lax.dot_general(a, b, (((1,), (1,)), ((), ())),
                           preferred_element_type=F32, precision=precision)


def _mod_kernel(c_ref, w_ref, b_ref, o_ref):
    c = c_ref[...]
    s = c * jax.nn.sigmoid(c)
    o_ref[...] = _dot(s, w_ref[...], HIGHEST) + b_ref[...]


def ada_modulation(c8, w_ada, b_ada, tn=1024):
    m, d = c8.shape
    n = w_ada.shape[1]
    return pl.pallas_call(
        _mod_kernel,
        grid=(n // tn,),
        in_specs=[pl.BlockSpec((m, d), lambda j: (0, 0)),
                  pl.BlockSpec((d, tn), lambda j: (0, j)),
                  pl.BlockSpec((1, tn), lambda j: (0, j))],
        out_specs=pl.BlockSpec((m, tn), lambda j: (0, j)),
        out_shape=jax.ShapeDtypeStruct((m, n), F32),
        compiler_params=_params("arbitrary"),
        name="ada_mod",
    )(c8, w_ada, b_ada.reshape(1, n))


def _norm_mod_kernel(x_ref, mod_ref, gain_ref, o_ref, *, shift_row, scale_row):
    x = x_ref[...]
    xn = x * lax.rsqrt(jnp.mean(x * x, axis=-1, keepdims=True) + NORM_EPS)
    h = xn * gain_ref[...] * (1.0 + mod_ref[scale_row:scale_row + 1, :]) + mod_ref[shift_row:shift_row + 1, :]
    o_ref[...] = h.astype(o_ref.dtype)


def norm_mod(x, mod, gain, shift_row, scale_row, tm=512):
    b, s, d = x.shape
    tm = min(tm, s)
    return pl.pallas_call(
        functools.partial(_norm_mod_kernel, shift_row=shift_row, scale_row=scale_row),
        grid=(b, s // tm),
        in_specs=[pl.BlockSpec((None, tm, d), lambda bi, i: (bi, i, 0)),
                  pl.BlockSpec((None, 8, d), lambda bi, i: (bi, 0, 0)),
                  pl.BlockSpec((1, d), lambda bi, i: (0, 0))],
        out_specs=pl.BlockSpec((None, tm, d), lambda bi, i: (bi, i, 0)),
        out_shape=jax.ShapeDtypeStruct((b, s, d), BF16),
        compiler_params=_params("parallel", "parallel"),
        name="norm_mod",
    )(x, mod, gain.reshape(1, d))


def _mm_kernel(a_ref, w_ref, o_ref):
    o_ref[...] = _dot(a_ref[...], w_ref[...]).astype(o_ref.dtype)


def _mm_rope_kernel(a_ref, w_ref, cos_ref, sina_ref, sinb_ref, o_ref, *, tn, q_cols):
    acc = _dot(a_ref[...], w_ref[...])
    scale = jnp.where(pl.program_id(2) * tn < q_cols, ATT_HEAD_DIM ** -0.5 * math.log2(math.e), 1.0).astype(F32)
    cos, sina, sinb = cos_ref[...], sina_ref[...], sinb_ref[...]
    for j in range(tn // LANES):
        t = acc[:, j * LANES:(j + 1) * LANES]
        half = ROT_DIM // 2
        r = t * cos + pltpu.roll(t, LANES - half, 1) * sina + pltpu.roll(t, half, 1) * sinb
        o_ref[:, j * LANES:(j + 1) * LANES] = (r * scale).astype(o_ref.dtype)


def matmul(a, w, out_dtype, tm=1024, tn=512, rope=None):
    b, s, k = a.shape
    n = w.shape[1]
    tm = min(tm, s)
    tn = min(tn, n)
    assert s % tm == 0 and n % tn == 0
    in_specs = [pl.BlockSpec((None, tm, k), lambda bi, i, j: (bi, i, 0)),
                pl.BlockSpec((k, tn), lambda bi, i, j: (0, j))]
    args = [a, w]
    if rope is None:
        body = _mm_kernel
        name = "matmul"
    else:
        cos, sina, sinb, q_cols = rope
        body = functools.partial(_mm_rope_kernel, tn=tn, q_cols=q_cols)
        in_specs += [pl.BlockSpec((tm, LANES), lambda bi, i, j: (i, 0))] * 3
        args += [cos, sina, sinb]
        name = "matmul_rope"
    return pl.pallas_call(
        body,
        grid=(b, s // tm, n // tn),
        in_specs=in_specs,
        out_specs=pl.BlockSpec((None, tm, tn), lambda bi, i, j: (bi, i, j)),
        out_shape=jax.ShapeDtypeStruct((b, s, n), out_dtype),
        compiler_params=_params("parallel", "parallel", "arbitrary"),
        name=name,
    )(*args)


def rope_tables(seq):
    half = ROT_DIM // 2
    pos = jnp.arange(seq, dtype=F32)
    inv = ROPE_THETA ** (-jnp.arange(0, ROT_DIM, 2, dtype=F32) / ROT_DIM)
    ang = pos[:, None] * inv[None, :]
    cos, sin = jnp.cos(ang), jnp.sin(ang)
    ones = jnp.ones((seq, ATT_HEAD_DIM - ROT_DIM), F32)
    zeros = jnp.zeros((seq, ATT_HEAD_DIM - ROT_DIM), F32)
    zh = jnp.zeros((seq, half), F32)
    cos64 = jnp.concatenate([cos, cos, ones], axis=1)
    sina64 = jnp.concatenate([-sin, zh, zeros], axis=1)
    sinb64 = jnp.concatenate([zh, sin, zeros], axis=1)
    rep = LANES // ATT_HEAD_DIM
    return (jnp.tile(cos64, (1, rep)), jnp.tile(sina64, (1, rep)), jnp.tile(sinb64, (1, rep)))


SUBLANES = 8


def _sublane_all(x, op):
    for shift in (4, 2, 1):
        x = op(x, pltpu.roll(x, shift, 0))
    return x


def _attn_kernel(qt_ref, k_ref, vt_ref, lq1_ref, lk1_ref, lq2_ref, lk2_ref, gain_ref, o_ref,
                 acc_s, *, tq, tk):
    qi = pl.program_id(2)
    qt = qt_ref[...]
    row = lax.broadcasted_iota(jnp.int32, qt.shape, 0)
    zero = jnp.zeros_like(qt)
    q2 = jnp.concatenate([jnp.where(row < ATT_HEAD_DIM, qt, zero),
                          jnp.where(row >= ATT_HEAD_DIM, qt, zero)], axis=1)
    n2 = 2 * tq
    acc_s[...] = jnp.zeros(acc_s.shape, F32)

    def step(j, m_old, l_old, masked):
        k0 = pl.multiple_of(j * tk, tk)
        kb = k_ref[pl.ds(k0, tk), :]
        vtb = vt_ref[:, pl.ds(k0, tk)]
        s = _dot(kb, q2)
        if masked:
            kpos = k0 + lax.broadcasted_iota(jnp.int32, (tk, tq), 0)
            qpos = qi * tq + lax.broadcasted_iota(jnp.int32, (tk, tq), 1)
            ok = kpos <= qpos
            s = jnp.where(jnp.concatenate([ok, ok], axis=1), s, -jnp.inf)
        s3 = s.reshape(tk // SUBLANES, SUBLANES, n2)
        m_new = jnp.maximum(m_old, _sublane_all(jnp.max(s3, axis=0), jnp.maximum))
        p3 = jnp.exp2(s3 - m_new[None])
        corr = jnp.exp2(m_old - m_new)
        l_new = l_old * corr + jnp.sum(p3, axis=0)
        pv = _dot(vtb, p3.reshape(tk, n2).astype(BF16))
        acc3 = acc_s[...].reshape(LANES // SUBLANES, SUBLANES, n2)
        acc_s[...] = (acc3 * corr[None]).reshape(LANES, n2) + pv
        return m_new, l_new

    n_full = (qi * tq) // tk
    m0 = jnp.full((SUBLANES, n2), -jnp.inf, F32)
    l0 = jnp.zeros((SUBLANES, n2), F32)
    m, l = lax.fori_loop(0, n_full, lambda j, c: step(j, c[0], c[1], False), (m0, l0))
    for d in range(tq // tk):
        m, l = step(n_full + d, m, l, True)

    lam = (jnp.exp(jnp.sum(lq1_ref[...] * lk1_ref[...], axis=-1, keepdims=True))
           - jnp.exp(jnp.sum(lq2_ref[...] * lk2_ref[...], axis=-1, keepdims=True))
           + LAMBDA_INIT)
    l = _sublane_all(l, jnp.add)
    o3 = acc_s[...].reshape(LANES // SUBLANES, SUBLANES, n2) / l[None]
    d3 = o3[:, :, :tq] - lam * o3[:, :, tq:]
    ms = _sublane_all(jnp.sum(d3 * d3, axis=0), jnp.add) * (1.0 / LANES)
    d3 = d3 * lax.rsqrt(ms + SUBLN_EPS)[None]
    o = d3.reshape(LANES, tq).T
    o_ref[...] = (o * gain_ref[...] * (1.0 - LAMBDA_INIT)).astype(o_ref.dtype)


def diff_attention(qt, kk, k_col0, vt, lq1, lk1, lq2, lk2, subln_gain, tq=1024, tk=1024):
    b, h, _, s = qt.shape
    tq = min(tq, s)
    tk = min(tk, tq)
    vec = lambda a: a.reshape(1, -1).astype(F32)
    small = lambda n: pl.BlockSpec((1, n), lambda bi, hi, i: (0, 0))
    return pl.pallas_call(
        functools.partial(_attn_kernel, tq=tq, tk=tk),
        grid=(b, h, s // tq),
        in_specs=[pl.BlockSpec((None, None, LANES, tq), lambda bi, hi, i: (bi, hi, 0, i)),
                  pl.BlockSpec((None, s, LANES), lambda bi, hi, i: (bi, 0, k_col0 + hi)),
                  pl.BlockSpec((None, None, LANES, s), lambda bi, hi, i: (bi, hi, 0, 0)),
                  small(ATT_HEAD_DIM), small(ATT_HEAD_DIM), small(ATT_HEAD_DIM), small(ATT_HEAD_DIM),
                  small(LANES)],
        out_specs=pl.BlockSpec((None, tq, LANES), lambda bi, hi, i: (bi, i, hi)),
        out_shape=jax.ShapeDtypeStruct((b, s, h * LANES), BF16),
        scratch_shapes=[pltpu.VMEM((LANES, 2 * tq), F32)],
        compiler_params=_params("parallel", "parallel", "arbitrary"),
        name="diff_attention",
    )(qt, kk, vt, vec(lq1), vec(lk1), vec(lq2), vec(lk2), vec(subln_gain))


def _bdot(a, b):
    return jnp.dot(a.astype(BF16), b.astype(BF16), preferred_element_type=F32)


def _dot3(a, b):
    a_hi = a.astype(BF16)
    a_lo = (a - a_hi.astype(F32)).astype(BF16)
    b_hi = b.astype(BF16)
    b_lo = (b - b_hi.astype(F32)).astype(BF16)
    d = functools.partial(jnp.dot, preferred_element_type=F32)
    return d(a_hi, b_hi) + (d(a_hi, b_lo) + d(a_lo, b_hi))


def _each(f, *lists):
    return [f(*xs) for xs in zip(*lists)]


def _rwkv_pairs(pairs, states, c):
    n = RWKV_HEAD_DIM
    c2 = 2 * c
    row = lax.broadcasted_iota(jnp.int32, (c2, c2), 0)
    col = lax.broadcasted_iota(jnp.int32, (c2, c2), 1)
    lower = row >= col
    strict = row > col
    eye2 = (row == col).astype(F32)
    rc = lax.broadcasted_iota(jnp.int32, (c, c), 0)
    cc = lax.broadcasted_iota(jnp.int32, (c, c), 1)
    lower_b = (rc >= cc).astype(BF16)
    li = lax.broadcasted_iota(jnp.int32, (LANES, LANES), 0)
    lj = lax.broadcasted_iota(jnp.int32, (LANES, LANES), 1)
    same_head = ((li // n) == (lj // n)).astype(BF16)
    eye_l = (li == lj).astype(F32)
    head1 = lax.broadcasted_iota(jnp.int32, (c, LANES), 1) >= n

    def stack(x):
        return jnp.concatenate([jnp.where(head1, 0.0, x), jnp.where(head1, x, 0.0)], axis=0)

    def head_sum(x):
        return jnp.dot(x.astype(BF16), same_head, preferred_element_type=F32)

    r, k, v, lw, ai, g, kk_w, ka_w, rk_w, lnw, lnb = [list(x) for x in zip(*pairs)]

    kk = _each(lambda k, w: k * w, k, kk_w)
    kk = _each(lambda kk: kk / jnp.maximum(jnp.sqrt(head_sum(kk * kk)), 1e-12), kk)
    k2 = _each(lambda k, ai, w: k * (1.0 + (ai - 1.0) * w), k, ai, ka_w)
    b = _each(lambda kk, ai: kk * ai, kk, ai)
    bonus = _each(lambda r, k2, w, v: head_sum(r * k2 * w) * v, r, k2, rk_w, v)

    def cumsum(lw):
        hi = lw.astype(BF16)
        r1 = lw - hi.astype(F32)
        mid = r1.astype(BF16)
        lo = (r1 - mid.astype(F32)).astype(BF16)
        c3 = jnp.dot(lower_b, jnp.concatenate([hi, mid, lo], axis=1), preferred_element_type=F32)
        return c3[:, :LANES] + c3[:, LANES:2 * LANES] + c3[:, 2 * LANES:]

    cum = _each(cumsum, lw)
    e_cum = _each(jnp.exp, cum)
    e_neg = _each(lambda cm: jnp.exp(-cm), cum)
    e_tail = _each(lambda cm: jnp.exp(cm[c - 1:c, :] - cm), cum)
    at = _each(lambda kk, cm, l: stack(-kk * jnp.exp(cm - l)), kk, cum, lw)
    rt = _each(lambda r, e: stack(r * e), r, e_cum)
    vs = _each(stack, v)
    qq = _each(lambda x, y: jnp.concatenate([x, y], axis=0).astype(BF16), at, rt)
    kq = _each(lambda b, k2, e: jnp.concatenate([stack(b * e), stack(k2 * e)], axis=0).astype(BF16),
               b, k2, e_neg)
    s = _each(_dot_t, qq, kq)
    a_ab = _each(lambda s: jnp.where(strict, s[:c2, :c2], 0.0), s)
    a_ak = _each(lambda s: jnp.where(strict, s[:c2, c2:], 0.0), s)
    a_rb = _each(lambda s: jnp.where(lower, s[c2:, :c2], 0.0), s)
    a_rk = _each(lambda s: jnp.where(lower, s[c2:, c2:], 0.0), s)

    def level(m):
        return ((row // (2 * m)) == (col // (2 * m))) & ((row // m) % 2 == 1) & ((col // m) % 2 == 0)

    t = _each(lambda x: eye2 + jnp.where(level(1), x, 0.0), a_ab)
    m = 2
    while m < c:
        sel = level(m)
        tn = _each(lambda t, x: _bdot(t, jnp.where(sel, x, 0.0)), t, a_ab)
        t = _each(lambda t, x: t + _bdot(x, t), t, tn)
        m *= 2

    x1 = _each(_bdot, a_ak, vs)
    wu = _each(lambda t, at, x1: _bdot(t, jnp.concatenate([at, x1], axis=1)), t, at, x1)
    bh_t = _each(lambda b, e: stack(b * e).T, b, e_tail)
    kh_t = _each(lambda k2, e: stack(k2 * e).T, k2, e_tail)
    z1 = _each(lambda a, bt, wu: _bdot(jnp.concatenate([a, bt], axis=0), wu), a_rb, bh_t, wu)
    z2 = _each(lambda a, kt, v: _bdot(jnp.concatenate([a, kt], axis=0), v), a_rk, kh_t, vs)
    pm = _each(lambda z1, rt, e: jnp.concatenate([rt + z1[:c2, :LANES],
                                                  eye_l * e[c - 1:c, :] + z1[c2:, :LANES]], axis=0),
               z1, rt, e_cum)
    yn0 = _each(lambda z1, z2: z1[:, LANES:] + z2, z1, z2)
    chunks = len(pairs) // len(states)
    h_next = list(states)
    yh = [None] * len(pairs)
    for j in range(chunks):
        for blk in range(len(states)):
            i = blk * chunks + j
            yh[i] = _bdot(pm[i], h_next[blk]) + yn0[i]
            h_next[blk] = yh[i][c2:]

    y = _each(lambda yh: yh[:c] + yh[c:c2], yh)
    d = _each(lambda y: y - head_sum(y) * (1.0 / n), y)
    var = _each(lambda d: head_sum(d * d) * (1.0 / n), d)
    outs = _each(lambda d, var, lnw, lnb, bonus, g: (d * lax.rsqrt(var + GN_EPS) * lnw + lnb + bonus) * g,
                 d, var, lnw, lnb, bonus, g)
    return outs, h_next


def _rwkv_kernel(rkv_ref, lora_ref, mu_rkv_ref, mu_lora_ref, w0_ref, wl_ref, a0_ref, al_ref, gl_ref,
                 kk_ref, ka_ref, rk_ref, lnw_ref, lnb_ref, o_ref,
                 state_s, prev_rkv_s, prev_lora_s, r_s, k_s, v_s, lw_s, ai_s, g_s, *, c):
    ci = pl.program_id(1)
    width = RWKV_HEADS * RWKV_HEAD_DIM

    @pl.when(ci == 0)
    def _():
        state_s[...] = jnp.zeros(state_s.shape, F32)
        prev_rkv_s[...] = jnp.zeros(prev_rkv_s.shape, F32)
        prev_lora_s[...] = jnp.zeros(prev_lora_s.shape, F32)

    def shift_mix(p, prev_s, mu):
        rolled = pltpu.roll(p, 1, 0)
        first = lax.broadcasted_iota(jnp.int32, p.shape, 0) == 0
        shifted = jnp.where(first, prev_s[7:8, :], rolled)
        prev_s[...] = p[p.shape[0] - 8:, :]
        return p + mu * (shifted - p)

    z = shift_mix(rkv_ref[...], prev_rkv_s, mu_rkv_ref[...])
    zl = shift_mix(lora_ref[...], prev_lora_s, mu_lora_ref[...])
    r_s[...] = z[:, :width]
    k_s[...] = z[:, width:2 * width]
    v_s[...] = z[:, 2 * width:]
    xw = zl[:, :DECAY_LORA]
    xa = zl[:, DECAY_LORA:DECAY_LORA + ICLR_LORA]
    xg = zl[:, DECAY_LORA + ICLR_LORA:DECAY_LORA + ICLR_LORA + GATE_LORA]
    w_logit = -jax.nn.softplus(-(w0_ref[...] + _dot3(jnp.tanh(xw), wl_ref[...]))) - 0.5
    lw_s[...] = -jnp.exp(w_logit)
    ai_s[...] = jax.nn.sigmoid(a0_ref[...] + _dot3(xa, al_ref[...]))
    g_s[...] = _dot3(jax.nn.sigmoid(xg), gl_ref[...])

    def group_body(gi, carry):
        slices = [pl.ds(pl.multiple_of((gi * RWKV_GROUP_BLOCKS + blk) * LANES, LANES), LANES)
                  for blk in range(RWKV_GROUP_BLOCKS)]
        chunk_rows = [pl.ds(j * c, c) for j in range(r_s.shape[0] // c)]
        pairs = [tuple(ref[rows, sl] for ref in (r_s, k_s, v_s, lw_s, ai_s, g_s))
                 + tuple(ref[:, sl] for ref in (kk_ref, ka_ref, rk_ref, lnw_ref, lnb_ref))
                 for sl in slices for rows in chunk_rows]
        states = [state_s[gi * RWKV_GROUP_BLOCKS + blk] for blk in range(RWKV_GROUP_BLOCKS)]
        outs, h_next = _rwkv_pairs(pairs, states, c)
        for blk, sl in enumerate(slices):
            for j, rows in enumerate(chunk_rows):
                o_ref[rows, sl] = outs[blk * len(chunk_rows) + j].astype(o_ref.dtype)
            state_s[gi * RWKV_GROUP_BLOCKS + blk] = h_next[blk]
        return carry

    lax.fori_loop(0, width // (LANES * RWKV_GROUP_BLOCKS), group_body, 0)


def rwkv7(rkv, lora, mu_rkv, mu_lora, w0, wl, a0, al, gl, k_k, k_a, r_k, lnx_w, lnx_b):
    b, s, _ = rkv.shape
    c = min(RWKV_CHUNK, s)
    rows = min(RWKV_STEP_CHUNKS * c, s)
    width = RWKV_HEADS * RWKV_HEAD_DIM
    row = lambda a: a.reshape(1, -1).astype(F32)
    const = lambda shape: pl.BlockSpec(shape, lambda bi, ci: (0,) * len(shape))
    return pl.pallas_call(
        functools.partial(_rwkv_kernel, c=c),
        grid=(b, s // rows),
        in_specs=[pl.BlockSpec((None, rows, 3 * width), lambda bi, ci: (bi, ci, 0)),
                  pl.BlockSpec((None, rows, LORA_PAD), lambda bi, ci: (bi, ci, 0)),
                  const((1, 3 * width)), const((1, LORA_PAD)),
                  const((1, width)), const((DECAY_LORA, width)),
                  const((1, width)), const((ICLR_LORA, width)), const((GATE_LORA, width)),
                  const((1, width)), const((1, width)), const((1, width)),
                  const((1, width)), const((1, width))],
        out_specs=pl.BlockSpec((None, rows, width), lambda bi, ci: (bi, ci, 0)),
        out_shape=jax.ShapeDtypeStruct((b, s, width), BF16),
        scratch_shapes=[pltpu.VMEM((width // LANES, LANES, LANES), F32),
                        pltpu.VMEM((8, 3 * width), F32), pltpu.VMEM((8, LORA_PAD), F32)]
                       + [pltpu.VMEM((rows, width), F32)] * 6,
        compiler_params=_params("parallel", "arbitrary"),
        name="rwkv7",
    )(rkv, lora, row(mu_rkv), row(mu_lora), row(w0), wl.astype(F32), row(a0), al.astype(F32),
      gl.astype(F32), row(k_k), row(k_a), row(r_k), row(lnx_w), row(lnx_b))


def _merge_kernel(ya_ref, yr_ref, g0_ref, g1_ref, wa_ref, wr_ref, o_ref):
    za = _dot(ya_ref[...], wa_ref[...])
    zr = _dot(yr_ref[...], wr_ref[...])
    ga = jax.nn.sigmoid(g0_ref[...].astype(F32))
    gr = jax.nn.sigmoid(g1_ref[...].astype(F32))
    o_ref[...] = (ga * za + gr * zr).astype(o_ref.dtype)


def gated_merge(y_att, y_rwkv, vg, gate_col0, wa, wr, tm=1024, tn=512):
    b, s, kdim = y_att.shape
    n = wa.shape[1]
    tm = min(tm, s)
    nb = n // tn
    return pl.pallas_call(
        _merge_kernel,
        grid=(b, s // tm, nb),
        in_specs=[pl.BlockSpec((None, tm, kdim), lambda bi, i, j: (bi, i, 0)),
                  pl.BlockSpec((None, tm, kdim), lambda bi, i, j: (bi, i, 0)),
                  pl.BlockSpec((None, tm, tn), lambda bi, i, j: (bi, i, gate_col0 + j)),
                  pl.BlockSpec((None, tm, tn), lambda bi, i, j: (bi, i, gate_col0 + nb + j)),
                  pl.BlockSpec((kdim, tn), lambda bi, i, j: (0, j)),
                  pl.BlockSpec((kdim, tn), lambda bi, i, j: (0, j))],
        out_specs=pl.BlockSpec((None, tm, tn), lambda bi, i, j: (bi, i, j)),
        out_shape=jax.ShapeDtypeStruct((b, s, n), BF16),
        compiler_params=_params("parallel", "parallel", "arbitrary"),
        name="gated_merge",
    )(y_att, y_rwkv, vg, vg, wa, wr)


def _out_proj_kernel(m_ref, w_ref, x_ref, mod_ref, gain_ref, x1_ref, h2_ref):
    x1 = x_ref[...] + mod_ref[2:3, :] * _dot(m_ref[...], w_ref[...])
    x1_ref[...] = x1
    xn = x1 * lax.rsqrt(jnp.mean(x1 * x1, axis=-1, keepdims=True) + NORM_EPS)
    h2_ref[...] = (xn * gain_ref[...] * (1.0 + mod_ref[4:5, :]) + mod_ref[3:4, :]).astype(h2_ref.dtype)


def out_proj(merged, w_out, x, mod, gain2, tm=256):
    b, s, d = x.shape
    tm = min(tm, s)
    return pl.pallas_call(
        _out_proj_kernel,
        grid=(b, s // tm),
        in_specs=[pl.BlockSpec((None, tm, d), lambda bi, i: (bi, i, 0)),
                  pl.BlockSpec((d, d), lambda bi, i: (0, 0)),
                  pl.BlockSpec((None, tm, d), lambda bi, i: (bi, i, 0)),
                  pl.BlockSpec((None, 8, d), lambda bi, i: (bi, 0, 0)),
                  pl.BlockSpec((1, d), lambda bi, i: (0, 0))],
        out_specs=[pl.BlockSpec((None, tm, d), lambda bi, i: (bi, i, 0)),
                   pl.BlockSpec((None, tm, d), lambda bi, i: (bi, i, 0))],
        out_shape=[jax.ShapeDtypeStruct((b, s, d), F32), jax.ShapeDtypeStruct((b, s, d), BF16)],
        compiler_params=_params("parallel", "parallel"),
        name="out_proj",
    )(merged, w_out, x, mod, gain2.reshape(1, d))


def _glu_down_kernel(ug_ref, halo_ref, uv_ref, cw_ref, cb_ref, wd_ref, x1_ref, mod_ref, gain_ref, o_ref,
                     acc_s, *, tm):
    i = pl.program_id(1)
    f = pl.program_id(2)

    @pl.when(f == 0)
    def _():
        acc_s[...] = jnp.zeros(acc_s.shape, F32)

    ug = ug_ref[...].astype(F32)
    halo = halo_ref[...].astype(F32) * (i > 0).astype(F32)
    row8 = lax.broadcasted_iota(jnp.int32, halo.shape, 0)

    def shifted(k):
        rolled = pltpu.roll(ug, k, 0)
        top = jnp.where(row8 < k, pltpu.roll(halo, k, 0), rolled[:8])
        return jnp.concatenate([top, rolled[8:]], axis=0)

    cw = cw_ref[...]
    conv = cb_ref[...] + shifted(2) * cw[0:1, :] + shifted(1) * cw[1:2, :] + ug * cw[2:3, :]
    gelu = 0.5 * conv * (1.0 + lax.erf(conv * (2.0 ** -0.5)))
    act = (gelu * uv_ref[...].astype(F32)).astype(BF16)
    acc_s[...] += _dot(act, wd_ref[...])

    @pl.when(f == pl.num_programs(2) - 1)
    def _():
        x2 = x1_ref[...] + mod_ref[5:6, :] * acc_s[...]
        o_ref[...] = x2 * lax.rsqrt(jnp.mean(x2 * x2, axis=-1, keepdims=True) + NORM_EPS) * gain_ref[...]


def glu_down(u, conv_w, conv_b, w_down, x1, mod, final_gain, tm=512, tf=512):
    b, s, d = x1.shape
    ff = w_down.shape[0]
    tm = min(tm, s)
    nf = ff // tf
    cw8 = jnp.zeros((8, ff), F32).at[:conv_w.shape[0]].set(conv_w)
    hb = tm // 8
    return pl.pallas_call(
        functools.partial(_glu_down_kernel, tm=tm),
        grid=(b, s // tm, nf),
        in_specs=[pl.BlockSpec((None, tm, tf), lambda bi, i, f: (bi, i, f)),
                  pl.BlockSpec((None, 8, tf), lambda bi, i, f: (bi, jnp.maximum(i * hb - 1, 0), f)),
                  pl.BlockSpec((None, tm, tf), lambda bi, i, f: (bi, i, nf + f)),
                  pl.BlockSpec((8, tf), lambda bi, i, f: (0, f)),
                  pl.BlockSpec((1, tf), lambda bi, i, f: (0, f)),
                  pl.BlockSpec((tf, d), lambda bi, i, f: (f, 0)),
                  pl.BlockSpec((None, tm, d), lambda bi, i, f: (bi, i, 0)),
                  pl.BlockSpec((None, 8, d), lambda bi, i, f: (bi, 0, 0)),
                  pl.BlockSpec((1, d), lambda bi, i, f: (0, 0))],
        out_specs=pl.BlockSpec((None, tm, d), lambda bi, i, f: (bi, i, 0)),
        out_shape=jax.ShapeDtypeStruct((b, s, d), F32),
        scratch_shapes=[pltpu.VMEM((tm, d), F32)],
        compiler_params=_params("parallel", "parallel", "arbitrary"),
        name="glu_down",
    )(u, u, u, cw8, conv_b.reshape(1, ff), w_down, x1, mod, final_gain.reshape(1, d))


def kernel(x, c, w_ada, b_ada, norm1_gain, w_in, lambda_q1, lambda_k1, lambda_q2, lambda_k2, subln_gain, mu_shift, w0, w_lora_up, a0, a_lora_up, g_lora_up, k_k, k_a, r_k, lnx_w, lnx_b, w_branch, w_out, norm2_gain, w_up, conv_w, conv_b, w_down, final_gain):
    b, s, d = x.shape
    depth = w_in.shape[0]
    assert depth == 1
    l = 0
    qk_cols = 2 * ATT_HEADS * 2 * ATT_HEAD_DIM
    v_cols = ATT_HEADS * 2 * ATT_HEAD_DIM
    width = RWKV_HEADS * RWKV_HEAD_DIM
    n_lora = DECAY_LORA + ICLR_LORA + GATE_LORA
    rkv0 = qk_cols + v_cols
    lora0 = rkv0 + 3 * width
    gate0 = lora0 + n_lora

    c8 = jnp.zeros((8, d), F32).at[:b].set(c)
    mod = ada_modulation(c8, w_ada[l], b_ada[l])[:b]
    mod = jnp.pad(mod.reshape(b, 6, d), ((0, 0), (0, 2), (0, 0)))

    h = norm_mod(x, mod, norm1_gain[l], shift_row=0, scale_row=1)

    w_in_l = w_in[l]
    w_qk = w_in_l[:, :qk_cols].astype(BF16)
    w_vg = jnp.concatenate([w_in_l[:, qk_cols:rkv0], w_in_l[:, gate0:]], axis=1).astype(BF16)
    w_rkv = w_in_l[:, rkv0:lora0].astype(BF16)
    w_lora = jnp.pad(w_in_l[:, lora0:gate0], ((0, 0), (0, LORA_PAD - n_lora))).astype(BF16)

    cos, sina, sinb = rope_tables(s)
    qk = matmul(h, w_qk, BF16, rope=(cos, sina, sinb, qk_cols // 2))
    vg = matmul(h, w_vg, BF16)
    rkv = matmul(h, w_rkv, F32)
    lora = matmul(h, w_lora, F32, tn=LORA_PAD)

    per_head_t = lambda a: a.reshape(b, s, ATT_HEADS, LANES).transpose(0, 2, 3, 1)
    y_att = diff_attention(per_head_t(qk[..., :qk_cols // 2]), qk, ATT_HEADS, per_head_t(vg[..., :v_cols]),
                           lambda_q1[l], lambda_k1[l], lambda_q2[l], lambda_k2[l], subln_gain[l])

    mu = mu_shift[l]
    mu_lora = jnp.pad(mu[3 * width:], (0, LORA_PAD - n_lora))
    y_rwkv = rwkv7(rkv, lora, mu[:3 * width], mu_lora, w0[l], w_lora_up[l], a0[l], a_lora_up[l],
                   g_lora_up[l], k_k[l], k_a[l], r_k[l].reshape(-1), lnx_w[l], lnx_b[l])

    tn_merge = 512
    merged = gated_merge(y_att, y_rwkv, vg, v_cols // tn_merge, w_branch[l, 0].astype(BF16),
                         w_branch[l, 1].astype(BF16), tn=tn_merge)
    x1, h2 = out_proj(merged, w_out[l].astype(BF16), x, mod, norm2_gain[l])

    u = matmul(h2, w_up[l].astype(BF16), BF16)
    return glu_down(u, conv_w[l], conv_b[l], w_down[l].astype(BF16), x1, mod, final_gain)
```

```python
import functools
import math

import jax
import jax.numpy as jnp
from jax import lax
from jax.experimental import pallas as pl
from jax.experimental.pallas import tpu as pltpu

F32 = jnp.float32
BF16 = jnp.bfloat16
HIGHEST = lax.Precision.HIGHEST

ATT_HEADS = 8
ATT_HEAD_DIM = 64
ROT_DIM = ATT_HEAD_DIM // 4
ROPE_THETA = 500000.0
SUBLN_EPS = 1e-5
RWKV_HEADS = 16
RWKV_HEAD_DIM = 64
DECAY_LORA = 64
ICLR_LORA = 64
GATE_LORA = 160
GN_EPS = 64e-5
NORM_EPS = 1e-6
LAMBDA_INIT = 0.8 - 0.6 * math.exp(-0.3 * 0)

LANES = 128
VMEM_LIMIT = 48 * 1024 * 1024
RWKV_CHUNK = 64
RWKV_STEP_CHUNKS = 2
RWKV_GROUP_BLOCKS = 8
LORA_PAD = 384


def _params(*sem):
    return pltpu.CompilerParams(dimension_semantics=sem, vmem_limit_bytes=VMEM_LIMIT)


def _dot(a, b, precision=None):
    return jnp.dot(a, b, preferred_element_type=F32, precision=precision)


def _dot_t(a, b, precision=None):
    return lax.dot_general(a, b, (((1,), (1,)), ((), ())),
                           preferred_element_type=F32, precision=precision)


def _mod_kernel(c_ref, w_ref, b_ref, o_ref):
    c = c_ref[...]
    s = c * jax.nn.sigmoid(c)
    o_ref[...] = _dot(s, w_ref[...], HIGHEST) + b_ref[...]


def ada_modulation(c8, w_ada, b_ada, tn=1024):
    m, d = c8.shape
    n = w_ada.shape[1]
    return pl.pallas_call(
        _mod_kernel,
        grid=(n // tn,),
        in_specs=[pl.BlockSpec((m, d), lambda j: (0, 0)),
                  pl.BlockSpec((d, tn), lambda j: (0, j)),
                  pl.BlockSpec((1, tn), lambda j: (0, j))],
        out_specs=pl.BlockSpec((m, tn), lambda j: (0, j)),
        out_shape=jax.ShapeDtypeStruct((m, n), F32),
        compiler_params=_params("arbitrary"),
        name="ada_mod",
    )(c8, w_ada, b_ada.reshape(1, n))


def _norm_mod_kernel(x_ref, mod_ref, gain_ref, o_ref, *, shift_row, scale_row):
    x = x_ref[...]
    xn = x * lax.rsqrt(jnp.mean(x * x, axis=-1, keepdims=True) + NORM_EPS)
    h = xn * gain_ref[...] * (1.0 + mod_ref[scale_row:scale_row + 1, :]) + mod_ref[shift_row:shift_row + 1, :]
    o_ref[...] = h.astype(o_ref.dtype)


def norm_mod(x, mod, gain, shift_row, scale_row, tm=512):
    b, s, d = x.shape
    tm = min(tm, s)
    return pl.pallas_call(
        functools.partial(_norm_mod_kernel, shift_row=shift_row, scale_row=scale_row),
        grid=(b, s // tm),
        in_specs=[pl.BlockSpec((None, tm, d), lambda bi, i: (bi, i, 0)),
                  pl.BlockSpec((None, 8, d), lambda bi, i: (bi, 0, 0)),
                  pl.BlockSpec((1, d), lambda bi, i: (0, 0))],
        out_specs=pl.BlockSpec((None, tm, d), lambda bi, i: (bi, i, 0)),
        out_shape=jax.ShapeDtypeStruct((b, s, d), BF16),
        compiler_params=_params("parallel", "parallel"),
        name="norm_mod",
    )(x, mod, gain.reshape(1, d))


def _mm_kernel(a_ref, w_ref, o_ref):
    o_ref[...] = _dot(a_ref[...], w_ref[...]).astype(o_ref.dtype)


def _mm_rope_kernel(a_ref, w_ref, cos_ref, sina_ref, sinb_ref, o_ref, *, tn, q_cols):
    acc = _dot(a_ref[...], w_ref[...])
    scale = jnp.where(pl.program_id(2) * tn < q_cols, ATT_HEAD_DIM ** -0.5 * math.log2(math.e), 1.0).astype(F32)
    cos, sina, sinb = cos_ref[...], sina_ref[...], sinb_ref[...]
    for j in range(tn // LANES):
        t = acc[:, j * LANES:(j + 1) * LANES]
        half = ROT_DIM // 2
        r = t * cos + pltpu.roll(t, LANES - half, 1) * sina + pltpu.roll(t, half, 1) * sinb
        o_ref[:, j * LANES:(j + 1) * LANES] = (r * scale).astype(o_ref.dtype)


def matmul(a, w, out_dtype, tm=1024, tn=512, rope=None):
    b, s, k = a.shape
    n = w.shape[1]
    tm = min(tm, s)
    tn = min(tn, n)
    assert s % tm == 0 and n % tn == 0
    in_specs = [pl.BlockSpec((None, tm, k), lambda bi, i, j: (bi, i, 0)),
                pl.BlockSpec((k, tn), lambda bi, i, j: (0, j))]
    args = [a, w]
    if rope is None:
        body = _mm_kernel
        name = "matmul"
    else:
        cos, sina, sinb, q_cols = rope
        body = functools.partial(_mm_rope_kernel, tn=tn, q_cols=q_cols)
        in_specs += [pl.BlockSpec((tm, LANES), lambda bi, i, j: (i, 0))] * 3
        args += [cos, sina, sinb]
        name = "matmul_rope"
    return pl.pallas_call(
        body,
        grid=(b, s // tm, n // tn),
        in_specs=in_specs,
        out_specs=pl.BlockSpec((None, tm, tn), lambda bi, i, j: (bi, i, j)),
        out_shape=jax.ShapeDtypeStruct((b, s, n), out_dtype),
        compiler_params=_params("parallel", "parallel", "arbitrary"),
        name=name,
    )(*args)


def _mm_t_kernel(wt_ref, a_ref, cos_ref, sina_ref, sinb_ref, o_ref):
    acc = _dot_t(wt_ref[...], a_ref[...])
    tn, tm = acc.shape
    is_q = pl.program_id(2) == 0
    cos = jnp.where(is_q, cos_ref[...], 1.0)
    sina = jnp.where(is_q, sina_ref[...], 0.0)
    sinb = jnp.where(is_q, sinb_ref[...], 0.0)
    scale = jnp.where(is_q, ATT_HEAD_DIM ** -0.5 * math.log2(math.e), 1.0).astype(F32)
    half = ROT_DIM // 2
    per_map = lambda x: x.reshape(tn // ATT_HEAD_DIM, ATT_HEAD_DIM, tm)
    r = (per_map(acc) * cos[None] + per_map(pltpu.roll(acc, tn - half, 0)) * sina[None]
         + per_map(pltpu.roll(acc, half, 0)) * sinb[None])
    o_ref[...] = (r.reshape(tn, tm) * scale).astype(o_ref.dtype)


def matmul_t(a, wt, cos_t, sina_t, sinb_t, tm=512, tn=1024):
    b, s, k = a.shape
    n = wt.shape[0]
    tm = min(tm, s)
    table = pl.BlockSpec((ATT_HEAD_DIM, tm), lambda bi, i, j: (0, i))
    return pl.pallas_call(
        _mm_t_kernel,
        grid=(b, s // tm, n // tn),
        in_specs=[pl.BlockSpec((tn, k), lambda bi, i, j: (j, 0)),
                  pl.BlockSpec((None, tm, k), lambda bi, i, j: (bi, i, 0)),
                  table, table, table],
        out_specs=pl.BlockSpec((None, tn, tm), lambda bi, i, j: (bi, j, i)),
        out_shape=jax.ShapeDtypeStruct((b, n, s), BF16),
        compiler_params=_params("parallel", "parallel", "arbitrary"),
        name="matmul_t",
    )(wt, a, cos_t, sina_t, sinb_t)


def rope_tables(seq):
    half = ROT_DIM // 2
    pos = jnp.arange(seq, dtype=F32)
    inv = ROPE_THETA ** (-jnp.arange(0, ROT_DIM, 2, dtype=F32) / ROT_DIM)
    ang = pos[:, None] * inv[None, :]
    cos, sin = jnp.cos(ang), jnp.sin(ang)
    ones = jnp.ones((seq, ATT_HEAD_DIM - ROT_DIM), F32)
    zeros = jnp.zeros((seq, ATT_HEAD_DIM - ROT_DIM), F32)
    zh = jnp.zeros((seq, half), F32)
    cos64 = jnp.concatenate([cos, cos, ones], axis=1)
    sina64 = jnp.concatenate([-sin, zh, zeros], axis=1)
    sinb64 = jnp.concatenate([zh, sin, zeros], axis=1)
    rep = LANES // ATT_HEAD_DIM
    by_lane = (jnp.tile(cos64, (1, rep)), jnp.tile(sina64, (1, rep)), jnp.tile(sinb64, (1, rep)))
    by_row = (cos64.T, sina64.T, sinb64.T)
    return by_lane, by_row


SUBLANES = 8


def _sublane_all(x, op):
    for shift in (4, 2, 1):
        x = op(x, pltpu.roll(x, shift, 0))
    return x


def _attn_kernel(qt_ref, k_ref, vt_ref, lq1_ref, lk1_ref, lq2_ref, lk2_ref, gain_ref, o_ref,
                 acc_s, *, tq, tk):
    qi = pl.program_id(2)
    qt = qt_ref[...]
    row = lax.broadcasted_iota(jnp.int32, qt.shape, 0)
    zero = jnp.zeros_like(qt)
    q2 = jnp.concatenate([jnp.where(row < ATT_HEAD_DIM, qt, zero),
                          jnp.where(row >= ATT_HEAD_DIM, qt, zero)], axis=1)
    n2 = 2 * tq
    acc_s[...] = jnp.zeros(acc_s.shape, F32)

    def step(j, m_old, l_old, masked):
        k0 = pl.multiple_of(j * tk, tk)
        kb = k_ref[pl.ds(k0, tk), :]
        vtb = vt_ref[:, pl.ds(k0, tk)]
        s = _dot(kb, q2)
        if masked:
            kpos = k0 + lax.broadcasted_iota(jnp.int32, (tk, tq), 0)
            qpos = qi * tq + lax.broadcasted_iota(jnp.int32, (tk, tq), 1)
            ok = kpos <= qpos
            s = jnp.where(jnp.concatenate([ok, ok], axis=1), s, -jnp.inf)
        s3 = s.reshape(tk // SUBLANES, SUBLANES, n2)
        m_new = jnp.maximum(m_old, _sublane_all(jnp.max(s3, axis=0), jnp.maximum))
        p3 = jnp.exp2(s3 - m_new[None])
        corr = jnp.exp2(m_old - m_new)
        l_new = l_old * corr + jnp.sum(p3, axis=0)
        pv = _dot(vtb, p3.reshape(tk, n2).astype(BF16))
        acc3 = acc_s[...].reshape(LANES // SUBLANES, SUBLANES, n2)
        acc_s[...] = (acc3 * corr[None]).reshape(LANES, n2) + pv
        return m_new, l_new

    n_full = (qi * tq) // tk
    m0 = jnp.full((SUBLANES, n2), -jnp.inf, F32)
    l0 = jnp.zeros((SUBLANES, n2), F32)
    m, l = lax.fori_loop(0, n_full, lambda j, c: step(j, c[0], c[1], False), (m0, l0))
    for d in range(tq // tk):
        m, l = step(n_full + d, m, l, True)

    lam = (jnp.exp(jnp.sum(lq1_ref[...] * lk1_ref[...], axis=-1, keepdims=True))
           - jnp.exp(jnp.sum(lq2_ref[...] * lk2_ref[...], axis=-1, keepdims=True))
           + LAMBDA_INIT)
    l = _sublane_all(l, jnp.add)
    o3 = acc_s[...].reshape(LANES // SUBLANES, SUBLANES, n2) / l[None]
    d3 = o3[:, :, :tq] - lam * o3[:, :, tq:]
    ms = _sublane_all(jnp.sum(d3 * d3, axis=0), jnp.add) * (1.0 / LANES)
    d3 = d3 * lax.rsqrt(ms + SUBLN_EPS)[None]
    o = d3.reshape(LANES, tq).T
    o_ref[...] = (o * gain_ref[...] * (1.0 - LAMBDA_INIT)).astype(o_ref.dtype)


def diff_attention(qvt, kk, lq1, lk1, lq2, lk2, subln_gain, tq=1024, tk=1024):
    b, h2, _, s = qvt.shape
    h = h2 // 2
    tq = min(tq, s)
    tk = min(tk, tq)
    vec = lambda a: a.reshape(1, -1).astype(F32)
    small = lambda n: pl.BlockSpec((1, n), lambda bi, hi, i: (0, 0))
    return pl.pallas_call(
        functools.partial(_attn_kernel, tq=tq, tk=tk),
        grid=(b, h, s // tq),
        in_specs=[pl.BlockSpec((None, None, LANES, tq), lambda bi, hi, i: (bi, hi, 0, i)),
                  pl.BlockSpec((None, s, LANES), lambda bi, hi, i: (bi, 0, hi)),
                  pl.BlockSpec((None, None, LANES, s), lambda bi, hi, i: (bi, h + hi, 0, 0)),
                  small(ATT_HEAD_DIM), small(ATT_HEAD_DIM), small(ATT_HEAD_DIM), small(ATT_HEAD_DIM),
                  small(LANES)],
        out_specs=pl.BlockSpec((None, tq, LANES), lambda bi, hi, i: (bi, i, hi)),
        out_shape=jax.ShapeDtypeStruct((b, s, h * LANES), BF16),
        scratch_shapes=[pltpu.VMEM((LANES, 2 * tq), F32)],
        compiler_params=_params("parallel", "parallel", "arbitrary"),
        name="diff_attention",
    )(qvt, kk, qvt, vec(lq1), vec(lk1), vec(lq2), vec(lk2), vec(subln_gain))


def _bdot(a, b):
    return jnp.dot(a.astype(BF16), b.astype(BF16), preferred_element_type=F32)


def _dot3(a, b):
    a_hi = a.astype(BF16)
    a_lo = (a - a_hi.astype(F32)).astype(BF16)
    b_hi = b.astype(BF16)
    b_lo = (b - b_hi.astype(F32)).astype(BF16)
    d = functools.partial(jnp.dot, preferred_element_type=F32)
    return d(a_hi, b_hi) + (d(a_hi, b_lo) + d(a_lo, b_hi))


def _each(f, *lists):
    return [f(*xs) for xs in zip(*lists)]


def _rwkv_pairs(pairs, states, c):
    n = RWKV_HEAD_DIM
    c2 = 2 * c
    row = lax.broadcasted_iota(jnp.int32, (c2, c2), 0)
    col = lax.broadcasted_iota(jnp.int32, (c2, c2), 1)
    lower = row >= col
    strict = row > col
    eye2 = (row == col).astype(F32)
    rc = lax.broadcasted_iota(jnp.int32, (c, c), 0)
    cc = lax.broadcasted_iota(jnp.int32, (c, c), 1)
    lower_b = (rc >= cc).astype(BF16)
    li = lax.broadcasted_iota(jnp.int32, (LANES, LANES), 0)
    lj = lax.broadcasted_iota(jnp.int32, (LANES, LANES), 1)
    same_head = ((li // n) == (lj // n)).astype(BF16)
    eye_l = (li == lj).astype(F32)
    head1 = lax.broadcasted_iota(jnp.int32, (c, LANES), 1) >= n

    def stack(x):
        return jnp.concatenate([jnp.where(head1, 0.0, x), jnp.where(head1, x, 0.0)], axis=0)

    def head_sum(x):
        return jnp.dot(x.astype(BF16), same_head, preferred_element_type=F32)

    r, k, v, lw, ai, g, kk_w, ka_w, rk_w, lnw, lnb = [list(x) for x in zip(*pairs)]

    kk = _each(lambda k, w: k * w, k, kk_w)
    kk = _each(lambda kk: kk * lax.rsqrt(jnp.maximum(head_sum(kk * kk), 1e-24)), kk)
    k2 = _each(lambda k, ai, w: k * (1.0 + (ai - 1.0) * w), k, ai, ka_w)
    b = _each(lambda kk, ai: kk * ai, kk, ai)
    bonus = _each(lambda r, k2, w, v: head_sum(r * k2 * w) * v, r, k2, rk_w, v)

    def cumsum(lw):
        hi = lw.astype(BF16)
        r1 = lw - hi.astype(F32)
        mid = r1.astype(BF16)
        lo = (r1 - mid.astype(F32)).astype(BF16)
        c3 = jnp.dot(lower_b, jnp.concatenate([hi, mid, lo], axis=1), preferred_element_type=F32)
        return c3[:, :LANES] + c3[:, LANES:2 * LANES] + c3[:, 2 * LANES:]

    cum = _each(cumsum, lw)
    e_cum = _each(jnp.exp, cum)
    e_neg = _each(lambda cm: jnp.exp(-cm), cum)
    e_tail = _each(lambda cm: jnp.exp(cm[c - 1:c, :] - cm), cum)
    at = _each(lambda kk, cm, l: stack(-kk * jnp.exp(cm - l)), kk, cum, lw)
    rt = _each(lambda r, e: stack(r * e), r, e_cum)
    vs = _each(stack, v)
    qq = _each(lambda x, y: jnp.concatenate([x, y], axis=0).astype(BF16), at, rt)
    kq = _each(lambda b, k2, e: jnp.concatenate([stack(b * e), stack(k2 * e)], axis=0).astype(BF16),
               b, k2, e_neg)
    s = _each(_dot_t, qq, kq)
    a_ab = _each(lambda s: jnp.where(strict, s[:c2, :c2], 0.0), s)
    a_ak = _each(lambda s: jnp.where(strict, s[:c2, c2:], 0.0), s)
    a_rb = _each(lambda s: jnp.where(lower, s[c2:, :c2], 0.0), s)
    a_rk = _each(lambda s: jnp.where(lower, s[c2:, c2:], 0.0), s)

    def level(m):
        return ((row // (2 * m)) == (col // (2 * m))) & ((row // m) % 2 == 1) & ((col // m) % 2 == 0)

    t = _each(lambda x: eye2 + jnp.where(level(1), x, 0.0), a_ab)
    m = 2
    while m < c:
        sel = level(m)
        tn = _each(lambda t, x: _bdot(t, jnp.where(sel, x, 0.0)), t, a_ab)
        t = _each(lambda t, x: t + _bdot(x, t), t, tn)
        m *= 2

    x1 = _each(_bdot, a_ak, vs)
    wu = _each(lambda t, at, x1: _bdot(t, jnp.concatenate([at, x1], axis=1)), t, at, x1)
    bh_t = _each(lambda b, e: stack(b * e).T, b, e_tail)
    kh_t = _each(lambda k2, e: stack(k2 * e).T, k2, e_tail)
    z1 = _each(lambda a, bt, wu: _bdot(jnp.concatenate([a, bt], axis=0), wu), a_rb, bh_t, wu)
    z2 = _each(lambda a, kt, v: _bdot(jnp.concatenate([a, kt], axis=0), v), a_rk, kh_t, vs)
    pm = _each(lambda z1, rt, e: jnp.concatenate([rt + z1[:c2, :LANES],
                                                  eye_l * e[c - 1:c, :] + z1[c2:, :LANES]], axis=0),
               z1, rt, e_cum)
    yn0 = _each(lambda z1, z2: z1[:, LANES:] + z2, z1, z2)
    chunks = len(pairs) // len(states)
    h_next = list(states)
    yh = [None] * len(pairs)
    for j in range(chunks):
        for blk in range(len(states)):
            i = blk * chunks + j
            yh[i] = _bdot(pm[i], h_next[blk]) + yn0[i]
            h_next[blk] = yh[i][c2:]

    y = _each(lambda yh: yh[:c] + yh[c:c2], yh)
    d = _each(lambda y: y - head_sum(y) * (1.0 / n), y)
    var = _each(lambda d: head_sum(d * d) * (1.0 / n), d)
    outs = _each(lambda d, var, lnw, lnb, bonus, g: (d * lax.rsqrt(var + GN_EPS) * lnw + lnb + bonus) * g,
                 d, var, lnw, lnb, bonus, g)
    return outs, h_next


def _rwkv_kernel(rkv_ref, lora_ref, mu_rkv_ref, mu_lora_ref, w0_ref, wl_ref, a0_ref, al_ref, gl_ref,
                 kk_ref, ka_ref, rk_ref, lnw_ref, lnb_ref, o_ref,
                 state_s, prev_rkv_s, prev_lora_s, r_s, k_s, v_s, lw_s, ai_s, g_s, *, c):
    ci = pl.program_id(1)
    width = RWKV_HEADS * RWKV_HEAD_DIM

    @pl.when(ci == 0)
    def _():
        state_s[...] = jnp.zeros(state_s.shape, F32)
        prev_rkv_s[...] = jnp.zeros(prev_rkv_s.shape, F32)
        prev_lora_s[...] = jnp.zeros(prev_lora_s.shape, F32)

    def shift_mix(p, prev_s, mu):
        rolled = pltpu.roll(p, 1, 0)
        first = lax.broadcasted_iota(jnp.int32, p.shape, 0) == 0
        shifted = jnp.where(first, prev_s[7:8, :], rolled)
        prev_s[...] = p[p.shape[0] - 8:, :]
        return p + mu * (shifted - p)

    z = shift_mix(rkv_ref[...], prev_rkv_s, mu_rkv_ref[...])
    zl = shift_mix(lora_ref[...], prev_lora_s, mu_lora_ref[...])
    r_s[...] = z[:, :width]
    k_s[...] = z[:, width:2 * width]
    v_s[...] = z[:, 2 * width:]
    xw = zl[:, :DECAY_LORA]
    xa = zl[:, DECAY_LORA:DECAY_LORA + ICLR_LORA]
    xg = zl[:, DECAY_LORA + ICLR_LORA:DECAY_LORA + ICLR_LORA + GATE_LORA]
    w_logit = -jax.nn.softplus(-(w0_ref[...] + _dot3(jnp.tanh(xw), wl_ref[...]))) - 0.5
    lw_s[...] = -jnp.exp(w_logit)
    ai_s[...] = jax.nn.sigmoid(a0_ref[...] + _dot3(xa, al_ref[...]))
    g_s[...] = _dot3(jax.nn.sigmoid(xg), gl_ref[...])

    def group_body(gi, carry):
        slices = [pl.ds(pl.multiple_of((gi * RWKV_GROUP_BLOCKS + blk) * LANES, LANES), LANES)
                  for blk in range(RWKV_GROUP_BLOCKS)]
        chunk_rows = [pl.ds(j * c, c) for j in range(r_s.shape[0] // c)]
        pairs = [tuple(ref[rows, sl] for ref in (r_s, k_s, v_s, lw_s, ai_s, g_s))
                 + tuple(ref[:, sl] for ref in (kk_ref, ka_ref, rk_ref, lnw_ref, lnb_ref))
                 for sl in slices for rows in chunk_rows]
        states = [state_s[gi * RWKV_GROUP_BLOCKS + blk] for blk in range(RWKV_GROUP_BLOCKS)]
        outs, h_next = _rwkv_pairs(pairs, states, c)
        for blk, sl in enumerate(slices):
            for j, rows in enumerate(chunk_rows):
                o_ref[rows, sl] = outs[blk * len(chunk_rows) + j].astype(o_ref.dtype)
            state_s[gi * RWKV_GROUP_BLOCKS + blk] = h_next[blk]
        return carry

    lax.fori_loop(0, width // (LANES * RWKV_GROUP_BLOCKS), group_body, 0)


def rwkv7(rkv, lora, mu_rkv, mu_lora, w0, wl, a0, al, gl, k_k, k_a, r_k, lnx_w, lnx_b):
    b, s, _ = rkv.shape
    c = min(RWKV_CHUNK, s)
    rows = min(RWKV_STEP_CHUNKS * c, s)
    width = RWKV_HEADS * RWKV_HEAD_DIM
    row = lambda a: a.reshape(1, -1).astype(F32)
    const = lambda shape: pl.BlockSpec(shape, lambda bi, ci: (0,) * len(shape))
    return pl.pallas_call(
        functools.partial(_rwkv_kernel, c=c),
        grid=(b, s // rows),
        in_specs=[pl.BlockSpec((None, rows, 3 * width), lambda bi, ci: (bi, ci, 0)),
                  pl.BlockSpec((None, rows, LORA_PAD), lambda bi, ci: (bi, ci, 0)),
                  const((1, 3 * width)), const((1, LORA_PAD)),
                  const((1, width)), const((DECAY_LORA, width)),
                  const((1, width)), const((ICLR_LORA, width)), const((GATE_LORA, width)),
                  const((1, width)), const((1, width)), const((1, width)),
                  const((1, width)), const((1, width))],
        out_specs=pl.BlockSpec((None, rows, width), lambda bi, ci: (bi, ci, 0)),
        out_shape=jax.ShapeDtypeStruct((b, s, width), BF16),
        scratch_shapes=[pltpu.VMEM((width // LANES, LANES, LANES), F32),
                        pltpu.VMEM((8, 3 * width), F32), pltpu.VMEM((8, LORA_PAD), F32)]
                       + [pltpu.VMEM((rows, width), F32)] * 6,
        compiler_params=_params("parallel", "arbitrary"),
        name="rwkv7",
    )(rkv, lora, row(mu_rkv), row(mu_lora), row(w0), wl.astype(F32), row(a0), al.astype(F32),
      gl.astype(F32), row(k_k), row(k_a), row(r_k), row(lnx_w), row(lnx_b))


def _merge_kernel(ya_ref, yr_ref, g0_ref, g1_ref, wa_ref, wr_ref, o_ref):
    za = _dot(ya_ref[...], wa_ref[...])
    zr = _dot(yr_ref[...], wr_ref[...])
    ga = jax.nn.sigmoid(g0_ref[...].astype(F32))
    gr = jax.nn.sigmoid(g1_ref[...].astype(F32))
    o_ref[...] = (ga * za + gr * zr).astype(o_ref.dtype)


def gated_merge(y_att, y_rwkv, vg, gate_col0, wa, wr, tm=1024, tn=512):
    b, s, kdim = y_att.shape
    n = wa.shape[1]
    tm = min(tm, s)
    nb = n // tn
    return pl.pallas_call(
        _merge_kernel,
        grid=(b, s // tm, nb),
        in_specs=[pl.BlockSpec((None, tm, kdim), lambda bi, i, j: (bi, i, 0)),
                  pl.BlockSpec((None, tm, kdim), lambda bi, i, j: (bi, i, 0)),
                  pl.BlockSpec((None, tm, tn), lambda bi, i, j: (bi, i, gate_col0 + j)),
                  pl.BlockSpec((None, tm, tn), lambda bi, i, j: (bi, i, gate_col0 + nb + j)),
                  pl.BlockSpec((kdim, tn), lambda bi, i, j: (0, j)),
                  pl.BlockSpec((kdim, tn), lambda bi, i, j: (0, j))],
        out_specs=pl.BlockSpec((None, tm, tn), lambda bi, i, j: (bi, i, j)),
        out_shape=jax.ShapeDtypeStruct((b, s, n), BF16),
        compiler_params=_params("parallel", "parallel", "arbitrary"),
        name="gated_merge",
    )(y_att, y_rwkv, vg, vg, wa, wr)


def _out_proj_kernel(m_ref, w_ref, x_ref, mod_ref, gain_ref, x1_ref, h2_ref):
    x1 = x_ref[...] + mod_ref[2:3, :] * _dot(m_ref[...], w_ref[...])
    x1_ref[...] = x1
    xn = x1 * lax.rsqrt(jnp.mean(x1 * x1, axis=-1, keepdims=True) + NORM_EPS)
    h2_ref[...] = (xn * gain_ref[...] * (1.0 + mod_ref[4:5, :]) + mod_ref[3:4, :]).astype(h2_ref.dtype)


def out_proj(merged, w_out, x, mod, gain2, tm=256):
    b, s, d = x.shape
    tm = min(tm, s)
    return pl.pallas_call(
        _out_proj_kernel,
        grid=(b, s // tm),
        in_specs=[pl.BlockSpec((None, tm, d), lambda bi, i: (bi, i, 0)),
                  pl.BlockSpec((d, d), lambda bi, i: (0, 0)),
                  pl.BlockSpec((None, tm, d), lambda bi, i: (bi, i, 0)),
                  pl.BlockSpec((None, 8, d), lambda bi, i: (bi, 0, 0)),
                  pl.BlockSpec((1, d), lambda bi, i: (0, 0))],
        out_specs=[pl.BlockSpec((None, tm, d), lambda bi, i: (bi, i, 0)),
                   pl.BlockSpec((None, tm, d), lambda bi, i: (bi, i, 0))],
        out_shape=[jax.ShapeDtypeStruct((b, s, d), F32), jax.ShapeDtypeStruct((b, s, d), BF16)],
        compiler_params=_params("parallel", "parallel"),
        name="out_proj",
    )(merged, w_out, x, mod, gain2.reshape(1, d))


GLU_CHUNK = 128


def _glu_down_kernel(ug_ref, halo_ref, uv_ref, cw_ref, cb_ref, wd_ref, x1_ref, mod_ref, gain_ref, o_ref,
                     acc_s, act_a, act_b, *, nf):
    i = pl.program_id(1)
    f = pl.program_id(2)
    tf = act_a.shape[1]

    @pl.when(f == 0)
    def _():
        acc_s[...] = jnp.zeros(acc_s.shape, F32)
        act_b[...] = jnp.zeros(act_b.shape, BF16)

    def activations(cs):
        ug = ug_ref[:, cs].astype(F32)
        halo = halo_ref[:, cs].astype(F32) * (i > 0).astype(F32)
        row8 = lax.broadcasted_iota(jnp.int32, halo.shape, 0)

        def shifted(k):
            rolled = pltpu.roll(ug, k, 0)
            top = jnp.where(row8 < k, pltpu.roll(halo, k, 0), rolled[:8])
            return jnp.concatenate([top, rolled[8:]], axis=0)

        cw = cw_ref[:, cs]
        conv = cb_ref[:, cs] + shifted(2) * cw[0:1, :] + shifted(1) * cw[1:2, :] + ug * cw[2:3, :]
        gelu = 0.5 * conv * (1.0 + lax.erf(conv * (2.0 ** -0.5)))
        return (gelu * uv_ref[:, cs].astype(F32)).astype(BF16)

    def step(act_new, act_old):
        d = acc_s.shape[1]
        pieces = tf // GLU_CHUNK
        for j in range(pieces):
            if act_new is not None:
                cs = slice(j * GLU_CHUNK, (j + 1) * GLU_CHUNK)
                act_new[:, cs] = activations(cs)
            ns = slice(j * (d // pieces), (j + 1) * (d // pieces))
            acc_s[:, ns] += _dot(act_old[...], wd_ref[:, ns])

    @pl.when((f % 2 == 0) & (f < nf))
    def _():
        step(act_a, act_b)

    @pl.when((f % 2 == 1) & (f < nf))
    def _():
        step(act_b, act_a)

    @pl.when(f == nf)
    def _():
        step(None, act_a if nf % 2 == 1 else act_b)
        x2 = x1_ref[...] + mod_ref[5:6, :] * acc_s[...]
        o_ref[...] = x2 * lax.rsqrt(jnp.mean(x2 * x2, axis=-1, keepdims=True) + NORM_EPS) * gain_ref[...]


def glu_down(u, conv_w, conv_b, w_down, x1, mod, final_gain, tm=512, tf=512):
    b, s, d = x1.shape
    ff = w_down.shape[0]
    tm = min(tm, s)
    nf = ff // tf
    cw8 = jnp.zeros((8, ff), F32).at[:conv_w.shape[0]].set(conv_w)
    hb = tm // 8
    cur = lambda f: jnp.minimum(f, nf - 1)
    prev = lambda f: jnp.maximum(f - 1, 0)
    return pl.pallas_call(
        functools.partial(_glu_down_kernel, nf=nf),
        grid=(b, s // tm, nf + 1),
        in_specs=[pl.BlockSpec((None, tm, tf), lambda bi, i, f: (bi, i, cur(f))),
                  pl.BlockSpec((None, 8, tf), lambda bi, i, f: (bi, jnp.maximum(i * hb - 1, 0), cur(f))),
                  pl.BlockSpec((None, tm, tf), lambda bi, i, f: (bi, i, nf + cur(f))),
                  pl.BlockSpec((8, tf), lambda bi, i, f: (0, cur(f))),
                  pl.BlockSpec((1, tf), lambda bi, i, f: (0, cur(f))),
                  pl.BlockSpec((tf, d), lambda bi, i, f: (prev(f), 0)),
                  pl.BlockSpec((None, tm, d), lambda bi, i, f: (bi, i, 0)),
                  pl.BlockSpec((None, 8, d), lambda bi, i, f: (bi, 0, 0)),
                  pl.BlockSpec((1, d), lambda bi, i, f: (0, 0))],
        out_specs=pl.BlockSpec((None, tm, d), lambda bi, i, f: (bi, i, 0)),
        out_shape=jax.ShapeDtypeStruct((b, s, d), F32),
        scratch_shapes=[pltpu.VMEM((tm, d), F32), pltpu.VMEM((tm, tf), BF16), pltpu.VMEM((tm, tf), BF16)],
        compiler_params=_params("parallel", "parallel", "arbitrary"),
        name="glu_down",
    )(u, u, u, cw8, conv_b.reshape(1, ff), w_down, x1, mod, final_gain.reshape(1, d))


def kernel(x, c, w_ada, b_ada, norm1_gain, w_in, lambda_q1, lambda_k1, lambda_q2, lambda_k2, subln_gain, mu_shift, w0, w_lora_up, a0, a_lora_up, g_lora_up, k_k, k_a, r_k, lnx_w, lnx_b, w_branch, w_out, norm2_gain, w_up, conv_w, conv_b, w_down, final_gain):
    b, s, d = x.shape
    depth = w_in.shape[0]
    assert depth == 1
    l = 0
    qk_cols = 2 * ATT_HEADS * 2 * ATT_HEAD_DIM
    v_cols = ATT_HEADS * 2 * ATT_HEAD_DIM
    width = RWKV_HEADS * RWKV_HEAD_DIM
    n_lora = DECAY_LORA + ICLR_LORA + GATE_LORA
    rkv0 = qk_cols + v_cols
    lora0 = rkv0 + 3 * width
    gate0 = lora0 + n_lora

    c8 = jnp.zeros((8, d), F32).at[:b].set(c)
    mod = ada_modulation(c8, w_ada[l], b_ada[l])[:b]
    mod = jnp.pad(mod.reshape(b, 6, d), ((0, 0), (0, 2), (0, 0)))

    h = norm_mod(x, mod, norm1_gain[l], shift_row=0, scale_row=1)

    w_in_l = w_in[l]
    q_cols = qk_cols // 2
    w_qv_t = jnp.concatenate([w_in_l[:, :q_cols], w_in_l[:, qk_cols:rkv0]], axis=1).T.astype(BF16)
    w_k = w_in_l[:, q_cols:qk_cols].astype(BF16)
    w_gate = w_in_l[:, gate0:].astype(BF16)
    w_rkv = w_in_l[:, rkv0:lora0].astype(BF16)
    w_lora = jnp.pad(w_in_l[:, lora0:gate0], ((0, 0), (0, LORA_PAD - n_lora))).astype(BF16)

    (cos, sina, sinb), rope_rows = rope_tables(s)
    qv_t = matmul_t(h, w_qv_t, *rope_rows)
    kk = matmul(h, w_k, BF16, rope=(cos, sina, sinb, 0))
    gates = matmul(h, w_gate, BF16)
    rkv = matmul(h, w_rkv, F32)
    lora = matmul(h, w_lora, F32, tn=LORA_PAD)

    y_att = diff_attention(qv_t.reshape(b, 2 * ATT_HEADS, LANES, s), kk,
                           lambda_q1[l], lambda_k1[l], lambda_q2[l], lambda_k2[l], subln_gain[l])

    mu = mu_shift[l]
    mu_lora = jnp.pad(mu[3 * width:], (0, LORA_PAD - n_lora))
    y_rwkv = rwkv7(rkv, lora, mu[:3 * width], mu_lora, w0[l], w_lora_up[l], a0[l], a_lora_up[l],
                   g_lora_up[l], k_k[l], k_a[l], r_k[l].reshape(-1), lnx_w[l], lnx_b[l])

    tn_merge = 512
    merged = gated_merge(y_att, y_rwkv, gates, 0, w_branch[l, 0].astype(BF16),
                         w_branch[l, 1].astype(BF16), tn=tn_merge)
    x1, h2 = out_proj(merged, w_out[l].astype(BF16), x, mod, norm2_gain[l])

    u = matmul(h2, w_up[l].astype(BF16), BF16)
    return glu_down(u, conv_w[l], conv_b[l], w_down[l].astype(BF16), x1, mod, final_gain)
```

```python
import functools
import math

import jax
import jax.numpy as jnp
from jax import lax
from jax.experimental import pallas as pl
from jax.experimental.pallas import tpu as pltpu

F32 = jnp.float32
BF16 = jnp.bfloat16
HIGHEST = lax.Precision.HIGHEST

ATT_HEADS = 8
ATT_HEAD_DIM = 64
ROT_DIM = ATT_HEAD_DIM // 4
ROPE_THETA = 500000.0
SUBLN_EPS = 1e-5
RWKV_HEADS = 16
RWKV_HEAD_DIM = 64
DECAY_LORA = 64
ICLR_LORA = 64
GATE_LORA = 160
GN_EPS = 64e-5
NORM_EPS = 1e-6
LAMBDA_INIT = 0.8 - 0.6 * math.exp(-0.3 * 0)

LANES = 128
VMEM_LIMIT = 48 * 1024 * 1024
RWKV_CHUNK = 64
RWKV_STEP_CHUNKS = 2
RWKV_GROUP_BLOCKS = 8
LORA_PAD = 384


def _params(*sem):
    return pltpu.CompilerParams(dimension_semantics=sem, vmem_limit_bytes=VMEM_LIMIT)


def _dot(a, b, precision=None):
    return jnp.dot(a, b, preferred_element_type=F32, precision=precision)


def _dot_t(a, b, precision=None):
    return lax.dot_general(a, b, (((1,), (1,)), ((), ())),
                           preferred_element_type=F32, precision=precision)


def _mod_kernel(c_ref, w_ref, b_ref, o_ref):
    c = c_ref[...]
    s = c * jax.nn.sigmoid(c)
    o_ref[...] = _dot(s, w_ref[...], HIGHEST) + b_ref[...]


def ada_modulation(c8, w_ada, b_ada, tn=1024):
    m, d = c8.shape
    n = w_ada.shape[1]
    return pl.pallas_call(
        _mod_kernel,
        grid=(n // tn,),
        in_specs=[pl.BlockSpec((m, d), lambda j: (0, 0)),
                  pl.BlockSpec((d, tn), lambda j: (0, j)),
                  pl.BlockSpec((1, tn), lambda j: (0, j))],
        out_specs=pl.BlockSpec((m, tn), lambda j: (0, j)),
        out_shape=jax.ShapeDtypeStruct((m, n), F32),
        compiler_params=_params("arbitrary"),
        name="ada_mod",
    )(c8, w_ada, b_ada.reshape(1, n))


def _norm_mod_kernel(x_ref, mod_ref, gain_ref, o_ref, *, shift_row, scale_row):
    x = x_ref[...]
    xn = x * lax.rsqrt(jnp.mean(x * x, axis=-1, keepdims=True) + NORM_EPS)
    h = xn * gain_ref[...] * (1.0 + mod_ref[scale_row:scale_row + 1, :]) + mod_ref[shift_row:shift_row + 1, :]
    o_ref[...] = h.astype(o_ref.dtype)


def norm_mod(x, mod, gain, shift_row, scale_row, tm=512):
    b, s, d = x.shape
    tm = min(tm, s)
    return pl.pallas_call(
        functools.partial(_norm_mod_kernel, shift_row=shift_row, scale_row=scale_row),
        grid=(b, s // tm),
        in_specs=[pl.BlockSpec((None, tm, d), lambda bi, i: (bi, i, 0)),
                  pl.BlockSpec((None, 8, d), lambda bi, i: (bi, 0, 0)),
                  pl.BlockSpec((1, d), lambda bi, i: (0, 0))],
        out_specs=pl.BlockSpec((None, tm, d), lambda bi, i: (bi, i, 0)),
        out_shape=jax.ShapeDtypeStruct((b, s, d), BF16),
        compiler_params=_params("parallel", "parallel"),
        name="norm_mod",
    )(x, mod, gain.reshape(1, d))


def _mm_kernel(a_ref, w_ref, o_ref):
    o_ref[...] = _dot(a_ref[...], w_ref[...]).astype(o_ref.dtype)


def _mm_rope_kernel(a_ref, w_ref, cos_ref, sina_ref, sinb_ref, o_ref, *, tn, q_cols):
    acc = _dot(a_ref[...], w_ref[...])
    scale = jnp.where(pl.program_id(2) * tn < q_cols, ATT_HEAD_DIM ** -0.5 * math.log2(math.e), 1.0).astype(F32)
    cos, sina, sinb = cos_ref[...], sina_ref[...], sinb_ref[...]
    for j in range(tn // LANES):
        t = acc[:, j * LANES:(j + 1) * LANES]
        half = ROT_DIM // 2
        r = t * cos + pltpu.roll(t, LANES - half, 1) * sina + pltpu.roll(t, half, 1) * sinb
        o_ref[:, j * LANES:(j + 1) * LANES] = (r * scale).astype(o_ref.dtype)


def matmul(a, w, out_dtype, tm=1024, tn=512, rope=None):
    b, s, k = a.shape
    n = w.shape[1]
    tm = min(tm, s)
    tn = min(tn, n)
    assert s % tm == 0 and n % tn == 0
    in_specs = [pl.BlockSpec((None, tm, k), lambda bi, i, j: (bi, i, 0)),
                pl.BlockSpec((k, tn), lambda bi, i, j: (0, j))]
    args = [a, w]
    if rope is None:
        body = _mm_kernel
        name = "matmul"
    else:
        cos, sina, sinb, q_cols = rope
        body = functools.partial(_mm_rope_kernel, tn=tn, q_cols=q_cols)
        in_specs += [pl.BlockSpec((tm, LANES), lambda bi, i, j: (i, 0))] * 3
        args += [cos, sina, sinb]
        name = "matmul_rope"
    return pl.pallas_call(
        body,
        grid=(b, s // tm, n // tn),
        in_specs=in_specs,
        out_specs=pl.BlockSpec((None, tm, tn), lambda bi, i, j: (bi, i, j)),
        out_shape=jax.ShapeDtypeStruct((b, s, n), out_dtype),
        compiler_params=_params("parallel", "parallel", "arbitrary"),
        name=name,
    )(*args)


def _mm_t_kernel(wt_ref, a_ref, cos_ref, sina_ref, sinb_ref, o_ref):
    acc = _dot_t(wt_ref[...], a_ref[...])
    tn, tm = acc.shape
    is_q = pl.program_id(2) == 0
    cos = jnp.where(is_q, cos_ref[...], 1.0)
    sina = jnp.where(is_q, sina_ref[...], 0.0)
    sinb = jnp.where(is_q, sinb_ref[...], 0.0)
    scale = jnp.where(is_q, ATT_HEAD_DIM ** -0.5 * math.log2(math.e), 1.0).astype(F32)
    half = ROT_DIM // 2
    per_map = lambda x: x.reshape(tn // ATT_HEAD_DIM, ATT_HEAD_DIM, tm)
    r = (per_map(acc) * cos[None] + per_map(pltpu.roll(acc, tn - half, 0)) * sina[None]
         + per_map(pltpu.roll(acc, half, 0)) * sinb[None])
    o_ref[...] = (r.reshape(tn, tm) * scale).astype(o_ref.dtype)


def matmul_t(a, wt, cos_t, sina_t, sinb_t, tm=512, tn=1024):
    b, s, k = a.shape
    n = wt.shape[0]
    tm = min(tm, s)
    table = pl.BlockSpec((ATT_HEAD_DIM, tm), lambda bi, i, j: (0, i))
    return pl.pallas_call(
        _mm_t_kernel,
        grid=(b, s // tm, n // tn),
        in_specs=[pl.BlockSpec((tn, k), lambda bi, i, j: (j, 0)),
                  pl.BlockSpec((None, tm, k), lambda bi, i, j: (bi, i, 0)),
                  table, table, table],
        out_specs=pl.BlockSpec((None, tn, tm), lambda bi, i, j: (bi, j, i)),
        out_shape=jax.ShapeDtypeStruct((b, n, s), BF16),
        compiler_params=_params("parallel", "parallel", "arbitrary"),
        name="matmul_t",
    )(wt, a, cos_t, sina_t, sinb_t)


def rope_tables(seq):
    half = ROT_DIM // 2
    pos = jnp.arange(seq, dtype=F32)
    inv = ROPE_THETA ** (-jnp.arange(0, ROT_DIM, 2, dtype=F32) / ROT_DIM)
    ang = pos[:, None] * inv[None, :]
    cos, sin = jnp.cos(ang), jnp.sin(ang)
    ones = jnp.ones((seq, ATT_HEAD_DIM - ROT_DIM), F32)
    zeros = jnp.zeros((seq, ATT_HEAD_DIM - ROT_DIM), F32)
    zh = jnp.zeros((seq, half), F32)
    cos64 = jnp.concatenate([cos, cos, ones], axis=1)
    sina64 = jnp.concatenate([-sin, zh, zeros], axis=1)
    sinb64 = jnp.concatenate([zh, sin, zeros], axis=1)
    rep = LANES // ATT_HEAD_DIM
    by_lane = (jnp.tile(cos64, (1, rep)), jnp.tile(sina64, (1, rep)), jnp.tile(sinb64, (1, rep)))
    by_row = (cos64.T, sina64.T, sinb64.T)
    return by_lane, by_row


SUBLANES = 8


def _sublane_all(x, op):
    for shift in (4, 2, 1):
        x = op(x, pltpu.roll(x, shift, 0))
    return x


def _attn_kernel(qt_ref, k_ref, vt_ref, lq1_ref, lk1_ref, lq2_ref, lk2_ref, gain_ref, o_ref,
                 acc_s, *, tq, tk):
    qi = pl.program_id(2)
    qt = qt_ref[...]
    row = lax.broadcasted_iota(jnp.int32, qt.shape, 0)
    zero = jnp.zeros_like(qt)
    q2 = jnp.concatenate([jnp.where(row < ATT_HEAD_DIM, qt, zero),
                          jnp.where(row >= ATT_HEAD_DIM, qt, zero)], axis=1)
    n2 = 2 * tq
    acc_s[...] = jnp.zeros(acc_s.shape, F32)

    def update(k0, nk, c0, nc, m_all, l_all, masked):
        cols = slice(c0, c0 + nc)
        kb = k_ref[pl.ds(k0, nk), :]
        vtb = vt_ref[:, pl.ds(k0, nk)]
        s = _dot(kb, q2[:, cols])
        if masked:
            kpos = k0 + lax.broadcasted_iota(jnp.int32, (nk, nc), 0)
            col = c0 + lax.broadcasted_iota(jnp.int32, (nk, nc), 1)
            qpos = qi * tq + jnp.where(col >= tq, col - tq, col)
            s = jnp.where(kpos <= qpos, s, -jnp.inf)
        s3 = s.reshape(nk // SUBLANES, SUBLANES, nc)
        m_old = m_all[:, cols]
        m_new = jnp.maximum(m_old, _sublane_all(jnp.max(s3, axis=0), jnp.maximum))
        p3 = jnp.exp2(s3 - m_new[None])
        corr = jnp.exp2(m_old - m_new)
        l_new = l_all[:, cols] * corr + jnp.sum(p3, axis=0)
        pv = _dot(vtb, p3.reshape(nk, nc).astype(BF16))
        acc3 = acc_s[:, cols].reshape(LANES // SUBLANES, SUBLANES, nc)
        acc_s[:, cols] = (acc3 * corr[None]).reshape(LANES, nc) + pv
        if nc == n2:
            return m_new, l_new
        put = lambda full, part: jnp.concatenate(
            ([full[:, :c0]] if c0 else []) + [part] + ([full[:, c0 + nc:]] if c0 + nc < n2 else []), axis=1)
        return put(m_all, m_new), put(l_all, l_new)

    m0 = jnp.full((SUBLANES, n2), -jnp.inf, F32)
    l0 = jnp.zeros((SUBLANES, n2), F32)
    m, l = lax.fori_loop(0, (qi * tq) // tk,
                         lambda j, c: update(pl.multiple_of(j * tk, tk), tk, 0, n2, c[0], c[1], False), (m0, l0))
    half = tq // 2
    kd = pl.multiple_of(qi * tq, tq)
    m, l = update(kd, half, 0, n2, m, l, True)
    m, l = update(kd + half, half, half, half, m, l, True)
    m, l = update(kd + half, half, tq + half, half, m, l, True)

    lam = (jnp.exp(jnp.sum(lq1_ref[...] * lk1_ref[...], axis=-1, keepdims=True))
           - jnp.exp(jnp.sum(lq2_ref[...] * lk2_ref[...], axis=-1, keepdims=True))
           + LAMBDA_INIT)
    l = _sublane_all(l, jnp.add)
    o3 = acc_s[...].reshape(LANES // SUBLANES, SUBLANES, n2) / l[None]
    d3 = o3[:, :, :tq] - lam * o3[:, :, tq:]
    ms = _sublane_all(jnp.sum(d3 * d3, axis=0), jnp.add) * (1.0 / LANES)
    d3 = d3 * lax.rsqrt(ms + SUBLN_EPS)[None]
    o = d3.reshape(LANES, tq).T
    o_ref[...] = (o * gain_ref[...] * (1.0 - LAMBDA_INIT)).astype(o_ref.dtype)


def diff_attention(qvt, kk, lq1, lk1, lq2, lk2, subln_gain, tq=1024, tk=1024):
    b, h2, _, s = qvt.shape
    h = h2 // 2
    tq = min(tq, s)
    tk = min(tk, tq)
    vec = lambda a: a.reshape(1, -1).astype(F32)
    small = lambda n: pl.BlockSpec((1, n), lambda bi, hi, i: (0, 0))
    return pl.pallas_call(
        functools.partial(_attn_kernel, tq=tq, tk=tk),
        grid=(b, h, s // tq),
        in_specs=[pl.BlockSpec((None, None, LANES, tq), lambda bi, hi, i: (bi, hi, 0, i)),
                  pl.BlockSpec((None, s, LANES), lambda bi, hi, i: (bi, 0, hi)),
                  pl.BlockSpec((None, None, LANES, s), lambda bi, hi, i: (bi, h + hi, 0, 0)),
                  small(ATT_HEAD_DIM), small(ATT_HEAD_DIM), small(ATT_HEAD_DIM), small(ATT_HEAD_DIM),
                  small(LANES)],
        out_specs=pl.BlockSpec((None, tq, LANES), lambda bi, hi, i: (bi, i, hi)),
        out_shape=jax.ShapeDtypeStruct((b, s, h * LANES), BF16),
        scratch_shapes=[pltpu.VMEM((LANES, 2 * tq), F32)],
        compiler_params=_params("parallel", "parallel", "arbitrary"),
        name="diff_attention",
    )(qvt, kk, qvt, vec(lq1), vec(lk1), vec(lq2), vec(lk2), vec(subln_gain))


def _bdot(a, b):
    return jnp.dot(a.astype(BF16), b.astype(BF16), preferred_element_type=F32)


def _dot3(a, b):
    a_hi = a.astype(BF16)
    a_lo = (a - a_hi.astype(F32)).astype(BF16)
    b_hi = b.astype(BF16)
    b_lo = (b - b_hi.astype(F32)).astype(BF16)
    d = functools.partial(jnp.dot, preferred_element_type=F32)
    return d(a_hi, b_hi) + (d(a_hi, b_lo) + d(a_lo, b_hi))


def _each(f, *lists):
    return [f(*xs) for xs in zip(*lists)]


def _rwkv_pairs(pairs, states, c):
    n = RWKV_HEAD_DIM
    c2 = 2 * c
    row = lax.broadcasted_iota(jnp.int32, (c2, c2), 0)
    col = lax.broadcasted_iota(jnp.int32, (c2, c2), 1)
    lower = row >= col
    strict = row > col
    eye2 = (row == col).astype(F32)
    rc = lax.broadcasted_iota(jnp.int32, (c, c), 0)
    cc = lax.broadcasted_iota(jnp.int32, (c, c), 1)
    lower_b = (rc >= cc).astype(BF16)
    li = lax.broadcasted_iota(jnp.int32, (LANES, LANES), 0)
    lj = lax.broadcasted_iota(jnp.int32, (LANES, LANES), 1)
    same_head = ((li // n) == (lj // n)).astype(BF16)
    eye_l = (li == lj).astype(F32)
    head1 = lax.broadcasted_iota(jnp.int32, (c, LANES), 1) >= n

    def stack(x):
        return jnp.concatenate([jnp.where(head1, 0.0, x), jnp.where(head1, x, 0.0)], axis=0)

    def head_sum(x):
        return jnp.dot(x.astype(BF16), same_head, preferred_element_type=F32)

    r, k, v, lw, ai, g, kk_w, ka_w, rk_w, lnw, lnb = [list(x) for x in zip(*pairs)]

    kk = _each(lambda k, w: k * w, k, kk_w)
    kk = _each(lambda kk: kk * lax.rsqrt(jnp.maximum(head_sum(kk * kk), 1e-24)), kk)
    k2 = _each(lambda k, ai, w: k * (1.0 + (ai - 1.0) * w), k, ai, ka_w)
    b = _each(lambda kk, ai: kk * ai, kk, ai)
    bonus = _each(lambda r, k2, w, v: head_sum(r * k2 * w) * v, r, k2, rk_w, v)

    def cumsum(lw):
        hi = lw.astype(BF16)
        r1 = lw - hi.astype(F32)
        mid = r1.astype(BF16)
        lo = (r1 - mid.astype(F32)).astype(BF16)
        c3 = jnp.dot(lower_b, jnp.concatenate([hi, mid, lo], axis=1), preferred_element_type=F32)
        return c3[:, :LANES] + c3[:, LANES:2 * LANES] + c3[:, 2 * LANES:]

    cum = _each(cumsum, lw)
    e_cum = _each(jnp.exp, cum)
    e_neg = _each(lambda cm: jnp.exp(-cm), cum)
    e_tail = _each(lambda cm: jnp.exp(cm[c - 1:c, :] - cm), cum)
    at = _each(lambda kk, cm, l: stack(-kk * jnp.exp(cm - l)), kk, cum, lw)
    rt = _each(lambda r, e: stack(r * e), r, e_cum)
    vs = _each(stack, v)
    qq = _each(lambda x, y: jnp.concatenate([x, y], axis=0).astype(BF16), at, rt)
    kq = _each(lambda b, k2, e: jnp.concatenate([stack(b * e), stack(k2 * e)], axis=0).astype(BF16),
               b, k2, e_neg)
    s = _each(_dot_t, qq, kq)
    a_ab = _each(lambda s: jnp.where(strict, s[:c2, :c2], 0.0), s)
    a_ak = _each(lambda s: jnp.where(strict, s[:c2, c2:], 0.0), s)
    a_rb = _each(lambda s: jnp.where(lower, s[c2:, :c2], 0.0), s)
    a_rk = _each(lambda s: jnp.where(lower, s[c2:, c2:], 0.0), s)

    def level(m):
        return ((row // (2 * m)) == (col // (2 * m))) & ((row // m) % 2 == 1) & ((col // m) % 2 == 0)

    t = _each(lambda x: eye2 + jnp.where(level(1), x, 0.0), a_ab)
    m = 2
    while m < c:
        sel = level(m)
        tn = _each(lambda t, x: _bdot(t, jnp.where(sel, x, 0.0)), t, a_ab)
        t = _each(lambda t, x: t + _bdot(x, t), t, tn)
        m *= 2

    x1 = _each(_bdot, a_ak, vs)
    wu = _each(lambda t, at, x1: _bdot(t, jnp.concatenate([at, x1], axis=1)), t, at, x1)
    bh_t = _each(lambda b, e: stack(b * e).T, b, e_tail)
    kh_t = _each(lambda k2, e: stack(k2 * e).T, k2, e_tail)
    z1 = _each(lambda a, bt, wu: _bdot(jnp.concatenate([a, bt], axis=0), wu), a_rb, bh_t, wu)
    z2 = _each(lambda a, kt, v: _bdot(jnp.concatenate([a, kt], axis=0), v), a_rk, kh_t, vs)
    pm = _each(lambda z1, rt, e: jnp.concatenate([rt + z1[:c2, :LANES],
                                                  eye_l * e[c - 1:c, :] + z1[c2:, :LANES]], axis=0),
               z1, rt, e_cum)
    yn0 = _each(lambda z1, z2: z1[:, LANES:] + z2, z1, z2)
    chunks = len(pairs) // len(states)
    h_next = list(states)
    yh = [None] * len(pairs)
    for j in range(chunks):
        for blk in range(len(states)):
            i = blk * chunks + j
            yh[i] = _bdot(pm[i], h_next[blk]) + yn0[i]
            h_next[blk] = yh[i][c2:]

    y = _each(lambda yh: yh[:c] + yh[c:c2], yh)
    d = _each(lambda y: y - head_sum(y) * (1.0 / n), y)
    var = _each(lambda d: head_sum(d * d) * (1.0 / n), d)
    outs = _each(lambda d, var, lnw, lnb, bonus, g: (d * lax.rsqrt(var + GN_EPS) * lnw + lnb + bonus) * g,
                 d, var, lnw, lnb, bonus, g)
    return outs, h_next


def _rwkv_kernel(rkv_ref, lora_ref, mu_rkv_ref, mu_lora_ref, w0_ref, wl_ref, a0_ref, al_ref, gl_ref,
                 kk_ref, ka_ref, rk_ref, lnw_ref, lnb_ref, o_ref,
                 state_s, prev_rkv_s, prev_lora_s, r_s, k_s, v_s, lw_s, ai_s, g_s, *, c):
    ci = pl.program_id(1)
    width = RWKV_HEADS * RWKV_HEAD_DIM

    @pl.when(ci == 0)
    def _():
        state_s[...] = jnp.zeros(state_s.shape, F32)
        prev_rkv_s[...] = jnp.zeros(prev_rkv_s.shape, F32)
        prev_lora_s[...] = jnp.zeros(prev_lora_s.shape, F32)

    def shift_mix(p, prev_s, mu):
        rolled = pltpu.roll(p, 1, 0)
        first = lax.broadcasted_iota(jnp.int32, p.shape, 0) == 0
        shifted = jnp.where(first, prev_s[7:8, :], rolled)
        prev_s[...] = p[p.shape[0] - 8:, :]
        return p + mu * (shifted - p)

    z = shift_mix(rkv_ref[...], prev_rkv_s, mu_rkv_ref[...])
    zl = shift_mix(lora_ref[...], prev_lora_s, mu_lora_ref[...])
    r_s[...] = z[:, :width]
    k_s[...] = z[:, width:2 * width]
    v_s[...] = z[:, 2 * width:]
    xw = zl[:, :DECAY_LORA]
    xa = zl[:, DECAY_LORA:DECAY_LORA + ICLR_LORA]
    xg = zl[:, DECAY_LORA + ICLR_LORA:DECAY_LORA + ICLR_LORA + GATE_LORA]
    w_logit = -jax.nn.softplus(-(w0_ref[...] + _dot3(jnp.tanh(xw), wl_ref[...]))) - 0.5
    lw_s[...] = -jnp.exp(w_logit)
    ai_s[...] = jax.nn.sigmoid(a0_ref[...] + _dot3(xa, al_ref[...]))
    g_s[...] = _dot3(jax.nn.sigmoid(xg), gl_ref[...])

    def group_body(gi, carry):
        slices = [pl.ds(pl.multiple_of((gi * RWKV_GROUP_BLOCKS + blk) * LANES, LANES), LANES)
                  for blk in range(RWKV_GROUP_BLOCKS)]
        chunk_rows = [pl.ds(j * c, c) for j in range(r_s.shape[0] // c)]
        pairs = [tuple(ref[rows, sl] for ref in (r_s, k_s, v_s, lw_s, ai_s, g_s))
                 + tuple(ref[:, sl] for ref in (kk_ref, ka_ref, rk_ref, lnw_ref, lnb_ref))
                 for sl in slices for rows in chunk_rows]
        states = [state_s[gi * RWKV_GROUP_BLOCKS + blk] for blk in range(RWKV_GROUP_BLOCKS)]
        outs, h_next = _rwkv_pairs(pairs, states, c)
        for blk, sl in enumerate(slices):
            for j, rows in enumerate(chunk_rows):
                o_ref[rows, sl] = outs[blk * len(chunk_rows) + j].astype(o_ref.dtype)
            state_s[gi * RWKV_GROUP_BLOCKS + blk] = h_next[blk]
        return carry

    lax.fori_loop(0, width // (LANES * RWKV_GROUP_BLOCKS), group_body, 0)


def rwkv7(rkv, lora, mu_rkv, mu_lora, w0, wl, a0, al, gl, k_k, k_a, r_k, lnx_w, lnx_b):
    b, s, _ = rkv.shape
    c = min(RWKV_CHUNK, s)
    rows = min(RWKV_STEP_CHUNKS * c, s)
    width = RWKV_HEADS * RWKV_HEAD_DIM
    row = lambda a: a.reshape(1, -1).astype(F32)
    const = lambda shape: pl.BlockSpec(shape, lambda bi, ci: (0,) * len(shape))
    return pl.pallas_call(
        functools.partial(_rwkv_kernel, c=c),
        grid=(b, s // rows),
        in_specs=[pl.BlockSpec((None, rows, 3 * width), lambda bi, ci: (bi, ci, 0)),
                  pl.BlockSpec((None, rows, LORA_PAD), lambda bi, ci: (bi, ci, 0)),
                  const((1, 3 * width)), const((1, LORA_PAD)),
                  const((1, width)), const((DECAY_LORA, width)),
                  const((1, width)), const((ICLR_LORA, width)), const((GATE_LORA, width)),
                  const((1, width)), const((1, width)), const((1, width)),
                  const((1, width)), const((1, width))],
        out_specs=pl.BlockSpec((None, rows, width), lambda bi, ci: (bi, ci, 0)),
        out_shape=jax.ShapeDtypeStruct((b, s, width), BF16),
        scratch_shapes=[pltpu.VMEM((width // LANES, LANES, LANES), F32),
                        pltpu.VMEM((8, 3 * width), F32), pltpu.VMEM((8, LORA_PAD), F32)]
                       + [pltpu.VMEM((rows, width), F32)] * 6,
        compiler_params=_params("parallel", "arbitrary"),
        name="rwkv7",
    )(rkv, lora, row(mu_rkv), row(mu_lora), row(w0), wl.astype(F32), row(a0), al.astype(F32),
      gl.astype(F32), row(k_k), row(k_a), row(r_k), row(lnx_w), row(lnx_b))


def _merge_kernel(ya_ref, yr_ref, g0_ref, g1_ref, wa_ref, wr_ref, o_ref):
    za = _dot(ya_ref[...], wa_ref[...])
    zr = _dot(yr_ref[...], wr_ref[...])
    ga = jax.nn.sigmoid(g0_ref[...].astype(F32))
    gr = jax.nn.sigmoid(g1_ref[...].astype(F32))
    o_ref[...] = (ga * za + gr * zr).astype(o_ref.dtype)


def gated_merge(y_att, y_rwkv, vg, gate_col0, wa, wr, tm=1024, tn=512):
    b, s, kdim = y_att.shape
    n = wa.shape[1]
    tm = min(tm, s)
    nb = n // tn
    return pl.pallas_call(
        _merge_kernel,
        grid=(b, s // tm, nb),
        in_specs=[pl.BlockSpec((None, tm, kdim), lambda bi, i, j: (bi, i, 0)),
                  pl.BlockSpec((None, tm, kdim), lambda bi, i, j: (bi, i, 0)),
                  pl.BlockSpec((None, tm, tn), lambda bi, i, j: (bi, i, gate_col0 + j)),
                  pl.BlockSpec((None, tm, tn), lambda bi, i, j: (bi, i, gate_col0 + nb + j)),
                  pl.BlockSpec((kdim, tn), lambda bi, i, j: (0, j)),
                  pl.BlockSpec((kdim, tn), lambda bi, i, j: (0, j))],
        out_specs=pl.BlockSpec((None, tm, tn), lambda bi, i, j: (bi, i, j)),
        out_shape=jax.ShapeDtypeStruct((b, s, n), BF16),
        compiler_params=_params("parallel", "parallel", "arbitrary"),
        name="gated_merge",
    )(y_att, y_rwkv, vg, vg, wa, wr)


def _out_proj_kernel(m_ref, w_ref, x_ref, mod_ref, gain_ref, x1_ref, h2_ref):
    x1 = x_ref[...] + mod_ref[2:3, :] * _dot(m_ref[...], w_ref[...])
    x1_ref[...] = x1
    xn = x1 * lax.rsqrt(jnp.mean(x1 * x1, axis=-1, keepdims=True) + NORM_EPS)
    h2_ref[...] = (xn * gain_ref[...] * (1.0 + mod_ref[4:5, :]) + mod_ref[3:4, :]).astype(h2_ref.dtype)


def out_proj(merged, w_out, x, mod, gain2, tm=256):
    b, s, d = x.shape
    tm = min(tm, s)
    return pl.pallas_call(
        _out_proj_kernel,
        grid=(b, s // tm),
        in_specs=[pl.BlockSpec((None, tm, d), lambda bi, i: (bi, i, 0)),
                  pl.BlockSpec((d, d), lambda bi, i: (0, 0)),
                  pl.BlockSpec((None, tm, d), lambda bi, i: (bi, i, 0)),
                  pl.BlockSpec((None, 8, d), lambda bi, i: (bi, 0, 0)),
                  pl.BlockSpec((1, d), lambda bi, i: (0, 0))],
        out_specs=[pl.BlockSpec((None, tm, d), lambda bi, i: (bi, i, 0)),
                   pl.BlockSpec((None, tm, d), lambda bi, i: (bi, i, 0))],
        out_shape=[jax.ShapeDtypeStruct((b, s, d), F32), jax.ShapeDtypeStruct((b, s, d), BF16)],
        compiler_params=_params("parallel", "parallel"),
        name="out_proj",
    )(merged, w_out, x, mod, gain2.reshape(1, d))


def _glu_down_kernel(ug_ref, halo_ref, uv_ref, cw_ref, cb_ref, wd_ref, x1_ref, mod_ref, gain_ref, o_ref,
                     acc_s, *, tm):
    i = pl.program_id(1)
    f = pl.program_id(2)

    @pl.when(f == 0)
    def _():
        acc_s[...] = jnp.zeros(acc_s.shape, F32)

    ug = ug_ref[...].astype(F32)
    halo = halo_ref[...].astype(F32) * (i > 0).astype(F32)
    row8 = lax.broadcasted_iota(jnp.int32, halo.shape, 0)

    def shifted(k):
        rolled = pltpu.roll(ug, k, 0)
        top = jnp.where(row8 < k, pltpu.roll(halo, k, 0), rolled[:8])
        return jnp.concatenate([top, rolled[8:]], axis=0)

    cw = cw_ref[...]
    conv = cb_ref[...] + shifted(2) * cw[0:1, :] + shifted(1) * cw[1:2, :] + ug * cw[2:3, :]
    gelu = 0.5 * conv * (1.0 + lax.erf(conv * (2.0 ** -0.5)))
    act = (gelu * uv_ref[...].astype(F32)).astype(BF16)
    acc_s[...] += _dot(act, wd_ref[...])

    @pl.when(f == pl.num_programs(2) - 1)
    def _():
        x2 = x1_ref[...] + mod_ref[5:6, :] * acc_s[...]
        o_ref[...] = x2 * lax.rsqrt(jnp.mean(x2 * x2, axis=-1, keepdims=True) + NORM_EPS) * gain_ref[...]


def glu_down(u, conv_w, conv_b, w_down, x1, mod, final_gain, tm=512, tf=1408):
    b, s, d = x1.shape
    ff = w_down.shape[0]
    tm = min(tm, s)
    nf = ff // tf
    cw8 = jnp.zeros((8, ff), F32).at[:conv_w.shape[0]].set(conv_w)
    hb = tm // 8
    return pl.pallas_call(
        functools.partial(_glu_down_kernel, tm=tm),
        grid=(b, s // tm, nf),
        in_specs=[pl.BlockSpec((None, tm, tf), lambda bi, i, f: (bi, i, f)),
                  pl.BlockSpec((None, 8, tf), lambda bi, i, f: (bi, jnp.maximum(i * hb - 1, 0), f)),
                  pl.BlockSpec((None, tm, tf), lambda bi, i, f: (bi, i, nf + f)),
                  pl.BlockSpec((8, tf), lambda bi, i, f: (0, f)),
                  pl.BlockSpec((1, tf), lambda bi, i, f: (0, f)),
                  pl.BlockSpec((tf, d), lambda bi, i, f: (f, 0)),
                  pl.BlockSpec((None, tm, d), lambda bi, i, f: (bi, i, 0)),
                  pl.BlockSpec((None, 8, d), lambda bi, i, f: (bi, 0, 0)),
                  pl.BlockSpec((1, d), lambda bi, i, f: (0, 0))],
        out_specs=pl.BlockSpec((None, tm, d), lambda bi, i, f: (bi, i, 0)),
        out_shape=jax.ShapeDtypeStruct((b, s, d), F32),
        scratch_shapes=[pltpu.VMEM((tm, d), F32)],
        compiler_params=_params("parallel", "parallel", "arbitrary"),
        name="glu_down",
    )(u, u, u, cw8, conv_b.reshape(1, ff), w_down, x1, mod, final_gain.reshape(1, d))


def kernel(x, c, w_ada, b_ada, norm1_gain, w_in, lambda_q1, lambda_k1, lambda_q2, lambda_k2, subln_gain, mu_shift, w0, w_lora_up, a0, a_lora_up, g_lora_up, k_k, k_a, r_k, lnx_w, lnx_b, w_branch, w_out, norm2_gain, w_up, conv_w, conv_b, w_down, final_gain):
    b, s, d = x.shape
    depth = w_in.shape[0]
    assert depth == 1
    l = 0
    qk_cols = 2 * ATT_HEADS * 2 * ATT_HEAD_DIM
    v_cols = ATT_HEADS * 2 * ATT_HEAD_DIM
    width = RWKV_HEADS * RWKV_HEAD_DIM
    n_lora = DECAY_LORA + ICLR_LORA + GATE_LORA
    rkv0 = qk_cols + v_cols
    lora0 = rkv0 + 3 * width
    gate0 = lora0 + n_lora

    c8 = jnp.zeros((8, d), F32).at[:b].set(c)
    mod = ada_modulation(c8, w_ada[l], b_ada[l])[:b]
    mod = jnp.pad(mod.reshape(b, 6, d), ((0, 0), (0, 2), (0, 0)))

    h = norm_mod(x, mod, norm1_gain[l], shift_row=0, scale_row=1)

    w_in_l = w_in[l]
    q_cols = qk_cols // 2
    w_qv_t = jnp.concatenate([w_in_l[:, :q_cols], w_in_l[:, qk_cols:rkv0]], axis=1).T.astype(BF16)
    w_k = w_in_l[:, q_cols:qk_cols].astype(BF16)
    w_gate = w_in_l[:, gate0:].astype(BF16)
    w_rkv = w_in_l[:, rkv0:lora0].astype(BF16)
    w_lora = jnp.pad(w_in_l[:, lora0:gate0], ((0, 0), (0, LORA_PAD - n_lora))).astype(BF16)

    (cos, sina, sinb), rope_rows = rope_tables(s)
    qv_t = matmul_t(h, w_qv_t, *rope_rows)
    kk = matmul(h, w_k, BF16, rope=(cos, sina, sinb, 0))
    gates = matmul(h, w_gate, BF16)
    rkv = matmul(h, w_rkv, F32)
    lora = matmul(h, w_lora, F32, tn=LORA_PAD)

    y_att = diff_attention(qv_t.reshape(b, 2 * ATT_HEADS, LANES, s), kk,
                           lambda_q1[l], lambda_k1[l], lambda_q2[l], lambda_k2[l], subln_gain[l])

    mu = mu_shift[l]
    mu_lora = jnp.pad(mu[3 * width:], (0, LORA_PAD - n_lora))
    y_rwkv = rwkv7(rkv, lora, mu[:3 * width], mu_lora, w0[l], w_lora_up[l], a0[l], a_lora_up[l],
                   g_lora_up[l], k_k[l], k_a[l], r_k[l].reshape(-1), lnx_w[l], lnx_b[l])

    tn_merge = 512
    merged = gated_merge(y_att, y_rwkv, gates, 0, w_branch[l, 0].astype(BF16),
                         w_branch[l, 1].astype(BF16), tn=tn_merge)
    x1, h2 = out_proj(merged, w_out[l].astype(BF16), x, mod, norm2_gain[l])

    u = matmul(h2, w_up[l].astype(BF16), BF16)
    return glu_down(u, conv_w[l], conv_b[l], w_down[l].astype(BF16), x1, mod, final_gain)
```

```python
import functools
import math

import jax
import jax.numpy as jnp
from jax import lax
from jax.experimental import pallas as pl
from jax.experimental.pallas import tpu as pltpu

F32 = jnp.float32
BF16 = jnp.bfloat16
HIGHEST = lax.Precision.HIGHEST

ATT_HEADS = 8
ATT_HEAD_DIM = 64
ROT_DIM = ATT_HEAD_DIM // 4
ROPE_THETA = 500000.0
SUBLN_EPS = 1e-5
RWKV_HEADS = 16
RWKV_HEAD_DIM = 64
DECAY_LORA = 64
ICLR_LORA = 64
GATE_LORA = 160
GN_EPS = 64e-5
NORM_EPS = 1e-6
LAMBDA_INIT = 0.8 - 0.6 * math.exp(-0.3 * 0)

LANES = 128
VMEM_LIMIT = 48 * 1024 * 1024
RWKV_CHUNK = 64
RWKV_STEP_CHUNKS = 2
RWKV_GROUP_BLOCKS = 8
LORA_PAD = 384


def _params(*sem):
    return pltpu.CompilerParams(dimension_semantics=sem, vmem_limit_bytes=VMEM_LIMIT)


def _dot(a, b, precision=None):
    return jnp.dot(a, b, preferred_element_type=F32, precision=precision)


def _dot_t(a, b, precision=None):
    return lax.dot_general(a, b, (((1,), (1,)), ((), ())),
                           preferred_element_type=F32, precision=precision)


def _mod_kernel(c_ref, w_ref, b_ref, o_ref):
    c = c_ref[...]
    s = c * jax.nn.sigmoid(c)
    o_ref[...] = _dot(s, w_ref[...], HIGHEST) + b_ref[...]


def ada_modulation(c8, w_ada, b_ada, tn=1024):
    m, d = c8.shape
    n = w_ada.shape[1]
    return pl.pallas_call(
        _mod_kernel,
        grid=(n // tn,),
        in_specs=[pl.BlockSpec((m, d), lambda j: (0, 0)),
                  pl.BlockSpec((d, tn), lambda j: (0, j)),
                  pl.BlockSpec((1, tn), lambda j: (0, j))],
        out_specs=pl.BlockSpec((m, tn), lambda j: (0, j)),
        out_shape=jax.ShapeDtypeStruct((m, n), F32),
        compiler_params=_params("arbitrary"),
        name="ada_mod",
    )(c8, w_ada, b_ada.reshape(1, n))


def _norm_mod_kernel(x_ref, mod_ref, gain_ref, o_ref, *, shift_row, scale_row):
    x = x_ref[...]
    xn = x * lax.rsqrt(jnp.mean(x * x, axis=-1, keepdims=True) + NORM_EPS)
    h = xn * gain_ref[...] * (1.0 + mod_ref[scale_row:scale_row + 1, :]) + mod_ref[shift_row:shift_row + 1, :]
    o_ref[...] = h.astype(o_ref.dtype)


def norm_mod(x, mod, gain, shift_row, scale_row, tm=1024):
    b, s, d = x.shape
    tm = min(tm, s)
    return pl.pallas_call(
        functools.partial(_norm_mod_kernel, shift_row=shift_row, scale_row=scale_row),
        grid=(b, s // tm),
        in_specs=[pl.BlockSpec((None, tm, d), lambda bi, i: (bi, i, 0)),
                  pl.BlockSpec((None, 8, d), lambda bi, i: (bi, 0, 0)),
                  pl.BlockSpec((1, d), lambda bi, i: (0, 0))],
        out_specs=pl.BlockSpec((None, tm, d), lambda bi, i: (bi, i, 0)),
        out_shape=jax.ShapeDtypeStruct((b, s, d), BF16),
        compiler_params=_params("parallel", "parallel"),
        name="norm_mod",
    )(x, mod, gain.reshape(1, d))


def _mm_kernel(a_ref, w_ref, o_ref):
    o_ref[...] = _dot(a_ref[...], w_ref[...]).astype(o_ref.dtype)


def _mm_rope_kernel(a_ref, w_ref, cos_ref, sina_ref, sinb_ref, o_ref, *, tn, q_cols):
    acc = _dot(a_ref[...], w_ref[...])
    scale = jnp.where(pl.program_id(2) * tn < q_cols, ATT_HEAD_DIM ** -0.5 * math.log2(math.e), 1.0).astype(F32)
    cos, sina, sinb = cos_ref[...], sina_ref[...], sinb_ref[...]
    for j in range(tn // LANES):
        t = acc[:, j * LANES:(j + 1) * LANES]
        half = ROT_DIM // 2
        r = t * cos + pltpu.roll(t, LANES - half, 1) * sina + pltpu.roll(t, half, 1) * sinb
        o_ref[:, j * LANES:(j + 1) * LANES] = (r * scale).astype(o_ref.dtype)


def matmul(a, w, out_dtype, tm=1024, tn=512, rope=None):
    b, s, k = a.shape
    n = w.shape[1]
    tm = min(tm, s)
    tn = min(tn, n)
    assert s % tm == 0 and n % tn == 0
    in_specs = [pl.BlockSpec((None, tm, k), lambda bi, i, j: (bi, i, 0)),
                pl.BlockSpec((k, tn), lambda bi, i, j: (0, j))]
    args = [a, w]
    if rope is None:
        body = _mm_kernel
        name = "matmul"
    else:
        cos, sina, sinb, q_cols = rope
        body = functools.partial(_mm_rope_kernel, tn=tn, q_cols=q_cols)
        in_specs += [pl.BlockSpec((tm, LANES), lambda bi, i, j: (i, 0))] * 3
        args += [cos, sina, sinb]
        name = "matmul_rope"
    return pl.pallas_call(
        body,
        grid=(b, s // tm, n // tn),
        in_specs=in_specs,
        out_specs=pl.BlockSpec((None, tm, tn), lambda bi, i, j: (bi, i, j)),
        out_shape=jax.ShapeDtypeStruct((b, s, n), out_dtype),
        compiler_params=_params("parallel", "parallel", "arbitrary"),
        name=name,
    )(*args)


def _mm_t_kernel(wt_ref, a_ref, cos_ref, sina_ref, sinb_ref, o_ref):
    acc = _dot_t(wt_ref[...], a_ref[...])
    tn, tm = acc.shape
    is_q = pl.program_id(2) == 0
    cos = jnp.where(is_q, cos_ref[...], 1.0)
    sina = jnp.where(is_q, sina_ref[...], 0.0)
    sinb = jnp.where(is_q, sinb_ref[...], 0.0)
    scale = jnp.where(is_q, ATT_HEAD_DIM ** -0.5 * math.log2(math.e), 1.0).astype(F32)
    half = ROT_DIM // 2
    per_map = lambda x: x.reshape(tn // ATT_HEAD_DIM, ATT_HEAD_DIM, tm)
    r = (per_map(acc) * cos[None] + per_map(pltpu.roll(acc, tn - half, 0)) * sina[None]
         + per_map(pltpu.roll(acc, half, 0)) * sinb[None])
    o_ref[...] = (r.reshape(tn, tm) * scale).astype(o_ref.dtype)


def matmul_t(a, wt, cos_t, sina_t, sinb_t, tm=1024, tn=1024):
    b, s, k = a.shape
    n = wt.shape[0]
    tm = min(tm, s)
    table = pl.BlockSpec((ATT_HEAD_DIM, tm), lambda bi, i, j: (0, i))
    return pl.pallas_call(
        _mm_t_kernel,
        grid=(b, s // tm, n // tn),
        in_specs=[pl.BlockSpec((tn, k), lambda bi, i, j: (j, 0)),
                  pl.BlockSpec((None, tm, k), lambda bi, i, j: (bi, i, 0)),
                  table, table, table],
        out_specs=pl.BlockSpec((None, tn, tm), lambda bi, i, j: (bi, j, i)),
        out_shape=jax.ShapeDtypeStruct((b, n, s), BF16),
        compiler_params=_params("parallel", "parallel", "arbitrary"),
        name="matmul_t",
    )(wt, a, cos_t, sina_t, sinb_t)


def rope_tables(seq):
    half = ROT_DIM // 2
    pos = jnp.arange(seq, dtype=F32)
    inv = ROPE_THETA ** (-jnp.arange(0, ROT_DIM, 2, dtype=F32) / ROT_DIM)
    ang = pos[:, None] * inv[None, :]
    cos, sin = jnp.cos(ang), jnp.sin(ang)
    ones = jnp.ones((seq, ATT_HEAD_DIM - ROT_DIM), F32)
    zeros = jnp.zeros((seq, ATT_HEAD_DIM - ROT_DIM), F32)
    zh = jnp.zeros((seq, half), F32)
    cos64 = jnp.concatenate([cos, cos, ones], axis=1)
    sina64 = jnp.concatenate([-sin, zh, zeros], axis=1)
    sinb64 = jnp.concatenate([zh, sin, zeros], axis=1)
    rep = LANES // ATT_HEAD_DIM
    by_lane = (jnp.tile(cos64, (1, rep)), jnp.tile(sina64, (1, rep)), jnp.tile(sinb64, (1, rep)))
    by_row = (cos64.T, sina64.T, sinb64.T)
    return by_lane, by_row


SUBLANES = 8


def _sublane_all(x, op):
    for shift in (4, 2, 1):
        x = op(x, pltpu.roll(x, shift, 0))
    return x


def _attn_kernel(qt_ref, k_ref, vt_ref, lq1_ref, lk1_ref, lq2_ref, lk2_ref, gain_ref, o_ref,
                 acc_s, *, tq, tk):
    qi = pl.program_id(2)
    qt = qt_ref[...]
    row = lax.broadcasted_iota(jnp.int32, qt.shape, 0)
    zero = jnp.zeros_like(qt)
    q2 = jnp.concatenate([jnp.where(row < ATT_HEAD_DIM, qt, zero),
                          jnp.where(row >= ATT_HEAD_DIM, qt, zero)], axis=1)
    n2 = 2 * tq
    acc_s[...] = jnp.zeros(acc_s.shape, F32)

    def update(k0, nk, c0, nc, m_all, l_all, masked):
        cols = slice(c0, c0 + nc)
        kb = k_ref[pl.ds(k0, nk), :]
        vtb = vt_ref[:, pl.ds(k0, nk)]
        s = _dot(kb, q2[:, cols])
        if masked:
            kpos = k0 + lax.broadcasted_iota(jnp.int32, (nk, nc), 0)
            col = c0 + lax.broadcasted_iota(jnp.int32, (nk, nc), 1)
            qpos = qi * tq + jnp.where(col >= tq, col - tq, col)
            s = jnp.where(kpos <= qpos, s, -jnp.inf)
        s3 = s.reshape(nk // SUBLANES, SUBLANES, nc)
        m_old = m_all[:, cols]
        m_new = jnp.maximum(m_old, _sublane_all(jnp.max(s3, axis=0), jnp.maximum))
        p3 = jnp.exp2(s3 - m_new[None])
        corr = jnp.exp2(m_old - m_new)
        l_new = l_all[:, cols] * corr + jnp.sum(p3, axis=0)
        pv = _dot(vtb, p3.reshape(nk, nc).astype(BF16))
        acc3 = acc_s[:, cols].reshape(LANES // SUBLANES, SUBLANES, nc)
        acc_s[:, cols] = (acc3 * corr[None]).reshape(LANES, nc) + pv
        if nc == n2:
            return m_new, l_new
        put = lambda full, part: jnp.concatenate(
            ([full[:, :c0]] if c0 else []) + [part] + ([full[:, c0 + nc:]] if c0 + nc < n2 else []), axis=1)
        return put(m_all, m_new), put(l_all, l_new)

    m0 = jnp.full((SUBLANES, n2), -jnp.inf, F32)
    l0 = jnp.zeros((SUBLANES, n2), F32)
    m, l = lax.fori_loop(0, (qi * tq) // tk,
                         lambda j, c: update(pl.multiple_of(j * tk, tk), tk, 0, n2, c[0], c[1], False), (m0, l0))
    half = tq // 2
    kd = pl.multiple_of(qi * tq, tq)
    m, l = update(kd, half, 0, n2, m, l, True)
    m, l = update(kd + half, half, half, half, m, l, True)
    m, l = update(kd + half, half, tq + half, half, m, l, True)

    lam = (jnp.exp(jnp.sum(lq1_ref[...] * lk1_ref[...], axis=-1, keepdims=True))
           - jnp.exp(jnp.sum(lq2_ref[...] * lk2_ref[...], axis=-1, keepdims=True))
           + LAMBDA_INIT)
    l = _sublane_all(l, jnp.add)
    o3 = acc_s[...].reshape(LANES // SUBLANES, SUBLANES, n2) / l[None]
    d3 = o3[:, :, :tq] - lam * o3[:, :, tq:]
    ms = _sublane_all(jnp.sum(d3 * d3, axis=0), jnp.add) * (1.0 / LANES)
    d3 = d3 * lax.rsqrt(ms + SUBLN_EPS)[None]
    o = d3.reshape(LANES, tq).T
    o_ref[...] = (o * gain_ref[...] * (1.0 - LAMBDA_INIT)).astype(o_ref.dtype)


def diff_attention(qvt, kk, lq1, lk1, lq2, lk2, subln_gain, tq=1024, tk=1024):
    b, h2, _, s = qvt.shape
    h = h2 // 2
    tq = min(tq, s)
    tk = min(tk, tq)
    vec = lambda a: a.reshape(1, -1).astype(F32)
    small = lambda n: pl.BlockSpec((1, n), lambda bi, hi, i: (0, 0))
    return pl.pallas_call(
        functools.partial(_attn_kernel, tq=tq, tk=tk),
        grid=(b, h, s // tq),
        in_specs=[pl.BlockSpec((None, None, LANES, tq), lambda bi, hi, i: (bi, hi, 0, i)),
                  pl.BlockSpec((None, s, LANES), lambda bi, hi, i: (bi, 0, hi)),
                  pl.BlockSpec((None, None, LANES, s), lambda bi, hi, i: (bi, h + hi, 0, 0)),
                  small(ATT_HEAD_DIM), small(ATT_HEAD_DIM), small(ATT_HEAD_DIM), small(ATT_HEAD_DIM),
                  small(LANES)],
        out_specs=pl.BlockSpec((None, tq, LANES), lambda bi, hi, i: (bi, i, hi)),
        out_shape=jax.ShapeDtypeStruct((b, s, h * LANES), BF16),
        scratch_shapes=[pltpu.VMEM((LANES, 2 * tq), F32)],
        compiler_params=_params("parallel", "parallel", "arbitrary"),
        name="diff_attention",
    )(qvt, kk, qvt, vec(lq1), vec(lk1), vec(lq2), vec(lk2), vec(subln_gain))


def _bdot(a, b):
    return jnp.dot(a.astype(BF16), b.astype(BF16), preferred_element_type=F32)


def _dot3(a, b):
    a_hi = a.astype(BF16)
    a_lo = (a - a_hi.astype(F32)).astype(BF16)
    b_hi = b.astype(BF16)
    b_lo = (b - b_hi.astype(F32)).astype(BF16)
    d = functools.partial(jnp.dot, preferred_element_type=F32)
    return d(a_hi, b_hi) + (d(a_hi, b_lo) + d(a_lo, b_hi))


def _each(f, *lists):
    return [f(*xs) for xs in zip(*lists)]


def _rwkv_pairs(pairs, states, c):
    n = RWKV_HEAD_DIM
    c2 = 2 * c
    row = lax.broadcasted_iota(jnp.int32, (c2, c2), 0)
    col = lax.broadcasted_iota(jnp.int32, (c2, c2), 1)
    lower = row >= col
    strict = row > col
    eye2 = (row == col).astype(F32)
    rc = lax.broadcasted_iota(jnp.int32, (c, c), 0)
    cc = lax.broadcasted_iota(jnp.int32, (c, c), 1)
    lower_b = (rc >= cc).astype(BF16)
    li = lax.broadcasted_iota(jnp.int32, (LANES, LANES), 0)
    lj = lax.broadcasted_iota(jnp.int32, (LANES, LANES), 1)
    same_head = ((li // n) == (lj // n)).astype(BF16)
    eye_l = (li == lj).astype(F32)
    head1 = lax.broadcasted_iota(jnp.int32, (c, LANES), 1) >= n

    def stack(x):
        return jnp.concatenate([jnp.where(head1, 0.0, x), jnp.where(head1, x, 0.0)], axis=0)

    def head_sum(x):
        return jnp.dot(x.astype(BF16), same_head, preferred_element_type=F32)

    r, k, v, lw, ai, g, kk_w, ka_w, rk_w, lnw, lnb = [list(x) for x in zip(*pairs)]

    kk = _each(lambda k, w: k * w, k, kk_w)
    kk = _each(lambda kk: kk * lax.rsqrt(jnp.maximum(head_sum(kk * kk), 1e-24)), kk)
    k2 = _each(lambda k, ai, w: k * (1.0 + (ai - 1.0) * w), k, ai, ka_w)
    b = _each(lambda kk, ai: kk * ai, kk, ai)
    bonus = _each(lambda r, k2, w, v: head_sum(r * k2 * w) * v, r, k2, rk_w, v)

    def cumsum(lw):
        hi = lw.astype(BF16)
        r1 = lw - hi.astype(F32)
        mid = r1.astype(BF16)
        lo = (r1 - mid.astype(F32)).astype(BF16)
        c3 = jnp.dot(lower_b, jnp.concatenate([hi, mid, lo], axis=1), preferred_element_type=F32)
        return c3[:, :LANES] + c3[:, LANES:2 * LANES] + c3[:, 2 * LANES:]

    cum = _each(cumsum, lw)
    e_cum = _each(jnp.exp, cum)
    e_neg = _each(lambda cm: jnp.exp(-cm), cum)
    e_tail = _each(lambda cm: jnp.exp(cm[c - 1:c, :] - cm), cum)
    at = _each(lambda kk, cm, l: stack(-kk * jnp.exp(cm - l)), kk, cum, lw)
    rt = _each(lambda r, e: stack(r * e), r, e_cum)
    vs = _each(stack, v)
    qq = _each(lambda x, y: jnp.concatenate([x, y], axis=0).astype(BF16), at, rt)
    kq = _each(lambda b, k2, e: jnp.concatenate([stack(b * e), stack(k2 * e)], axis=0).astype(BF16),
               b, k2, e_neg)
    s = _each(_dot_t, qq, kq)
    a_ab = _each(lambda s: jnp.where(strict, s[:c2, :c2], 0.0), s)
    a_ak = _each(lambda s: jnp.where(strict, s[:c2, c2:], 0.0), s)
    a_rb = _each(lambda s: jnp.where(lower, s[c2:, :c2], 0.0), s)
    a_rk = _each(lambda s: jnp.where(lower, s[c2:, c2:], 0.0), s)

    def level(m):
        return ((row // (2 * m)) == (col // (2 * m))) & ((row // m) % 2 == 1) & ((col // m) % 2 == 0)

    t = _each(lambda x: eye2 + jnp.where(level(1), x, 0.0), a_ab)
    m = 2
    while m < c:
        sel = level(m)
        tn = _each(lambda t, x: _bdot(t, jnp.where(sel, x, 0.0)), t, a_ab)
        t = _each(lambda t, x: t + _bdot(x, t), t, tn)
        m *= 2

    x1 = _each(_bdot, a_ak, vs)
    wu = _each(lambda t, at, x1: _bdot(t, jnp.concatenate([at, x1], axis=1)), t, at, x1)
    bh_t = _each(lambda b, e: stack(b * e).T, b, e_tail)
    kh_t = _each(lambda k2, e: stack(k2 * e).T, k2, e_tail)
    z1 = _each(lambda a, bt, wu: _bdot(jnp.concatenate([a, bt], axis=0), wu), a_rb, bh_t, wu)
    z2 = _each(lambda a, kt, v: _bdot(jnp.concatenate([a, kt], axis=0), v), a_rk, kh_t, vs)
    pm = _each(lambda z1, rt, e: jnp.concatenate([rt + z1[:c2, :LANES],
                                                  eye_l * e[c - 1:c, :] + z1[c2:, :LANES]], axis=0),
               z1, rt, e_cum)
    yn0 = _each(lambda z1, z2: z1[:, LANES:] + z2, z1, z2)
    chunks = len(pairs) // len(states)
    h_next = list(states)
    yh = [None] * len(pairs)
    for j in range(chunks):
        for blk in range(len(states)):
            i = blk * chunks + j
            yh[i] = _bdot(pm[i], h_next[blk]) + yn0[i]
            h_next[blk] = yh[i][c2:]

    y = _each(lambda yh: yh[:c] + yh[c:c2], yh)
    d = _each(lambda y: y - head_sum(y) * (1.0 / n), y)
    var = _each(lambda d: head_sum(d * d) * (1.0 / n), d)
    outs = _each(lambda d, var, lnw, lnb, bonus, g: (d * lax.rsqrt(var + GN_EPS) * lnw + lnb + bonus) * g,
                 d, var, lnw, lnb, bonus, g)
    return outs, h_next


def _rwkv_kernel(rkv_ref, lora_ref, mu_rkv_ref, mu_lora_ref, w0_ref, wl_ref, a0_ref, al_ref, gl_ref,
                 kk_ref, ka_ref, rk_ref, lnw_ref, lnb_ref, o_ref,
                 state_s, prev_rkv_s, prev_lora_s, r_s, k_s, v_s, lw_s, ai_s, g_s, *, c):
    ci = pl.program_id(1)
    width = RWKV_HEADS * RWKV_HEAD_DIM

    @pl.when(ci == 0)
    def _():
        state_s[...] = jnp.zeros(state_s.shape, F32)
        prev_rkv_s[...] = jnp.zeros(prev_rkv_s.shape, F32)
        prev_lora_s[...] = jnp.zeros(prev_lora_s.shape, F32)

    def shift_mix(p, prev_s, mu):
        rolled = pltpu.roll(p, 1, 0)
        first = lax.broadcasted_iota(jnp.int32, p.shape, 0) == 0
        shifted = jnp.where(first, prev_s[7:8, :], rolled)
        prev_s[...] = p[p.shape[0] - 8:, :]
        return p + mu * (shifted - p)

    z = shift_mix(rkv_ref[...], prev_rkv_s, mu_rkv_ref[...])
    zl = shift_mix(lora_ref[...], prev_lora_s, mu_lora_ref[...])
    r_s[...] = z[:, :width]
    k_s[...] = z[:, width:2 * width]
    v_s[...] = z[:, 2 * width:]
    xw = zl[:, :DECAY_LORA]
    xa = zl[:, DECAY_LORA:DECAY_LORA + ICLR_LORA]
    xg = zl[:, DECAY_LORA + ICLR_LORA:DECAY_LORA + ICLR_LORA + GATE_LORA]
    w_logit = -jax.nn.softplus(-(w0_ref[...] + _dot3(jnp.tanh(xw), wl_ref[...]))) - 0.5
    lw_s[...] = -jnp.exp(w_logit)
    ai_s[...] = jax.nn.sigmoid(a0_ref[...] + _dot3(xa, al_ref[...]))
    g_s[...] = _dot3(jax.nn.sigmoid(xg), gl_ref[...])

    def group_body(gi, carry):
        slices = [pl.ds(pl.multiple_of((gi * RWKV_GROUP_BLOCKS + blk) * LANES, LANES), LANES)
                  for blk in range(RWKV_GROUP_BLOCKS)]
        chunk_rows = [pl.ds(j * c, c) for j in range(r_s.shape[0] // c)]
        pairs = [tuple(ref[rows, sl] for ref in (r_s, k_s, v_s, lw_s, ai_s, g_s))
                 + tuple(ref[:, sl] for ref in (kk_ref, ka_ref, rk_ref, lnw_ref, lnb_ref))
                 for sl in slices for rows in chunk_rows]
        states = [state_s[gi * RWKV_GROUP_BLOCKS + blk] for blk in range(RWKV_GROUP_BLOCKS)]
        outs, h_next = _rwkv_pairs(pairs, states, c)
        for blk, sl in enumerate(slices):
            for j, rows in enumerate(chunk_rows):
                o_ref[rows, sl] = outs[blk * len(chunk_rows) + j].astype(o_ref.dtype)
            state_s[gi * RWKV_GROUP_BLOCKS + blk] = h_next[blk]
        return carry

    lax.fori_loop(0, width // (LANES * RWKV_GROUP_BLOCKS), group_body, 0)


def rwkv7(rkv, lora, mu_rkv, mu_lora, w0, wl, a0, al, gl, k_k, k_a, r_k, lnx_w, lnx_b):
    b, s, _ = rkv.shape
    c = min(RWKV_CHUNK, s)
    rows = min(RWKV_STEP_CHUNKS * c, s)
    width = RWKV_HEADS * RWKV_HEAD_DIM
    row = lambda a: a.reshape(1, -1).astype(F32)
    const = lambda shape: pl.BlockSpec(shape, lambda bi, ci: (0,) * len(shape))
    return pl.pallas_call(
        functools.partial(_rwkv_kernel, c=c),
        grid=(b, s // rows),
        in_specs=[pl.BlockSpec((None, rows, 3 * width), lambda bi, ci: (bi, ci, 0)),
                  pl.BlockSpec((None, rows, LORA_PAD), lambda bi, ci: (bi, ci, 0)),
                  const((1, 3 * width)), const((1, LORA_PAD)),
                  const((1, width)), const((DECAY_LORA, width)),
                  const((1, width)), const((ICLR_LORA, width)), const((GATE_LORA, width)),
                  const((1, width)), const((1, width)), const((1, width)),
                  const((1, width)), const((1, width))],
        out_specs=pl.BlockSpec((None, rows, width), lambda bi, ci: (bi, ci, 0)),
        out_shape=jax.ShapeDtypeStruct((b, s, width), BF16),
        scratch_shapes=[pltpu.VMEM((width // LANES, LANES, LANES), F32),
                        pltpu.VMEM((8, 3 * width), F32), pltpu.VMEM((8, LORA_PAD), F32)]
                       + [pltpu.VMEM((rows, width), F32)] * 6,
        compiler_params=_params("parallel", "arbitrary"),
        name="rwkv7",
    )(rkv, lora, row(mu_rkv), row(mu_lora), row(w0), wl.astype(F32), row(a0), al.astype(F32),
      gl.astype(F32), row(k_k), row(k_a), row(r_k), row(lnx_w), row(lnx_b))


MERGE_CHUNK = 256


def _merge_kernel(ya_ref, yr_ref, g0_ref, g1_ref, wa_ref, wr_ref, o_ref):
    ya, yr = ya_ref[...], yr_ref[...]
    for c0 in range(0, o_ref.shape[1], MERGE_CHUNK):
        cs = slice(c0, c0 + MERGE_CHUNK)
        za = _dot(ya, wa_ref[:, cs])
        zr = _dot(yr, wr_ref[:, cs])
        ga = jax.nn.sigmoid(g0_ref[:, cs].astype(F32))
        gr = jax.nn.sigmoid(g1_ref[:, cs].astype(F32))
        o_ref[:, cs] = (ga * za + gr * zr).astype(o_ref.dtype)


def gated_merge(y_att, y_rwkv, gates, wa, wr, tm=1024, tn=1024):
    b, s, kdim = y_att.shape
    n = wa.shape[1]
    tm = min(tm, s)
    nb = n // tn
    return pl.pallas_call(
        _merge_kernel,
        grid=(b, s // tm, nb),
        in_specs=[pl.BlockSpec((None, tm, kdim), lambda bi, i, j: (bi, i, 0)),
                  pl.BlockSpec((None, tm, kdim), lambda bi, i, j: (bi, i, 0)),
                  pl.BlockSpec((None, tm, tn), lambda bi, i, j: (bi, i, j)),
                  pl.BlockSpec((None, tm, tn), lambda bi, i, j: (bi, i, nb + j)),
                  pl.BlockSpec((kdim, tn), lambda bi, i, j: (0, j)),
                  pl.BlockSpec((kdim, tn), lambda bi, i, j: (0, j))],
        out_specs=pl.BlockSpec((None, tm, tn), lambda bi, i, j: (bi, i, j)),
        out_shape=jax.ShapeDtypeStruct((b, s, n), BF16),
        compiler_params=_params("parallel", "parallel", "arbitrary"),
        name="gated_merge",
    )(y_att, y_rwkv, gates, gates, wa, wr)


def _out_proj_kernel(m_ref, w_ref, x_ref, mod_ref, gain_ref, x1_ref, h2_ref):
    x1 = x_ref[...] + mod_ref[2:3, :] * _dot(m_ref[...], w_ref[...])
    x1_ref[...] = x1
    xn = x1 * lax.rsqrt(jnp.mean(x1 * x1, axis=-1, keepdims=True) + NORM_EPS)
    h2_ref[...] = (xn * gain_ref[...] * (1.0 + mod_ref[4:5, :]) + mod_ref[3:4, :]).astype(h2_ref.dtype)


def out_proj(merged, w_out, x, mod, gain2, tm=256):
    b, s, d = x.shape
    tm = min(tm, s)
    return pl.pallas_call(
        _out_proj_kernel,
        grid=(b, s // tm),
        in_specs=[pl.BlockSpec((None, tm, d), lambda bi, i: (bi, i, 0)),
                  pl.BlockSpec((d, d), lambda bi, i: (0, 0)),
                  pl.BlockSpec((None, tm, d), lambda bi, i: (bi, i, 0)),
                  pl.BlockSpec((None, 8, d), lambda bi, i: (bi, 0, 0)),
                  pl.BlockSpec((1, d), lambda bi, i: (0, 0))],
        out_specs=[pl.BlockSpec((None, tm, d), lambda bi, i: (bi, i, 0)),
                   pl.BlockSpec((None, tm, d), lambda bi, i: (bi, i, 0))],
        out_shape=[jax.ShapeDtypeStruct((b, s, d), F32), jax.ShapeDtypeStruct((b, s, d), BF16)],
        compiler_params=_params("parallel", "parallel"),
        name="out_proj",
    )(merged, w_out, x, mod, gain2.reshape(1, d))


def _glu_down_kernel(ug_ref, halo_ref, uv_ref, cw_ref, cb_ref, wd_ref, x1_ref, mod_ref, gain_ref, o_ref,
                     acc_s, *, tm):
    i = pl.program_id(1)
    f = pl.program_id(2)

    @pl.when(f == 0)
    def _():
        acc_s[...] = jnp.zeros(acc_s.shape, F32)

    ug = ug_ref[...].astype(F32)
    halo = halo_ref[...].astype(F32) * (i > 0).astype(F32)
    row8 = lax.broadcasted_iota(jnp.int32, halo.shape, 0)

    def shifted(k):
        rolled = pltpu.roll(ug, k, 0)
        top = jnp.where(row8 < k, pltpu.roll(halo, k, 0), rolled[:8])
        return jnp.concatenate([top, rolled[8:]], axis=0)

    cw = cw_ref[...]
    conv = cb_ref[...] + shifted(2) * cw[0:1, :] + shifted(1) * cw[1:2, :] + ug * cw[2:3, :]
    gelu = 0.5 * conv * (1.0 + lax.erf(conv * (2.0 ** -0.5)))
    act = (gelu * uv_ref[...].astype(F32)).astype(BF16)
    acc_s[...] += _dot(act, wd_ref[...])

    @pl.when(f == pl.num_programs(2) - 1)
    def _():
        x2 = x1_ref[...] + mod_ref[5:6, :] * acc_s[...]
        o_ref[...] = x2 * lax.rsqrt(jnp.mean(x2 * x2, axis=-1, keepdims=True) + NORM_EPS) * gain_ref[...]


def glu_down(u, conv_w, conv_b, w_down, x1, mod, final_gain, tm=512, tf=1408):
    b, s, d = x1.shape
    ff = w_down.shape[0]
    tm = min(tm, s)
    nf = ff // tf
    cw8 = jnp.zeros((8, ff), F32).at[:conv_w.shape[0]].set(conv_w)
    hb = tm // 8
    return pl.pallas_call(
        functools.partial(_glu_down_kernel, tm=tm),
        grid=(b, s // tm, nf),
        in_specs=[pl.BlockSpec((None, tm, tf), lambda bi, i, f: (bi, i, f)),
                  pl.BlockSpec((None, 8, tf), lambda bi, i, f: (bi, jnp.maximum(i * hb - 1, 0), f)),
                  pl.BlockSpec((None, tm, tf), lambda bi, i, f: (bi, i, nf + f)),
                  pl.BlockSpec((8, tf), lambda bi, i, f: (0, f)),
                  pl.BlockSpec((1, tf), lambda bi, i, f: (0, f)),
                  pl.BlockSpec((tf, d), lambda bi, i, f: (f, 0)),
                  pl.BlockSpec((None, tm, d), lambda bi, i, f: (bi, i, 0)),
                  pl.BlockSpec((None, 8, d), lambda bi, i, f: (bi, 0, 0)),
                  pl.BlockSpec((1, d), lambda bi, i, f: (0, 0))],
        out_specs=pl.BlockSpec((None, tm, d), lambda bi, i, f: (bi, i, 0)),
        out_shape=jax.ShapeDtypeStruct((b, s, d), F32),
        scratch_shapes=[pltpu.VMEM((tm, d), F32)],
        compiler_params=_params("parallel", "parallel", "arbitrary"),
        name="glu_down",
    )(u, u, u, cw8, conv_b.reshape(1, ff), w_down, x1, mod, final_gain.reshape(1, d))


def kernel(x, c, w_ada, b_ada, norm1_gain, w_in, lambda_q1, lambda_k1, lambda_q2, lambda_k2, subln_gain, mu_shift, w0, w_lora_up, a0, a_lora_up, g_lora_up, k_k, k_a, r_k, lnx_w, lnx_b, w_branch, w_out, norm2_gain, w_up, conv_w, conv_b, w_down, final_gain):
    b, s, d = x.shape
    depth = w_in.shape[0]
    assert depth == 1
    l = 0
    qk_cols = 2 * ATT_HEADS * 2 * ATT_HEAD_DIM
    v_cols = ATT_HEADS * 2 * ATT_HEAD_DIM
    width = RWKV_HEADS * RWKV_HEAD_DIM
    n_lora = DECAY_LORA + ICLR_LORA + GATE_LORA
    rkv0 = qk_cols + v_cols
    lora0 = rkv0 + 3 * width
    gate0 = lora0 + n_lora

    c8 = jnp.zeros((8, d), F32).at[:b].set(c)
    mod = ada_modulation(c8, w_ada[l], b_ada[l])[:b]
    mod = jnp.pad(mod.reshape(b, 6, d), ((0, 0), (0, 2), (0, 0)))

    h = norm_mod(x, mod, norm1_gain[l], shift_row=0, scale_row=1)

    w_in_l = w_in[l]
    q_cols = qk_cols // 2
    w_qv_t = jnp.concatenate([w_in_l[:, :q_cols], w_in_l[:, qk_cols:rkv0]], axis=1).T.astype(BF16)
    w_k = w_in_l[:, q_cols:qk_cols].astype(BF16)
    w_gate = w_in_l[:, gate0:].astype(BF16)
    w_rkv = w_in_l[:, rkv0:lora0].astype(BF16)
    w_lora = jnp.pad(w_in_l[:, lora0:gate0], ((0, 0), (0, LORA_PAD - n_lora))).astype(BF16)

    (cos, sina, sinb), rope_rows = rope_tables(s)
    qv_t = matmul_t(h, w_qv_t, *rope_rows)
    kk = matmul(h, w_k, BF16, tn=1024, rope=(cos, sina, sinb, 0))
    gates = matmul(h, w_gate, BF16, tm=2048, tn=1024)
    rkv = matmul(h, w_rkv, F32, tn=1024)
    lora = matmul(h, w_lora, F32, tm=2048, tn=LORA_PAD)

    y_att = diff_attention(qv_t.reshape(b, 2 * ATT_HEADS, LANES, s), kk,
                           lambda_q1[l], lambda_k1[l], lambda_q2[l], lambda_k2[l], subln_gain[l])

    mu = mu_shift[l]
    mu_lora = jnp.pad(mu[3 * width:], (0, LORA_PAD - n_lora))
    y_rwkv = rwkv7(rkv, lora, mu[:3 * width], mu_lora, w0[l], w_lora_up[l], a0[l], a_lora_up[l],
                   g_lora_up[l], k_k[l], k_a[l], r_k[l].reshape(-1), lnx_w[l], lnx_b[l])

    merged = gated_merge(y_att, y_rwkv, gates, w_branch[l, 0].astype(BF16), w_branch[l, 1].astype(BF16))
    x1, h2 = out_proj(merged, w_out[l].astype(BF16), x, mod, norm2_gain[l])

    u = matmul(h2, w_up[l].astype(BF16), BF16, tm=2048, tn=1024)
    return glu_down(u, conv_w[l], conv_b[l], w_down[l].astype(BF16), x1, mod, final_gain)
```

```python
import functools
import math

import jax
import jax.numpy as jnp
from jax import lax
from jax.experimental import pallas as pl
from jax.experimental.pallas import tpu as pltpu

F32 = jnp.float32
BF16 = jnp.bfloat16
HIGHEST = lax.Precision.HIGHEST

ATT_HEADS = 8
ATT_HEAD_DIM = 64
ROT_DIM = ATT_HEAD_DIM // 4
ROPE_THETA = 500000.0
SUBLN_EPS = 1e-5
RWKV_HEADS = 16
RWKV_HEAD_DIM = 64
DECAY_LORA = 64
ICLR_LORA = 64
GATE_LORA = 160
GN_EPS = 64e-5
NORM_EPS = 1e-6
LAMBDA_INIT = 0.8 - 0.6 * math.exp(-0.3 * 0)

LANES = 128
VMEM_LIMIT = 48 * 1024 * 1024
RWKV_CHUNK = 64
RWKV_STEP_CHUNKS = 2
RWKV_GROUP_BLOCKS = 8
LORA_PAD = 384


def _params(*sem):
    return pltpu.CompilerParams(dimension_semantics=sem, vmem_limit_bytes=VMEM_LIMIT)


def _dot(a, b, precision=None):
    return jnp.dot(a, b, preferred_element_type=F32, precision=precision)


def _dot_t(a, b, precision=None):
    return lax.dot_general(a, b, (((1,), (1,)), ((), ())),
                           preferred_element_type=F32, precision=precision)


def _mod_kernel(c_ref, w_ref, b_ref, o_ref):
    c = c_ref[...]
    s = c * jax.nn.sigmoid(c)
    o_ref[...] = _dot(s, w_ref[...], HIGHEST) + b_ref[...]


def ada_modulation(c8, w_ada, b_ada, tn=1024):
    m, d = c8.shape
    n = w_ada.shape[1]
    return pl.pallas_call(
        _mod_kernel,
        grid=(n // tn,),
        in_specs=[pl.BlockSpec((m, d), lambda j: (0, 0)),
                  pl.BlockSpec((d, tn), lambda j: (0, j)),
                  pl.BlockSpec((1, tn), lambda j: (0, j))],
        out_specs=pl.BlockSpec((m, tn), lambda j: (0, j)),
        out_shape=jax.ShapeDtypeStruct((m, n), F32),
        compiler_params=_params("arbitrary"),
        name="ada_mod",
    )(c8, w_ada, b_ada.reshape(1, n))


def _norm_mod_kernel(x_ref, mod_ref, gain_ref, o_ref, *, shift_row, scale_row):
    x = x_ref[...]
    xn = x * lax.rsqrt(jnp.mean(x * x, axis=-1, keepdims=True) + NORM_EPS)
    h = xn * gain_ref[...] * (1.0 + mod_ref[scale_row:scale_row + 1, :]) + mod_ref[shift_row:shift_row + 1, :]
    o_ref[...] = h.astype(o_ref.dtype)


def norm_mod(x, mod, gain, shift_row, scale_row, tm=1024):
    b, s, d = x.shape
    tm = min(tm, s)
    return pl.pallas_call(
        functools.partial(_norm_mod_kernel, shift_row=shift_row, scale_row=scale_row),
        grid=(b, s // tm),
        in_specs=[pl.BlockSpec((None, tm, d), lambda bi, i: (bi, i, 0)),
                  pl.BlockSpec((None, 8, d), lambda bi, i: (bi, 0, 0)),
                  pl.BlockSpec((1, d), lambda bi, i: (0, 0))],
        out_specs=pl.BlockSpec((None, tm, d), lambda bi, i: (bi, i, 0)),
        out_shape=jax.ShapeDtypeStruct((b, s, d), BF16),
        compiler_params=_params("parallel", "parallel"),
        name="norm_mod",
    )(x, mod, gain.reshape(1, d))


def _mm_kernel(a_ref, w_ref, o_ref):
    o_ref[...] = _dot(a_ref[...], w_ref[...]).astype(o_ref.dtype)


def _mm_rope_kernel(a_ref, w_ref, cos_ref, sina_ref, sinb_ref, o_ref, *, tn, q_cols):
    acc = _dot(a_ref[...], w_ref[...])
    scale = jnp.where(pl.program_id(2) * tn < q_cols, ATT_HEAD_DIM ** -0.5 * math.log2(math.e), 1.0).astype(F32)
    cos, sina, sinb = cos_ref[...], sina_ref[...], sinb_ref[...]
    for j in range(tn // LANES):
        t = acc[:, j * LANES:(j + 1) * LANES]
        half = ROT_DIM // 2
        r = t * cos + pltpu.roll(t, LANES - half, 1) * sina + pltpu.roll(t, half, 1) * sinb
        o_ref[:, j * LANES:(j + 1) * LANES] = (r * scale).astype(o_ref.dtype)


def matmul(a, w, out_dtype, tm=1024, tn=512, rope=None):
    b, s, k = a.shape
    n = w.shape[1]
    tm = min(tm, s)
    tn = min(tn, n)
    assert s % tm == 0 and n % tn == 0
    in_specs = [pl.BlockSpec((None, tm, k), lambda bi, i, j: (bi, i, 0)),
                pl.BlockSpec((k, tn), lambda bi, i, j: (0, j))]
    args = [a, w]
    if rope is None:
        body = _mm_kernel
        name = "matmul"
    else:
        cos, sina, sinb, q_cols = rope
        body = functools.partial(_mm_rope_kernel, tn=tn, q_cols=q_cols)
        in_specs += [pl.BlockSpec((tm, LANES), lambda bi, i, j: (i, 0))] * 3
        args += [cos, sina, sinb]
        name = "matmul_rope"
    return pl.pallas_call(
        body,
        grid=(b, s // tm, n // tn),
        in_specs=in_specs,
        out_specs=pl.BlockSpec((None, tm, tn), lambda bi, i, j: (bi, i, j)),
        out_shape=jax.ShapeDtypeStruct((b, s, n), out_dtype),
        compiler_params=_params("parallel", "parallel", "arbitrary"),
        name=name,
    )(*args)


def _mm_t_kernel(wt_ref, a_ref, cos_ref, sina_ref, sinb_ref, o_ref):
    acc = _dot_t(wt_ref[...], a_ref[...])
    tn, tm = acc.shape
    is_q = pl.program_id(2) == 0
    cos = jnp.where(is_q, cos_ref[...], 1.0)
    sina = jnp.where(is_q, sina_ref[...], 0.0)
    sinb = jnp.where(is_q, sinb_ref[...], 0.0)
    scale = jnp.where(is_q, ATT_HEAD_DIM ** -0.5 * math.log2(math.e), 1.0).astype(F32)
    half = ROT_DIM // 2
    per_map = lambda x: x.reshape(tn // ATT_HEAD_DIM, ATT_HEAD_DIM, tm)
    r = (per_map(acc) * cos[None] + per_map(pltpu.roll(acc, tn - half, 0)) * sina[None]
         + per_map(pltpu.roll(acc, half, 0)) * sinb[None])
    o_ref[...] = (r.reshape(tn, tm) * scale).astype(o_ref.dtype)


def matmul_t(a, wt, cos_t, sina_t, sinb_t, tm=1024, tn=1024):
    b, s, k = a.shape
    n = wt.shape[0]
    tm = min(tm, s)
    table = pl.BlockSpec((ATT_HEAD_DIM, tm), lambda bi, i, j: (0, i))
    return pl.pallas_call(
        _mm_t_kernel,
        grid=(b, s // tm, n // tn),
        in_specs=[pl.BlockSpec((tn, k), lambda bi, i, j: (j, 0)),
                  pl.BlockSpec((None, tm, k), lambda bi, i, j: (bi, i, 0)),
                  table, table, table],
        out_specs=pl.BlockSpec((None, tn, tm), lambda bi, i, j: (bi, j, i)),
        out_shape=jax.ShapeDtypeStruct((b, n, s), BF16),
        compiler_params=_params("parallel", "parallel", "arbitrary"),
        name="matmul_t",
    )(wt, a, cos_t, sina_t, sinb_t)


def rope_tables(seq):
    half = ROT_DIM // 2
    pos = jnp.arange(seq, dtype=F32)
    inv = ROPE_THETA ** (-jnp.arange(0, ROT_DIM, 2, dtype=F32) / ROT_DIM)
    ang = pos[:, None] * inv[None, :]
    cos, sin = jnp.cos(ang), jnp.sin(ang)
    ones = jnp.ones((seq, ATT_HEAD_DIM - ROT_DIM), F32)
    zeros = jnp.zeros((seq, ATT_HEAD_DIM - ROT_DIM), F32)
    zh = jnp.zeros((seq, half), F32)
    cos64 = jnp.concatenate([cos, cos, ones], axis=1)
    sina64 = jnp.concatenate([-sin, zh, zeros], axis=1)
    sinb64 = jnp.concatenate([zh, sin, zeros], axis=1)
    rep = LANES // ATT_HEAD_DIM
    by_lane = (jnp.tile(cos64, (1, rep)), jnp.tile(sina64, (1, rep)), jnp.tile(sinb64, (1, rep)))
    by_row = (cos64.T, sina64.T, sinb64.T)
    return by_lane, by_row


SUBLANES = 8


def _sublane_all(x, op):
    for shift in (4, 2, 1):
        x = op(x, pltpu.roll(x, shift, 0))
    return x


def _attn_kernel(qt_ref, k_ref, vt_ref, lq1_ref, lk1_ref, lq2_ref, lk2_ref, gain_ref, o_ref,
                 acc_s, *, tq, tk):
    qi = pl.program_id(2)
    qt = qt_ref[...]
    row = lax.broadcasted_iota(jnp.int32, qt.shape, 0)
    zero = jnp.zeros_like(qt)
    q2 = jnp.concatenate([jnp.where(row < ATT_HEAD_DIM, qt, zero),
                          jnp.where(row >= ATT_HEAD_DIM, qt, zero)], axis=1)
    n2 = 2 * tq
    acc_s[...] = jnp.zeros(acc_s.shape, F32)

    def update(k0, nk, c0, nc, m_all, l_all, masked):
        cols = slice(c0, c0 + nc)
        kb = k_ref[pl.ds(k0, nk), :]
        vtb = vt_ref[:, pl.ds(k0, nk)]
        s = _dot(kb, q2[:, cols])
        if masked:
            kpos = k0 + lax.broadcasted_iota(jnp.int32, (nk, nc), 0)
            col = c0 + lax.broadcasted_iota(jnp.int32, (nk, nc), 1)
            qpos = qi * tq + jnp.where(col >= tq, col - tq, col)
            s = jnp.where(kpos <= qpos, s, -jnp.inf)
        s3 = s.reshape(nk // SUBLANES, SUBLANES, nc)
        m_old = m_all[:, cols]
        m_new = jnp.maximum(m_old, _sublane_all(jnp.max(s3, axis=0), jnp.maximum))
        p3 = jnp.exp2(s3 - m_new[None])
        corr = jnp.exp2(m_old - m_new)
        l_new = l_all[:, cols] * corr + jnp.sum(p3, axis=0)
        pv = _dot(vtb, p3.reshape(nk, nc).astype(BF16))
        acc3 = acc_s[:, cols].reshape(LANES // SUBLANES, SUBLANES, nc)
        acc_s[:, cols] = (acc3 * corr[None]).reshape(LANES, nc) + pv
        if nc == n2:
            return m_new, l_new
        put = lambda full, part: jnp.concatenate(
            ([full[:, :c0]] if c0 else []) + [part] + ([full[:, c0 + nc:]] if c0 + nc < n2 else []), axis=1)
        return put(m_all, m_new), put(l_all, l_new)

    m0 = jnp.full((SUBLANES, n2), -jnp.inf, F32)
    l0 = jnp.zeros((SUBLANES, n2), F32)
    m, l = lax.fori_loop(0, (qi * tq) // tk,
                         lambda j, c: update(pl.multiple_of(j * tk, tk), tk, 0, n2, c[0], c[1], False), (m0, l0))
    half = tq // 2
    kd = pl.multiple_of(qi * tq, tq)
    m, l = update(kd, half, 0, n2, m, l, True)
    m, l = update(kd + half, half, half, half, m, l, True)
    m, l = update(kd + half, half, tq + half, half, m, l, True)

    lam = (jnp.exp(jnp.sum(lq1_ref[...] * lk1_ref[...], axis=-1, keepdims=True))
           - jnp.exp(jnp.sum(lq2_ref[...] * lk2_ref[...], axis=-1, keepdims=True))
           + LAMBDA_INIT)
    l = _sublane_all(l, jnp.add)
    o3 = acc_s[...].reshape(LANES // SUBLANES, SUBLANES, n2) / l[None]
    d3 = o3[:, :, :tq] - lam * o3[:, :, tq:]
    ms = _sublane_all(jnp.sum(d3 * d3, axis=0), jnp.add) * (1.0 / LANES)
    d3 = d3 * lax.rsqrt(ms + SUBLN_EPS)[None]
    o = d3.reshape(LANES, tq).T
    o_ref[...] = (o * gain_ref[...] * (1.0 - LAMBDA_INIT)).astype(o_ref.dtype)


def diff_attention(qvt, kk, lq1, lk1, lq2, lk2, subln_gain, tq=1024, tk=1024):
    b, h2, _, s = qvt.shape
    h = h2 // 2
    tq = min(tq, s)
    tk = min(tk, tq)
    vec = lambda a: a.reshape(1, -1).astype(F32)
    small = lambda n: pl.BlockSpec((1, n), lambda bi, hi, i: (0, 0))
    return pl.pallas_call(
        functools.partial(_attn_kernel, tq=tq, tk=tk),
        grid=(b, h, s // tq),
        in_specs=[pl.BlockSpec((None, None, LANES, tq), lambda bi, hi, i: (bi, hi, 0, i)),
                  pl.BlockSpec((None, s, LANES), lambda bi, hi, i: (bi, 0, hi)),
                  pl.BlockSpec((None, None, LANES, s), lambda bi, hi, i: (bi, h + hi, 0, 0)),
                  small(ATT_HEAD_DIM), small(ATT_HEAD_DIM), small(ATT_HEAD_DIM), small(ATT_HEAD_DIM),
                  small(LANES)],
        out_specs=pl.BlockSpec((None, tq, LANES), lambda bi, hi, i: (bi, i, hi)),
        out_shape=jax.ShapeDtypeStruct((b, s, h * LANES), BF16),
        scratch_shapes=[pltpu.VMEM((LANES, 2 * tq), F32)],
        compiler_params=_params("parallel", "parallel", "arbitrary"),
        name="diff_attention",
    )(qvt, kk, qvt, vec(lq1), vec(lk1), vec(lq2), vec(lk2), vec(subln_gain))


def _bdot(a, b):
    return jnp.dot(a.astype(BF16), b.astype(BF16), preferred_element_type=F32)


def _dot3(a, b):
    a_hi = a.astype(BF16)
    a_lo = (a - a_hi.astype(F32)).astype(BF16)
    b_hi = b.astype(BF16)
    b_lo = (b - b_hi.astype(F32)).astype(BF16)
    d = functools.partial(jnp.dot, preferred_element_type=F32)
    return d(a_hi, b_hi) + (d(a_hi, b_lo) + d(a_lo, b_hi))


def _each(f, *lists):
    return [f(*xs) for xs in zip(*lists)]


def _rwkv_pairs(pairs, states, c):
    n = RWKV_HEAD_DIM
    c2 = 2 * c
    row = lax.broadcasted_iota(jnp.int32, (c2, c2), 0)
    col = lax.broadcasted_iota(jnp.int32, (c2, c2), 1)
    lower = row >= col
    strict = row > col
    eye2 = (row == col).astype(F32)
    rc = lax.broadcasted_iota(jnp.int32, (c, c), 0)
    cc = lax.broadcasted_iota(jnp.int32, (c, c), 1)
    lower_b = (rc >= cc).astype(BF16)
    li = lax.broadcasted_iota(jnp.int32, (LANES, LANES), 0)
    lj = lax.broadcasted_iota(jnp.int32, (LANES, LANES), 1)
    same_head = ((li // n) == (lj // n)).astype(BF16)
    eye_l = (li == lj).astype(F32)
    head1 = lax.broadcasted_iota(jnp.int32, (c, LANES), 1) >= n

    def stack(x):
        return jnp.concatenate([jnp.where(head1, 0.0, x), jnp.where(head1, x, 0.0)], axis=0)

    def head_sum(x):
        return jnp.dot(x.astype(BF16), same_head, preferred_element_type=F32)

    r, k, v, lw, ai, g, kk_w, ka_w, rk_w, lnw, lnb = [list(x) for x in zip(*pairs)]

    kk = _each(lambda k, w: k * w, k, kk_w)
    kk = _each(lambda kk: kk * lax.rsqrt(jnp.maximum(head_sum(kk * kk), 1e-24)), kk)
    k2 = _each(lambda k, ai, w: k * (1.0 + (ai - 1.0) * w), k, ai, ka_w)
    b = _each(lambda kk, ai: kk * ai, kk, ai)
    bonus = _each(lambda r, k2, w, v: head_sum(r * k2 * w) * v, r, k2, rk_w, v)

    def cumsum(lw):
        hi = lw.astype(BF16)
        r1 = lw - hi.astype(F32)
        mid = r1.astype(BF16)
        lo = (r1 - mid.astype(F32)).astype(BF16)
        c3 = jnp.dot(lower_b, jnp.concatenate([hi, mid, lo], axis=1), preferred_element_type=F32)
        return c3[:, :LANES] + c3[:, LANES:2 * LANES] + c3[:, 2 * LANES:]

    cum = _each(cumsum, lw)
    e_cum = _each(jnp.exp, cum)
    e_neg = _each(lambda cm: jnp.exp(-cm), cum)
    e_tail = _each(lambda cm: jnp.exp(cm[c - 1:c, :] - cm), cum)
    at = _each(lambda kk, cm, l: stack(-kk * jnp.exp(cm - l)), kk, cum, lw)
    rt = _each(lambda r, e: stack(r * e), r, e_cum)
    vs = _each(stack, v)
    qq = _each(lambda x, y: jnp.concatenate([x, y], axis=0).astype(BF16), at, rt)
    kq = _each(lambda b, k2, e: jnp.concatenate([stack(b * e), stack(k2 * e)], axis=0).astype(BF16),
               b, k2, e_neg)
    s = _each(_dot_t, qq, kq)
    a_ab = _each(lambda s: jnp.where(strict, s[:c2, :c2], 0.0), s)
    a_ak = _each(lambda s: jnp.where(strict, s[:c2, c2:], 0.0), s)
    a_rb = _each(lambda s: jnp.where(lower, s[c2:, :c2], 0.0), s)
    a_rk = _each(lambda s: jnp.where(lower, s[c2:, c2:], 0.0), s)

    def level(m):
        return ((row // (2 * m)) == (col // (2 * m))) & ((row // m) % 2 == 1) & ((col // m) % 2 == 0)

    t = _each(lambda x: eye2 + jnp.where(level(1), x, 0.0), a_ab)
    m = 2
    while m < c:
        sel = level(m)
        tn = _each(lambda t, x: _bdot(t, jnp.where(sel, x, 0.0)), t, a_ab)
        t = _each(lambda t, x: t + _bdot(x, t), t, tn)
        m *= 2

    x1 = _each(_bdot, a_ak, vs)
    wu = _each(lambda t, at, x1: _bdot(t, jnp.concatenate([at, x1], axis=1)), t, at, x1)
    bh_t = _each(lambda b, e: stack(b * e).T, b, e_tail)
    kh_t = _each(lambda k2, e: stack(k2 * e).T, k2, e_tail)
    z1 = _each(lambda a, bt, wu: _bdot(jnp.concatenate([a, bt], axis=0), wu), a_rb, bh_t, wu)
    z2 = _each(lambda a, kt, v: _bdot(jnp.concatenate([a, kt], axis=0), v), a_rk, kh_t, vs)
    pm = _each(lambda z1, rt, e: jnp.concatenate([rt + z1[:c2, :LANES],
                                                  eye_l * e[c - 1:c, :] + z1[c2:, :LANES]], axis=0),
               z1, rt, e_cum)
    yn0 = _each(lambda z1, z2: z1[:, LANES:] + z2, z1, z2)
    chunks = len(pairs) // len(states)
    h_next = list(states)
    yh = [None] * len(pairs)
    for j in range(chunks):
        for blk in range(len(states)):
            i = blk * chunks + j
            yh[i] = _bdot(pm[i], h_next[blk]) + yn0[i]
            h_next[blk] = yh[i][c2:]

    y = _each(lambda yh: yh[:c] + yh[c:c2], yh)
    d = _each(lambda y: y - head_sum(y) * (1.0 / n), y)
    var = _each(lambda d: head_sum(d * d) * (1.0 / n), d)
    outs = _each(lambda d, var, lnw, lnb, bonus, g: (d * lax.rsqrt(var + GN_EPS) * lnw + lnb + bonus) * g,
                 d, var, lnw, lnb, bonus, g)
    return outs, h_next


def _rwkv_kernel(rkv_ref, lora_ref, mu_rkv_ref, mu_lora_ref, w0_ref, wl_ref, a0_ref, al_ref, gl_ref,
                 kk_ref, ka_ref, rk_ref, lnw_ref, lnb_ref, o_ref,
                 state_s, prev_rkv_s, prev_lora_s, r_s, k_s, v_s, lw_s, ai_s, g_s, *, c):
    ci = pl.program_id(1)
    width = RWKV_HEADS * RWKV_HEAD_DIM

    @pl.when(ci == 0)
    def _():
        state_s[...] = jnp.zeros(state_s.shape, F32)
        prev_rkv_s[...] = jnp.zeros(prev_rkv_s.shape, F32)
        prev_lora_s[...] = jnp.zeros(prev_lora_s.shape, F32)

    def shift_mix(p, prev_s, mu):
        rolled = pltpu.roll(p, 1, 0)
        first = lax.broadcasted_iota(jnp.int32, p.shape, 0) == 0
        shifted = jnp.where(first, prev_s[7:8, :], rolled)
        prev_s[...] = p[p.shape[0] - 8:, :]
        return p + mu * (shifted - p)

    z = shift_mix(rkv_ref[...], prev_rkv_s, mu_rkv_ref[...])
    zl = shift_mix(lora_ref[...], prev_lora_s, mu_lora_ref[...])
    r_s[...] = z[:, :width]
    k_s[...] = z[:, width:2 * width]
    v_s[...] = z[:, 2 * width:]
    xw = zl[:, :DECAY_LORA]
    xa = zl[:, DECAY_LORA:DECAY_LORA + ICLR_LORA]
    xg = zl[:, DECAY_LORA + ICLR_LORA:DECAY_LORA + ICLR_LORA + GATE_LORA]
    w_logit = -jax.nn.softplus(-(w0_ref[...] + _dot3(jnp.tanh(xw), wl_ref[...]))) - 0.5
    lw_s[...] = -jnp.exp(w_logit)
    ai_s[...] = jax.nn.sigmoid(a0_ref[...] + _dot3(xa, al_ref[...]))
    g_s[...] = _dot3(jax.nn.sigmoid(xg), gl_ref[...])

    def group_body(gi, carry):
        slices = [pl.ds(pl.multiple_of((gi * RWKV_GROUP_BLOCKS + blk) * LANES, LANES), LANES)
                  for blk in range(RWKV_GROUP_BLOCKS)]
        chunk_rows = [pl.ds(j * c, c) for j in range(r_s.shape[0] // c)]
        pairs = [tuple(ref[rows, sl] for ref in (r_s, k_s, v_s, lw_s, ai_s, g_s))
                 + tuple(ref[:, sl] for ref in (kk_ref, ka_ref, rk_ref, lnw_ref, lnb_ref))
                 for sl in slices for rows in chunk_rows]
        states = [state_s[gi * RWKV_GROUP_BLOCKS + blk] for blk in range(RWKV_GROUP_BLOCKS)]
        outs, h_next = _rwkv_pairs(pairs, states, c)
        for blk, sl in enumerate(slices):
            for j, rows in enumerate(chunk_rows):
                o_ref[rows, sl] = outs[blk * len(chunk_rows) + j].astype(o_ref.dtype)
            state_s[gi * RWKV_GROUP_BLOCKS + blk] = h_next[blk]
        return carry

    lax.fori_loop(0, width // (LANES * RWKV_GROUP_BLOCKS), group_body, 0)


def rwkv7(rkv, lora, mu_rkv, mu_lora, w0, wl, a0, al, gl, k_k, k_a, r_k, lnx_w, lnx_b):
    b, s, _ = rkv.shape
    c = min(RWKV_CHUNK, s)
    rows = min(RWKV_STEP_CHUNKS * c, s)
    width = RWKV_HEADS * RWKV_HEAD_DIM
    row = lambda a: a.reshape(1, -1).astype(F32)
    const = lambda shape: pl.BlockSpec(shape, lambda bi, ci: (0,) * len(shape))
    return pl.pallas_call(
        functools.partial(_rwkv_kernel, c=c),
        grid=(b, s // rows),
        in_specs=[pl.BlockSpec((None, rows, 3 * width), lambda bi, ci: (bi, ci, 0)),
                  pl.BlockSpec((None, rows, LORA_PAD), lambda bi, ci: (bi, ci, 0)),
                  const((1, 3 * width)), const((1, LORA_PAD)),
                  const((1, width)), const((DECAY_LORA, width)),
                  const((1, width)), const((ICLR_LORA, width)), const((GATE_LORA, width)),
                  const((1, width)), const((1, width)), const((1, width)),
                  const((1, width)), const((1, width))],
        out_specs=pl.BlockSpec((None, rows, width), lambda bi, ci: (bi, ci, 0)),
        out_shape=jax.ShapeDtypeStruct((b, s, width), BF16),
        scratch_shapes=[pltpu.VMEM((width // LANES, LANES, LANES), F32),
                        pltpu.VMEM((8, 3 * width), F32), pltpu.VMEM((8, LORA_PAD), F32)]
                       + [pltpu.VMEM((rows, width), F32)] * 6,
        compiler_params=_params("parallel", "arbitrary"),
        name="rwkv7",
    )(rkv, lora, row(mu_rkv), row(mu_lora), row(w0), wl.astype(F32), row(a0), al.astype(F32),
      gl.astype(F32), row(k_k), row(k_a), row(r_k), row(lnx_w), row(lnx_b))


MERGE_CHUNK = 256


def _merge_kernel(ya_ref, yr_ref, g0_ref, g1_ref, wa_ref, wr_ref, o_ref):
    ya, yr = ya_ref[...], yr_ref[...]
    for c0 in range(0, o_ref.shape[1], MERGE_CHUNK):
        cs = slice(c0, c0 + MERGE_CHUNK)
        za = _dot(ya, wa_ref[:, cs])
        zr = _dot(yr, wr_ref[:, cs])
        ga = jax.nn.sigmoid(g0_ref[:, cs].astype(F32))
        gr = jax.nn.sigmoid(g1_ref[:, cs].astype(F32))
        o_ref[:, cs] = (ga * za + gr * zr).astype(o_ref.dtype)


def gated_merge(y_att, y_rwkv, gates, wa, wr, tm=1024, tn=1024):
    b, s, kdim = y_att.shape
    n = wa.shape[1]
    tm = min(tm, s)
    nb = n // tn
    return pl.pallas_call(
        _merge_kernel,
        grid=(b, s // tm, nb),
        in_specs=[pl.BlockSpec((None, tm, kdim), lambda bi, i, j: (bi, i, 0)),
                  pl.BlockSpec((None, tm, kdim), lambda bi, i, j: (bi, i, 0)),
                  pl.BlockSpec((None, tm, tn), lambda bi, i, j: (bi, i, j)),
                  pl.BlockSpec((None, tm, tn), lambda bi, i, j: (bi, i, nb + j)),
                  pl.BlockSpec((kdim, tn), lambda bi, i, j: (0, j)),
                  pl.BlockSpec((kdim, tn), lambda bi, i, j: (0, j))],
        out_specs=pl.BlockSpec((None, tm, tn), lambda bi, i, j: (bi, i, j)),
        out_shape=jax.ShapeDtypeStruct((b, s, n), BF16),
        compiler_params=_params("parallel", "parallel", "arbitrary"),
        name="gated_merge",
    )(y_att, y_rwkv, gates, gates, wa, wr)


def _out_proj_kernel(m_ref, w_ref, x_ref, mod_ref, gain_ref, x1_ref, h2_ref):
    x1 = x_ref[...] + mod_ref[2:3, :] * _dot(m_ref[...], w_ref[...])
    x1_ref[...] = x1
    xn = x1 * lax.rsqrt(jnp.mean(x1 * x1, axis=-1, keepdims=True) + NORM_EPS)
    h2_ref[...] = (xn * gain_ref[...] * (1.0 + mod_ref[4:5, :]) + mod_ref[3:4, :]).astype(h2_ref.dtype)


def out_proj(merged, w_out, x, mod, gain2, tm=512):
    b, s, d = x.shape
    tm = min(tm, s)
    return pl.pallas_call(
        _out_proj_kernel,
        grid=(b, s // tm),
        in_specs=[pl.BlockSpec((None, tm, d), lambda bi, i: (bi, i, 0)),
                  pl.BlockSpec((d, d), lambda bi, i: (0, 0), pipeline_mode=pl.Buffered(1)),
                  pl.BlockSpec((None, tm, d), lambda bi, i: (bi, i, 0)),
                  pl.BlockSpec((None, 8, d), lambda bi, i: (bi, 0, 0)),
                  pl.BlockSpec((1, d), lambda bi, i: (0, 0))],
        out_specs=[pl.BlockSpec((None, tm, d), lambda bi, i: (bi, i, 0)),
                   pl.BlockSpec((None, tm, d), lambda bi, i: (bi, i, 0))],
        out_shape=[jax.ShapeDtypeStruct((b, s, d), F32), jax.ShapeDtypeStruct((b, s, d), BF16)],
        compiler_params=_params("parallel", "parallel"),
        name="out_proj",
    )(merged, w_out, x, mod, gain2.reshape(1, d))


GLU_HALO = 16
GLU_CHUNK = 256


def _glu_up_kernel(h_ref, halo_ref, wg_ref, wv_ref, cw_ref, cb_ref, o_ref, hext_s):
    i = pl.program_id(1)

    @pl.when(pl.program_id(2) == 0)
    def _():
        hext_s[:GLU_HALO, :] = jnp.where(i > 0, halo_ref[...], jnp.zeros_like(halo_ref[...]))
        hext_s[GLU_HALO:, :] = h_ref[...]

    hext = hext_s[...]
    chunks = [slice(c0, c0 + GLU_CHUNK) for c0 in range(0, o_ref.shape[1], GLU_CHUNK)]

    def gate(ug, cs):
        cw = cw_ref[:, cs]
        conv = (cb_ref[:, cs] + pltpu.roll(ug, 2, 0)[GLU_HALO:] * cw[0:1, :]
                + pltpu.roll(ug, 1, 0)[GLU_HALO:] * cw[1:2, :] + ug[GLU_HALO:] * cw[2:3, :])
        return 0.5 * conv * (1.0 + lax.erf(conv * (2.0 ** -0.5)))

    ug = _dot(hext, wg_ref[:, chunks[0]])
    gates = []
    for cs, nxt in zip(chunks, chunks[1:] + [None]):
        ug_next = _dot(hext, wg_ref[:, nxt]) if nxt is not None else None
        gates.append(gate(ug, cs))
        ug = ug_next
    for cs, g in zip(chunks, gates):
        o_ref[:, cs] = (g * _dot(hext[GLU_HALO:], wv_ref[:, cs])).astype(o_ref.dtype)


def glu_up(h, w_up, conv_w, conv_b, tm=1024, tf=512):
    b, s, d = h.shape
    ff = w_up.shape[1] // 2
    tm = min(tm, s)
    nf = ff // tf
    cw8 = jnp.zeros((8, ff), F32).at[:conv_w.shape[0]].set(conv_w)
    hb = tm // GLU_HALO
    return pl.pallas_call(
        _glu_up_kernel,
        grid=(b, s // tm, nf),
        in_specs=[pl.BlockSpec((None, tm, d), lambda bi, i, f: (bi, i, 0)),
                  pl.BlockSpec((None, GLU_HALO, d), lambda bi, i, f: (bi, jnp.maximum(i * hb - 1, 0), 0)),
                  pl.BlockSpec((d, tf), lambda bi, i, f: (0, f)),
                  pl.BlockSpec((d, tf), lambda bi, i, f: (0, nf + f)),
                  pl.BlockSpec((8, tf), lambda bi, i, f: (0, f)),
                  pl.BlockSpec((1, tf), lambda bi, i, f: (0, f))],
        out_specs=pl.BlockSpec((None, tm, tf), lambda bi, i, f: (bi, i, f)),
        out_shape=jax.ShapeDtypeStruct((b, s, ff), BF16),
        scratch_shapes=[pltpu.VMEM((GLU_HALO + tm, d), BF16)],
        compiler_params=_params("parallel", "parallel", "arbitrary"),
        name="glu_up",
    )(h, h, w_up, w_up, cw8, conv_b.reshape(1, ff))


def _down_kernel(a_ref, wd_ref, x1_ref, mod_ref, gain_ref, o_ref):
    x2 = x1_ref[...] + mod_ref[5:6, :] * _dot(a_ref[...], wd_ref[...])
    o_ref[...] = x2 * lax.rsqrt(jnp.mean(x2 * x2, axis=-1, keepdims=True) + NORM_EPS) * gain_ref[...]


def down_proj(act, w_down, x1, mod, final_gain, tm=256):
    b, s, d = x1.shape
    ff = w_down.shape[0]
    tm = min(tm, s)
    return pl.pallas_call(
        _down_kernel,
        grid=(b, s // tm),
        in_specs=[pl.BlockSpec((None, tm, ff), lambda bi, i: (bi, i, 0)),
                  pl.BlockSpec((ff, d), lambda bi, i: (0, 0), pipeline_mode=pl.Buffered(1)),
                  pl.BlockSpec((None, tm, d), lambda bi, i: (bi, i, 0)),
                  pl.BlockSpec((None, 8, d), lambda bi, i: (bi, 0, 0)),
                  pl.BlockSpec((1, d), lambda bi, i: (0, 0))],
        out_specs=pl.BlockSpec((None, tm, d), lambda bi, i: (bi, i, 0)),
        out_shape=jax.ShapeDtypeStruct((b, s, d), F32),
        compiler_params=_params("parallel", "parallel"),
        name="down_proj",
    )(act, w_down, x1, mod, final_gain.reshape(1, d))


def kernel(x, c, w_ada, b_ada, norm1_gain, w_in, lambda_q1, lambda_k1, lambda_q2, lambda_k2, subln_gain, mu_shift, w0, w_lora_up, a0, a_lora_up, g_lora_up, k_k, k_a, r_k, lnx_w, lnx_b, w_branch, w_out, norm2_gain, w_up, conv_w, conv_b, w_down, final_gain):
    b, s, d = x.shape
    depth = w_in.shape[0]
    assert depth == 1
    l = 0
    qk_cols = 2 * ATT_HEADS * 2 * ATT_HEAD_DIM
    v_cols = ATT_HEADS * 2 * ATT_HEAD_DIM
    width = RWKV_HEADS * RWKV_HEAD_DIM
    n_lora = DECAY_LORA + ICLR_LORA + GATE_LORA
    rkv0 = qk_cols + v_cols
    lora0 = rkv0 + 3 * width
    gate0 = lora0 + n_lora

    c8 = jnp.zeros((8, d), F32).at[:b].set(c)
    mod = ada_modulation(c8, w_ada[l], b_ada[l])[:b]
    mod = jnp.pad(mod.reshape(b, 6, d), ((0, 0), (0, 2), (0, 0)))

    h = norm_mod(x, mod, norm1_gain[l], shift_row=0, scale_row=1)

    w_in_l = w_in[l]
    q_cols = qk_cols // 2
    w_qv_t = jnp.concatenate([w_in_l[:, :q_cols], w_in_l[:, qk_cols:rkv0]], axis=1).T.astype(BF16)
    w_k = w_in_l[:, q_cols:qk_cols].astype(BF16)
    w_gate = w_in_l[:, gate0:].astype(BF16)
    w_rkv = w_in_l[:, rkv0:lora0].astype(BF16)
    w_lora = jnp.pad(w_in_l[:, lora0:gate0], ((0, 0), (0, LORA_PAD - n_lora))).astype(BF16)

    (cos, sina, sinb), rope_rows = rope_tables(s)
    qv_t = matmul_t(h, w_qv_t, *rope_rows)
    kk = matmul(h, w_k, BF16, tn=1024, rope=(cos, sina, sinb, 0))
    gates = matmul(h, w_gate, BF16, tm=2048, tn=1024)
    rkv = matmul(h, w_rkv, F32, tn=1024)
    lora = matmul(h, w_lora, F32, tm=2048, tn=LORA_PAD)

    y_att = diff_attention(qv_t.reshape(b, 2 * ATT_HEADS, LANES, s), kk,
                           lambda_q1[l], lambda_k1[l], lambda_q2[l], lambda_k2[l], subln_gain[l])

    mu = mu_shift[l]
    mu_lora = jnp.pad(mu[3 * width:], (0, LORA_PAD - n_lora))
    y_rwkv = rwkv7(rkv, lora, mu[:3 * width], mu_lora, w0[l], w_lora_up[l], a0[l], a_lora_up[l],
                   g_lora_up[l], k_k[l], k_a[l], r_k[l].reshape(-1), lnx_w[l], lnx_b[l])

    merged = gated_merge(y_att, y_rwkv, gates, w_branch[l, 0].astype(BF16), w_branch[l, 1].astype(BF16))
    x1, h2 = out_proj(merged, w_out[l].astype(BF16), x, mod, norm2_gain[l])

    act = glu_up(h2, w_up[l].astype(BF16), conv_w[l], conv_b[l])
    return down_proj(act, w_down[l].astype(BF16), x1, mod, final_gain)
```

```python
import functools
import math

import jax
import jax.numpy as jnp
from jax import lax
from jax.experimental import pallas as pl
from jax.experimental.pallas import tpu as pltpu

F32 = jnp.float32
BF16 = jnp.bfloat16
HIGHEST = lax.Precision.HIGHEST

ATT_HEADS = 8
ATT_HEAD_DIM = 64
ROT_DIM = ATT_HEAD_DIM // 4
ROPE_THETA = 500000.0
SUBLN_EPS = 1e-5
RWKV_HEADS = 16
RWKV_HEAD_DIM = 64
DECAY_LORA = 64
ICLR_LORA = 64
GATE_LORA = 160
GN_EPS = 64e-5
NORM_EPS = 1e-6
LAMBDA_INIT = 0.8 - 0.6 * math.exp(-0.3 * 0)

LANES = 128
VMEM_LIMIT = 48 * 1024 * 1024
RWKV_CHUNK = 64
RWKV_STEP_CHUNKS = 2
RWKV_GROUP_BLOCKS = 8
LORA_PAD = 384


def _params(*sem):
    return pltpu.CompilerParams(dimension_semantics=sem, vmem_limit_bytes=VMEM_LIMIT)


def _dot(a, b, precision=None):
    return jnp.dot(a, b, preferred_element_type=F32, precision=precision)


def _dot_t(a, b, precision=None):
    return lax.dot_general(a, b, (((1,), (1,)), ((), ())),
                           preferred_element_type=F32, precision=precision)


def _mod_kernel(c_ref, w_ref, b_ref, o_ref):
    c = c_ref[...]
    s = c * jax.nn.sigmoid(c)
    o_ref[...] = _dot(s, w_ref[...], HIGHEST) + b_ref[...]


def ada_modulation(c8, w_ada, b_ada, tn=1024):
    m, d = c8.shape
    n = w_ada.shape[1]
    return pl.pallas_call(
        _mod_kernel,
        grid=(n // tn,),
        in_specs=[pl.BlockSpec((m, d), lambda j: (0, 0)),
                  pl.BlockSpec((d, tn), lambda j: (0, j)),
                  pl.BlockSpec((1, tn), lambda j: (0, j))],
        out_specs=pl.BlockSpec((m, tn), lambda j: (0, j)),
        out_shape=jax.ShapeDtypeStruct((m, n), F32),
        compiler_params=_params("arbitrary"),
        name="ada_mod",
    )(c8, w_ada, b_ada.reshape(1, n))


def _norm_mod_kernel(x_ref, mod_ref, gain_ref, o_ref, *, shift_row, scale_row):
    x = x_ref[...]
    xn = x * lax.rsqrt(jnp.mean(x * x, axis=-1, keepdims=True) + NORM_EPS)
    h = xn * gain_ref[...] * (1.0 + mod_ref[scale_row:scale_row + 1, :]) + mod_ref[shift_row:shift_row + 1, :]
    o_ref[...] = h.astype(o_ref.dtype)


def norm_mod(x, mod, gain, shift_row, scale_row, tm=1024):
    b, s, d = x.shape
    tm = min(tm, s)
    return pl.pallas_call(
        functools.partial(_norm_mod_kernel, shift_row=shift_row, scale_row=scale_row),
        grid=(b, s // tm),
        in_specs=[pl.BlockSpec((None, tm, d), lambda bi, i: (bi, i, 0)),
                  pl.BlockSpec((None, 8, d), lambda bi, i: (bi, 0, 0)),
                  pl.BlockSpec((1, d), lambda bi, i: (0, 0))],
        out_specs=pl.BlockSpec((None, tm, d), lambda bi, i: (bi, i, 0)),
        out_shape=jax.ShapeDtypeStruct((b, s, d), BF16),
        compiler_params=_params("parallel", "parallel"),
        name="norm_mod",
    )(x, mod, gain.reshape(1, d))


def _mm_kernel(a_ref, w_ref, o_ref):
    o_ref[...] = _dot(a_ref[...], w_ref[...]).astype(o_ref.dtype)


def _mm_rope_kernel(a_ref, w_ref, cos_ref, sina_ref, sinb_ref, o_ref, *, tn, q_cols):
    acc = _dot(a_ref[...], w_ref[...])
    scale = jnp.where(pl.program_id(2) * tn < q_cols, ATT_HEAD_DIM ** -0.5 * math.log2(math.e), 1.0).astype(F32)
    cos, sina, sinb = cos_ref[...], sina_ref[...], sinb_ref[...]
    for j in range(tn // LANES):
        t = acc[:, j * LANES:(j + 1) * LANES]
        half = ROT_DIM // 2
        r = t * cos + pltpu.roll(t, LANES - half, 1) * sina + pltpu.roll(t, half, 1) * sinb
        o_ref[:, j * LANES:(j + 1) * LANES] = (r * scale).astype(o_ref.dtype)


def matmul(a, w, out_dtype, tm=1024, tn=512, rope=None):
    b, s, k = a.shape
    n = w.shape[1]
    tm = min(tm, s)
    tn = min(tn, n)
    assert s % tm == 0 and n % tn == 0
    in_specs = [pl.BlockSpec((None, tm, k), lambda bi, i, j: (bi, i, 0)),
                pl.BlockSpec((k, tn), lambda bi, i, j: (0, j))]
    args = [a, w]
    if rope is None:
        body = _mm_kernel
        name = "matmul"
    else:
        cos, sina, sinb, q_cols = rope
        body = functools.partial(_mm_rope_kernel, tn=tn, q_cols=q_cols)
        in_specs += [pl.BlockSpec((tm, LANES), lambda bi, i, j: (i, 0))] * 3
        args += [cos, sina, sinb]
        name = "matmul_rope"
    return pl.pallas_call(
        body,
        grid=(b, s // tm, n // tn),
        in_specs=in_specs,
        out_specs=pl.BlockSpec((None, tm, tn), lambda bi, i, j: (bi, i, j)),
        out_shape=jax.ShapeDtypeStruct((b, s, n), out_dtype),
        compiler_params=_params("parallel", "parallel", "arbitrary"),
        name=name,
    )(*args)


def _mm_t_kernel(wt_ref, a_ref, cos_ref, sina_ref, sinb_ref, o_ref):
    acc = _dot_t(wt_ref[...], a_ref[...])
    tn, tm = acc.shape
    is_q = pl.program_id(2) == 0
    cos = jnp.where(is_q, cos_ref[...], 1.0)
    sina = jnp.where(is_q, sina_ref[...], 0.0)
    sinb = jnp.where(is_q, sinb_ref[...], 0.0)
    scale = jnp.where(is_q, ATT_HEAD_DIM ** -0.5 * math.log2(math.e), 1.0).astype(F32)
    half = ROT_DIM // 2
    per_map = lambda x: x.reshape(tn // ATT_HEAD_DIM, ATT_HEAD_DIM, tm)
    r = (per_map(acc) * cos[None] + per_map(pltpu.roll(acc, tn - half, 0)) * sina[None]
         + per_map(pltpu.roll(acc, half, 0)) * sinb[None])
    o_ref[...] = (r.reshape(tn, tm) * scale).astype(o_ref.dtype)


def matmul_t(a, wt, cos_t, sina_t, sinb_t, tm=1024, tn=1024):
    b, s, k = a.shape
    n = wt.shape[0]
    tm = min(tm, s)
    table = pl.BlockSpec((ATT_HEAD_DIM, tm), lambda bi, i, j: (0, i))
    return pl.pallas_call(
        _mm_t_kernel,
        grid=(b, s // tm, n // tn),
        in_specs=[pl.BlockSpec((tn, k), lambda bi, i, j: (j, 0)),
                  pl.BlockSpec((None, tm, k), lambda bi, i, j: (bi, i, 0)),
                  table, table, table],
        out_specs=pl.BlockSpec((None, tn, tm), lambda bi, i, j: (bi, j, i)),
        out_shape=jax.ShapeDtypeStruct((b, n, s), BF16),
        compiler_params=_params("parallel", "parallel", "arbitrary"),
        name="matmul_t",
    )(wt, a, cos_t, sina_t, sinb_t)


def rope_tables(seq):
    half = ROT_DIM // 2
    pos = jnp.arange(seq, dtype=F32)
    inv = ROPE_THETA ** (-jnp.arange(0, ROT_DIM, 2, dtype=F32) / ROT_DIM)
    ang = pos[:, None] * inv[None, :]
    cos, sin = jnp.cos(ang), jnp.sin(ang)
    ones = jnp.ones((seq, ATT_HEAD_DIM - ROT_DIM), F32)
    zeros = jnp.zeros((seq, ATT_HEAD_DIM - ROT_DIM), F32)
    zh = jnp.zeros((seq, half), F32)
    cos64 = jnp.concatenate([cos, cos, ones], axis=1)
    sina64 = jnp.concatenate([-sin, zh, zeros], axis=1)
    sinb64 = jnp.concatenate([zh, sin, zeros], axis=1)
    rep = LANES // ATT_HEAD_DIM
    by_lane = (jnp.tile(cos64, (1, rep)), jnp.tile(sina64, (1, rep)), jnp.tile(sinb64, (1, rep)))
    by_row = (cos64.T, sina64.T, sinb64.T)
    return by_lane, by_row


SUBLANES = 8
ATT_GROUP = 256


def _sublane_all(x, op):
    for shift in (4, 2, 1):
        x = op(x, pltpu.roll(x, shift, 0))
    return x


def _attn_kernel(qt_ref, k_ref, vt_ref, lq1_ref, lk1_ref, lq2_ref, lk2_ref, gain_ref, o_ref,
                 acc_s, q2_s, sa_s, sb_s, pa_s, pb_s, *, tq, tk):
    qi = pl.program_id(2)
    qt = qt_ref[...]
    row = lax.broadcasted_iota(jnp.int32, qt.shape, 0)
    zero = jnp.zeros_like(qt)
    q2 = jnp.concatenate([jnp.where(row < ATT_HEAD_DIM, qt, zero),
                          jnp.where(row >= ATT_HEAD_DIM, qt, zero)], axis=1)
    n2 = 2 * tq
    acc_s[...] = jnp.zeros(acc_s.shape, F32)

    def update(k0, nk, c0, nc, m_all, l_all, masked):
        cols = slice(c0, c0 + nc)
        kb = k_ref[pl.ds(k0, nk), :]
        vtb = vt_ref[:, pl.ds(k0, nk)]
        s = _dot(kb, q2[:, cols])
        if masked:
            kpos = k0 + lax.broadcasted_iota(jnp.int32, (nk, nc), 0)
            col = c0 + lax.broadcasted_iota(jnp.int32, (nk, nc), 1)
            qpos = qi * tq + jnp.where(col >= tq, col - tq, col)
            s = jnp.where(kpos <= qpos, s, -jnp.inf)
        s3 = s.reshape(nk // SUBLANES, SUBLANES, nc)
        m_old = m_all[:, cols]
        m_new = jnp.maximum(m_old, _sublane_all(jnp.max(s3, axis=0), jnp.maximum))
        p3 = jnp.exp2(s3 - m_new[None])
        corr = jnp.exp2(m_old - m_new)
        l_new = l_all[:, cols] * corr + jnp.sum(p3, axis=0)
        pv = _dot(vtb, p3.reshape(nk, nc).astype(BF16))
        acc3 = acc_s[:, cols].reshape(LANES // SUBLANES, SUBLANES, nc)
        acc_s[:, cols] = (acc3 * corr[None]).reshape(LANES, nc) + pv
        if nc == n2:
            return m_new, l_new
        put = lambda full, part: jnp.concatenate(
            ([full[:, :c0]] if c0 else []) + [part] + ([full[:, c0 + nc:]] if c0 + nc < n2 else []), axis=1)
        return put(m_all, m_new), put(l_all, l_new)

    q2_s[...] = q2
    groups = n2 // ATT_GROUP

    def weighted_values(t, p_prv, corr_prv, gs):
        pv = _dot(vt_ref[:, pl.ds(pl.multiple_of(t * tk, tk), tk)], p_prv[:, gs])
        acc3 = acc_s[:, gs].reshape(LANES // SUBLANES, SUBLANES, gs.stop - gs.start)
        acc_s[:, gs] = (acc3 * corr_prv[:, gs][None]).reshape(LANES, gs.stop - gs.start) + pv

    def step(t, carry, s_cur, p_cur, s_nxt, p_prv):
        m_old, l_old, corr_prv = carry
        t_prv = jnp.maximum(t - 1, 0)
        kn = pl.multiple_of((t + 1) * tk, tk)
        m_parts, l_parts, c_parts = [], [], []
        for g in range(groups):
            gs = slice(g * ATT_GROUP, (g + 1) * ATT_GROUP)
            weighted_values(t_prv, p_prv, corr_prv, gs)
            s3 = s_cur[:, gs].reshape(tk // SUBLANES, SUBLANES, ATT_GROUP)
            m_o = m_old[:, gs]
            m_n = jnp.maximum(m_o, _sublane_all(jnp.max(s3, axis=0), jnp.maximum))
            p3 = jnp.exp2(s3 - m_n[None])
            corr = jnp.exp2(m_o - m_n)
            p_cur[:, gs] = p3.reshape(tk, ATT_GROUP).astype(BF16)
            m_parts.append(m_n)
            l_parts.append(l_old[:, gs] * corr + jnp.sum(p3, axis=0))
            c_parts.append(corr)
            s_nxt[:, gs] = _dot(k_ref[pl.ds(kn, tk), :], q2_s[:, gs])
        cat = lambda parts: jnp.concatenate(parts, axis=1)
        return cat(m_parts), cat(l_parts), cat(c_parts)

    n_full = (qi * tq) // tk
    pb_s[...] = jnp.zeros(pb_s.shape, BF16)
    sa_s[...] = _dot(k_ref[pl.ds(0, tk), :], q2)
    carry = (jnp.full((SUBLANES, n2), -jnp.inf, F32), jnp.zeros((SUBLANES, n2), F32),
             jnp.ones((SUBLANES, n2), F32))

    def pair(i, carry):
        carry = step(2 * i, carry, sa_s, pa_s, sb_s, pb_s)
        return step(2 * i + 1, carry, sb_s, pb_s, sa_s, pa_s)

    m, l, corr = lax.fori_loop(0, n_full // 2, pair, carry)
    for g in range(groups):
        weighted_values(jnp.maximum(n_full - 1, 0), pb_s, corr, slice(g * ATT_GROUP, (g + 1) * ATT_GROUP))

    half = tq // 2
    kd = pl.multiple_of(qi * tq, tq)
    m, l = update(kd, half, 0, n2, m, l, True)
    m, l = update(kd + half, half, half, half, m, l, True)
    m, l = update(kd + half, half, tq + half, half, m, l, True)

    lam = (jnp.exp(jnp.sum(lq1_ref[...] * lk1_ref[...], axis=-1, keepdims=True))
           - jnp.exp(jnp.sum(lq2_ref[...] * lk2_ref[...], axis=-1, keepdims=True))
           + LAMBDA_INIT)
    l = _sublane_all(l, jnp.add)
    o3 = acc_s[...].reshape(LANES // SUBLANES, SUBLANES, n2) / l[None]
    d3 = o3[:, :, :tq] - lam * o3[:, :, tq:]
    ms = _sublane_all(jnp.sum(d3 * d3, axis=0), jnp.add) * (1.0 / LANES)
    d3 = d3 * lax.rsqrt(ms + SUBLN_EPS)[None]
    o = d3.reshape(LANES, tq).T
    o_ref[...] = (o * gain_ref[...] * (1.0 - LAMBDA_INIT)).astype(o_ref.dtype)


def diff_attention(qvt, kk, lq1, lk1, lq2, lk2, subln_gain, tq=1024):
    b, h2, _, s = qvt.shape
    h = h2 // 2
    tq = min(tq, s)
    tk = tq // 2
    vec = lambda a: a.reshape(1, -1).astype(F32)
    small = lambda n: pl.BlockSpec((1, n), lambda bi, hi, i: (0, 0))
    return pl.pallas_call(
        functools.partial(_attn_kernel, tq=tq, tk=tk),
        grid=(b, h, s // tq),
        in_specs=[pl.BlockSpec((None, None, LANES, tq), lambda bi, hi, i: (bi, hi, 0, i)),
                  pl.BlockSpec((None, s, LANES), lambda bi, hi, i: (bi, 0, hi)),
                  pl.BlockSpec((None, None, LANES, s), lambda bi, hi, i: (bi, h + hi, 0, 0)),
                  small(ATT_HEAD_DIM), small(ATT_HEAD_DIM), small(ATT_HEAD_DIM), small(ATT_HEAD_DIM),
                  small(LANES)],
        out_specs=pl.BlockSpec((None, tq, LANES), lambda bi, hi, i: (bi, i, hi)),
        out_shape=jax.ShapeDtypeStruct((b, s, h * LANES), BF16),
        scratch_shapes=[pltpu.VMEM((LANES, 2 * tq), F32), pltpu.VMEM((LANES, 2 * tq), BF16),
                        pltpu.VMEM((tk, 2 * tq), F32), pltpu.VMEM((tk, 2 * tq), F32),
                        pltpu.VMEM((tk, 2 * tq), BF16), pltpu.VMEM((tk, 2 * tq), BF16)],
        compiler_params=_params("parallel", "parallel", "arbitrary"),
        name="diff_attention",
    )(qvt, kk, qvt, vec(lq1), vec(lk1), vec(lq2), vec(lk2), vec(subln_gain))


def _bdot(a, b):
    return jnp.dot(a.astype(BF16), b.astype(BF16), preferred_element_type=F32)


def _dot3(a, b):
    a_hi = a.astype(BF16)
    a_lo = (a - a_hi.astype(F32)).astype(BF16)
    b_hi = b.astype(BF16)
    b_lo = (b - b_hi.astype(F32)).astype(BF16)
    d = functools.partial(jnp.dot, preferred_element_type=F32)
    return d(a_hi, b_hi) + (d(a_hi, b_lo) + d(a_lo, b_hi))


def _each(f, *lists):
    return [f(*xs) for xs in zip(*lists)]


def _rwkv_pairs(pairs, states, c):
    n = RWKV_HEAD_DIM
    c2 = 2 * c
    row = lax.broadcasted_iota(jnp.int32, (c2, c2), 0)
    col = lax.broadcasted_iota(jnp.int32, (c2, c2), 1)
    lower = row >= col
    strict = row > col
    eye2 = (row == col).astype(F32)
    rc = lax.broadcasted_iota(jnp.int32, (c, c), 0)
    cc = lax.broadcasted_iota(jnp.int32, (c, c), 1)
    lower_b = (rc >= cc).astype(BF16)
    li = lax.broadcasted_iota(jnp.int32, (LANES, LANES), 0)
    lj = lax.broadcasted_iota(jnp.int32, (LANES, LANES), 1)
    same_head = ((li // n) == (lj // n)).astype(BF16)
    eye_l = (li == lj).astype(F32)
    head1 = lax.broadcasted_iota(jnp.int32, (c, LANES), 1) >= n

    def stack(x):
        return jnp.concatenate([jnp.where(head1, 0.0, x), jnp.where(head1, x, 0.0)], axis=0)

    def head_sum(x):
        return jnp.dot(x.astype(BF16), same_head, preferred_element_type=F32)

    r, k, v, lw, ai, g, kk_w, ka_w, rk_w, lnw, lnb = [list(x) for x in zip(*pairs)]

    kk = _each(lambda k, w: k * w, k, kk_w)
    kk = _each(lambda kk: kk * lax.rsqrt(jnp.maximum(head_sum(kk * kk), 1e-24)), kk)
    k2 = _each(lambda k, ai, w: k * (1.0 + (ai - 1.0) * w), k, ai, ka_w)
    b = _each(lambda kk, ai: kk * ai, kk, ai)
    bonus = _each(lambda r, k2, w, v: head_sum(r * k2 * w) * v, r, k2, rk_w, v)

    def cumsum(lw):
        hi = lw.astype(BF16)
        r1 = lw - hi.astype(F32)
        mid = r1.astype(BF16)
        lo = (r1 - mid.astype(F32)).astype(BF16)
        c3 = jnp.dot(lower_b, jnp.concatenate([hi, mid, lo], axis=1), preferred_element_type=F32)
        return c3[:, :LANES] + c3[:, LANES:2 * LANES] + c3[:, 2 * LANES:]

    cum = _each(cumsum, lw)
    e_cum = _each(jnp.exp, cum)
    e_neg = _each(lambda cm: jnp.exp(-cm), cum)
    e_tail = _each(lambda cm: jnp.exp(cm[c - 1:c, :] - cm), cum)
    at = _each(lambda kk, cm, l: stack(-kk * jnp.exp(cm - l)), kk, cum, lw)
    rt = _each(lambda r, e: stack(r * e), r, e_cum)
    vs = _each(stack, v)
    qq = _each(lambda x, y: jnp.concatenate([x, y], axis=0).astype(BF16), at, rt)
    kq = _each(lambda b, k2, e: jnp.concatenate([stack(b * e), stack(k2 * e)], axis=0).astype(BF16),
               b, k2, e_neg)
    s = _each(_dot_t, qq, kq)
    a_ab = _each(lambda s: jnp.where(strict, s[:c2, :c2], 0.0), s)
    a_ak = _each(lambda s: jnp.where(strict, s[:c2, c2:], 0.0), s)
    a_rb = _each(lambda s: jnp.where(lower, s[c2:, :c2], 0.0), s)
    a_rk = _each(lambda s: jnp.where(lower, s[c2:, c2:], 0.0), s)

    def level(m):
        return ((row // (2 * m)) == (col // (2 * m))) & ((row // m) % 2 == 1) & ((col // m) % 2 == 0)

    t = _each(lambda x: eye2 + jnp.where(level(1), x, 0.0), a_ab)
    m = 2
    while m < c:
        sel = level(m)
        tn = _each(lambda t, x: _bdot(t, jnp.where(sel, x, 0.0)), t, a_ab)
        t = _each(lambda t, x: t + _bdot(x, t), t, tn)
        m *= 2

    x1 = _each(_bdot, a_ak, vs)
    wu = _each(lambda t, at, x1: _bdot(t, jnp.concatenate([at, x1], axis=1)), t, at, x1)
    bh_t = _each(lambda b, e: stack(b * e).T, b, e_tail)
    kh_t = _each(lambda k2, e: stack(k2 * e).T, k2, e_tail)
    z1 = _each(lambda a, bt, wu: _bdot(jnp.concatenate([a, bt], axis=0), wu), a_rb, bh_t, wu)
    z2 = _each(lambda a, kt, v: _bdot(jnp.concatenate([a, kt], axis=0), v), a_rk, kh_t, vs)
    pm = _each(lambda z1, rt, e: jnp.concatenate([rt + z1[:c2, :LANES],
                                                  eye_l * e[c - 1:c, :] + z1[c2:, :LANES]], axis=0),
               z1, rt, e_cum)
    yn0 = _each(lambda z1, z2: z1[:, LANES:] + z2, z1, z2)
    chunks = len(pairs) // len(states)
    h_next = list(states)
    yh = [None] * len(pairs)
    for j in range(chunks):
        for blk in range(len(states)):
            i = blk * chunks + j
            yh[i] = _bdot(pm[i], h_next[blk]) + yn0[i]
            h_next[blk] = yh[i][c2:]

    y = _each(lambda yh: yh[:c] + yh[c:c2], yh)
    d = _each(lambda y: y - head_sum(y) * (1.0 / n), y)
    var = _each(lambda d: head_sum(d * d) * (1.0 / n), d)
    outs = _each(lambda d, var, lnw, lnb, bonus, g: (d * lax.rsqrt(var + GN_EPS) * lnw + lnb + bonus) * g,
                 d, var, lnw, lnb, bonus, g)
    return outs, h_next


def _rwkv_kernel(rkv_ref, lora_ref, mu_rkv_ref, mu_lora_ref, w0_ref, wl_ref, a0_ref, al_ref, gl_ref,
                 kk_ref, ka_ref, rk_ref, lnw_ref, lnb_ref, o_ref,
                 state_s, prev_rkv_s, prev_lora_s, r_s, k_s, v_s, lw_s, ai_s, g_s, *, c):
    ci = pl.program_id(1)
    width = RWKV_HEADS * RWKV_HEAD_DIM

    @pl.when(ci == 0)
    def _():
        state_s[...] = jnp.zeros(state_s.shape, F32)
        prev_rkv_s[...] = jnp.zeros(prev_rkv_s.shape, F32)
        prev_lora_s[...] = jnp.zeros(prev_lora_s.shape, F32)

    def shift_mix(p, prev_s, mu):
        rolled = pltpu.roll(p, 1, 0)
        first = lax.broadcasted_iota(jnp.int32, p.shape, 0) == 0
        shifted = jnp.where(first, prev_s[7:8, :], rolled)
        prev_s[...] = p[p.shape[0] - 8:, :]
        return p + mu * (shifted - p)

    z = shift_mix(rkv_ref[...], prev_rkv_s, mu_rkv_ref[...])
    zl = shift_mix(lora_ref[...], prev_lora_s, mu_lora_ref[...])
    r_s[...] = z[:, :width]
    k_s[...] = z[:, width:2 * width]
    v_s[...] = z[:, 2 * width:]
    xw = zl[:, :DECAY_LORA]
    xa = zl[:, DECAY_LORA:DECAY_LORA + ICLR_LORA]
    xg = zl[:, DECAY_LORA + ICLR_LORA:DECAY_LORA + ICLR_LORA + GATE_LORA]
    w_logit = -jax.nn.softplus(-(w0_ref[...] + _dot3(jnp.tanh(xw), wl_ref[...]))) - 0.5
    lw_s[...] = -jnp.exp(w_logit)
    ai_s[...] = jax.nn.sigmoid(a0_ref[...] + _dot3(xa, al_ref[...]))
    g_s[...] = _dot3(jax.nn.sigmoid(xg), gl_ref[...])

    def group_body(gi, carry):
        slices = [pl.ds(pl.multiple_of((gi * RWKV_GROUP_BLOCKS + blk) * LANES, LANES), LANES)
                  for blk in range(RWKV_GROUP_BLOCKS)]
        chunk_rows = [pl.ds(j * c, c) for j in range(r_s.shape[0] // c)]
        pairs = [tuple(ref[rows, sl] for ref in (r_s, k_s, v_s, lw_s, ai_s, g_s))
                 + tuple(ref[:, sl] for ref in (kk_ref, ka_ref, rk_ref, lnw_ref, lnb_ref))
                 for sl in slices for rows in chunk_rows]
        states = [state_s[gi * RWKV_GROUP_BLOCKS + blk] for blk in range(RWKV_GROUP_BLOCKS)]
        outs, h_next = _rwkv_pairs(pairs, states, c)
        for blk, sl in enumerate(slices):
            for j, rows in enumerate(chunk_rows):
                o_ref[rows, sl] = outs[blk * len(chunk_rows) + j].astype(o_ref.dtype)
            state_s[gi * RWKV_GROUP_BLOCKS + blk] = h_next[blk]
        return carry

    lax.fori_loop(0, width // (LANES * RWKV_GROUP_BLOCKS), group_body, 0)


def rwkv7(rkv, lora, mu_rkv, mu_lora, w0, wl, a0, al, gl, k_k, k_a, r_k, lnx_w, lnx_b):
    b, s, _ = rkv.shape
    c = min(RWKV_CHUNK, s)
    rows = min(RWKV_STEP_CHUNKS * c, s)
    width = RWKV_HEADS * RWKV_HEAD_DIM
    row = lambda a: a.reshape(1, -1).astype(F32)
    const = lambda shape: pl.BlockSpec(shape, lambda bi, ci: (0,) * len(shape))
    return pl.pallas_call(
        functools.partial(_rwkv_kernel, c=c),
        grid=(b, s // rows),
        in_specs=[pl.BlockSpec((None, rows, 3 * width), lambda bi, ci: (bi, ci, 0)),
                  pl.BlockSpec((None, rows, LORA_PAD), lambda bi, ci: (bi, ci, 0)),
                  const((1, 3 * width)), const((1, LORA_PAD)),
                  const((1, width)), const((DECAY_LORA, width)),
                  const((1, width)), const((ICLR_LORA, width)), const((GATE_LORA, width)),
                  const((1, width)), const((1, width)), const((1, width)),
                  const((1, width)), const((1, width))],
        out_specs=pl.BlockSpec((None, rows, width), lambda bi, ci: (bi, ci, 0)),
        out_shape=jax.ShapeDtypeStruct((b, s, width), BF16),
        scratch_shapes=[pltpu.VMEM((width // LANES, LANES, LANES), F32),
                        pltpu.VMEM((8, 3 * width), F32), pltpu.VMEM((8, LORA_PAD), F32)]
                       + [pltpu.VMEM((rows, width), F32)] * 6,
        compiler_params=_params("parallel", "arbitrary"),
        name="rwkv7",
    )(rkv, lora, row(mu_rkv), row(mu_lora), row(w0), wl.astype(F32), row(a0), al.astype(F32),
      gl.astype(F32), row(k_k), row(k_a), row(r_k), row(lnx_w), row(lnx_b))


MERGE_CHUNK = 256


def _merge_kernel(ya_ref, yr_ref, g0_ref, g1_ref, wa_ref, wr_ref, o_ref):
    ya, yr = ya_ref[...], yr_ref[...]
    for c0 in range(0, o_ref.shape[1], MERGE_CHUNK):
        cs = slice(c0, c0 + MERGE_CHUNK)
        za = _dot(ya, wa_ref[:, cs])
        zr = _dot(yr, wr_ref[:, cs])
        ga = jax.nn.sigmoid(g0_ref[:, cs].astype(F32))
        gr = jax.nn.sigmoid(g1_ref[:, cs].astype(F32))
        o_ref[:, cs] = (ga * za + gr * zr).astype(o_ref.dtype)


def gated_merge(y_att, y_rwkv, gates, wa, wr, tm=1024, tn=1024):
    b, s, kdim = y_att.shape
    n = wa.shape[1]
    tm = min(tm, s)
    nb = n // tn
    return pl.pallas_call(
        _merge_kernel,
        grid=(b, s // tm, nb),
        in_specs=[pl.BlockSpec((None, tm, kdim), lambda bi, i, j: (bi, i, 0)),
                  pl.BlockSpec((None, tm, kdim), lambda bi, i, j: (bi, i, 0)),
                  pl.BlockSpec((None, tm, tn), lambda bi, i, j: (bi, i, j)),
                  pl.BlockSpec((None, tm, tn), lambda bi, i, j: (bi, i, nb + j)),
                  pl.BlockSpec((kdim, tn), lambda bi, i, j: (0, j)),
                  pl.BlockSpec((kdim, tn), lambda bi, i, j: (0, j))],
        out_specs=pl.BlockSpec((None, tm, tn), lambda bi, i, j: (bi, i, j)),
        out_shape=jax.ShapeDtypeStruct((b, s, n), BF16),
        compiler_params=_params("parallel", "parallel", "arbitrary"),
        name="gated_merge",
    )(y_att, y_rwkv, gates, gates, wa, wr)


def _out_proj_kernel(m_ref, w_ref, x_ref, mod_ref, gain_ref, x1_ref, h2_ref):
    x1 = x_ref[...] + mod_ref[2:3, :] * _dot(m_ref[...], w_ref[...])
    x1_ref[...] = x1
    xn = x1 * lax.rsqrt(jnp.mean(x1 * x1, axis=-1, keepdims=True) + NORM_EPS)
    h2_ref[...] = (xn * gain_ref[...] * (1.0 + mod_ref[4:5, :]) + mod_ref[3:4, :]).astype(h2_ref.dtype)


def out_proj(merged, w_out, x, mod, gain2, tm=512):
    b, s, d = x.shape
    tm = min(tm, s)
    return pl.pallas_call(
        _out_proj_kernel,
        grid=(b, s // tm),
        in_specs=[pl.BlockSpec((None, tm, d), lambda bi, i: (bi, i, 0)),
                  pl.BlockSpec((d, d), lambda bi, i: (0, 0), pipeline_mode=pl.Buffered(1)),
                  pl.BlockSpec((None, tm, d), lambda bi, i: (bi, i, 0)),
                  pl.BlockSpec((None, 8, d), lambda bi, i: (bi, 0, 0)),
                  pl.BlockSpec((1, d), lambda bi, i: (0, 0))],
        out_specs=[pl.BlockSpec((None, tm, d), lambda bi, i: (bi, i, 0)),
                   pl.BlockSpec((None, tm, d), lambda bi, i: (bi, i, 0))],
        out_shape=[jax.ShapeDtypeStruct((b, s, d), F32), jax.ShapeDtypeStruct((b, s, d), BF16)],
        compiler_params=_params("parallel", "parallel"),
        name="out_proj",
    )(merged, w_out, x, mod, gain2.reshape(1, d))


GLU_HALO = 16
GLU_CHUNK = 256


def _glu_up_kernel(h_ref, halo_ref, wg_ref, wv_ref, cw_ref, cb_ref, o_ref, hext_s):
    i = pl.program_id(1)

    @pl.when(pl.program_id(2) == 0)
    def _():
        hext_s[:GLU_HALO, :] = jnp.where(i > 0, halo_ref[...], jnp.zeros_like(halo_ref[...]))
        hext_s[GLU_HALO:, :] = h_ref[...]

    hext = hext_s[...]
    chunks = [slice(c0, c0 + GLU_CHUNK) for c0 in range(0, o_ref.shape[1], GLU_CHUNK)]

    def gate(ug, cs):
        cw = cw_ref[:, cs]
        conv = (cb_ref[:, cs] + pltpu.roll(ug, 2, 0)[GLU_HALO:] * cw[0:1, :]
                + pltpu.roll(ug, 1, 0)[GLU_HALO:] * cw[1:2, :] + ug[GLU_HALO:] * cw[2:3, :])
        return 0.5 * conv * (1.0 + lax.erf(conv * (2.0 ** -0.5)))

    ug = _dot(hext, wg_ref[:, chunks[0]])
    gates = []
    for cs, nxt in zip(chunks, chunks[1:] + [None]):
        ug_next = _dot(hext, wg_ref[:, nxt]) if nxt is not None else None
        gates.append(gate(ug, cs))
        ug = ug_next
    for cs, g in zip(chunks, gates):
        o_ref[:, cs] = (g * _dot(hext[GLU_HALO:], wv_ref[:, cs])).astype(o_ref.dtype)


def glu_up(h, w_up, conv_w, conv_b, tm=1024, tf=512):
    b, s, d = h.shape
    ff = w_up.shape[1] // 2
    tm = min(tm, s)
    nf = ff // tf
    cw8 = jnp.zeros((8, ff), F32).at[:conv_w.shape[0]].set(conv_w)
    hb = tm // GLU_HALO
    return pl.pallas_call(
        _glu_up_kernel,
        grid=(b, s // tm, nf),
        in_specs=[pl.BlockSpec((None, tm, d), lambda bi, i, f: (bi, i, 0)),
                  pl.BlockSpec((None, GLU_HALO, d), lambda bi, i, f: (bi, jnp.maximum(i * hb - 1, 0), 0)),
                  pl.BlockSpec((d, tf), lambda bi, i, f: (0, f)),
                  pl.BlockSpec((d, tf), lambda bi, i, f: (0, nf + f)),
                  pl.BlockSpec((8, tf), lambda bi, i, f: (0, f)),
                  pl.BlockSpec((1, tf), lambda bi, i, f: (0, f))],
        out_specs=pl.BlockSpec((None, tm, tf), lambda bi, i, f: (bi, i, f)),
        out_shape=jax.ShapeDtypeStruct((b, s, ff), BF16),
        scratch_shapes=[pltpu.VMEM((GLU_HALO + tm, d), BF16)],
        compiler_params=_params("parallel", "parallel", "arbitrary"),
        name="glu_up",
    )(h, h, w_up, w_up, cw8, conv_b.reshape(1, ff))


def _down_kernel(a_ref, wd_ref, x1_ref, mod_ref, gain_ref, o_ref):
    x2 = x1_ref[...] + mod_ref[5:6, :] * _dot(a_ref[...], wd_ref[...])
    o_ref[...] = x2 * lax.rsqrt(jnp.mean(x2 * x2, axis=-1, keepdims=True) + NORM_EPS) * gain_ref[...]


def down_proj(act, w_down, x1, mod, final_gain, tm=256):
    b, s, d = x1.shape
    ff = w_down.shape[0]
    tm = min(tm, s)
    return pl.pallas_call(
        _down_kernel,
        grid=(b, s // tm),
        in_specs=[pl.BlockSpec((None, tm, ff), lambda bi, i: (bi, i, 0)),
                  pl.BlockSpec((ff, d), lambda bi, i: (0, 0), pipeline_mode=pl.Buffered(1)),
                  pl.BlockSpec((None, tm, d), lambda bi, i: (bi, i, 0)),
                  pl.BlockSpec((None, 8, d), lambda bi, i: (bi, 0, 0)),
                  pl.BlockSpec((1, d), lambda bi, i: (0, 0))],
        out_specs=pl.BlockSpec((None, tm, d), lambda bi, i: (bi, i, 0)),
        out_shape=jax.ShapeDtypeStruct((b, s, d), F32),
        compiler_params=_params("parallel", "parallel"),
        name="down_proj",
    )(act, w_down, x1, mod, final_gain.reshape(1, d))


def kernel(x, c, w_ada, b_ada, norm1_gain, w_in, lambda_q1, lambda_k1, lambda_q2, lambda_k2, subln_gain, mu_shift, w0, w_lora_up, a0, a_lora_up, g_lora_up, k_k, k_a, r_k, lnx_w, lnx_b, w_branch, w_out, norm2_gain, w_up, conv_w, conv_b, w_down, final_gain):
    b, s, d = x.shape
    depth = w_in.shape[0]
    assert depth == 1
    l = 0
    qk_cols = 2 * ATT_HEADS * 2 * ATT_HEAD_DIM
    v_cols = ATT_HEADS * 2 * ATT_HEAD_DIM
    width = RWKV_HEADS * RWKV_HEAD_DIM
    n_lora = DECAY_LORA + ICLR_LORA + GATE_LORA
    rkv0 = qk_cols + v_cols
    lora0 = rkv0 + 3 * width
    gate0 = lora0 + n_lora

    c8 = jnp.zeros((8, d), F32).at[:b].set(c)
    mod = ada_modulation(c8, w_ada[l], b_ada[l])[:b]
    mod = jnp.pad(mod.reshape(b, 6, d), ((0, 0), (0, 2), (0, 0)))

    h = norm_mod(x, mod, norm1_gain[l], shift_row=0, scale_row=1)

    w_in_l = w_in[l]
    q_cols = qk_cols // 2
    w_qv_t = jnp.concatenate([w_in_l[:, :q_cols], w_in_l[:, qk_cols:rkv0]], axis=1).T.astype(BF16)
    w_k = w_in_l[:, q_cols:qk_cols].astype(BF16)
    w_gate = w_in_l[:, gate0:].astype(BF16)
    w_rkv = w_in_l[:, rkv0:lora0].astype(BF16)
    w_lora = jnp.pad(w_in_l[:, lora0:gate0], ((0, 0), (0, LORA_PAD - n_lora))).astype(BF16)

    (cos, sina, sinb), rope_rows = rope_tables(s)
    qv_t = matmul_t(h, w_qv_t, *rope_rows)
    kk = matmul(h, w_k, BF16, tn=1024, rope=(cos, sina, sinb, 0))
    gates = matmul(h, w_gate, BF16, tm=2048, tn=1024)
    rkv = matmul(h, w_rkv, F32, tn=1024)
    lora = matmul(h, w_lora, F32, tm=2048, tn=LORA_PAD)

    y_att = diff_attention(qv_t.reshape(b, 2 * ATT_HEADS, LANES, s), kk,
                           lambda_q1[l], lambda_k1[l], lambda_q2[l], lambda_k2[l], subln_gain[l])

    mu = mu_shift[l]
    mu_lora = jnp.pad(mu[3 * width:], (0, LORA_PAD - n_lora))
    y_rwkv = rwkv7(rkv, lora, mu[:3 * width], mu_lora, w0[l], w_lora_up[l], a0[l], a_lora_up[l],
                   g_lora_up[l], k_k[l], k_a[l], r_k[l].reshape(-1), lnx_w[l], lnx_b[l])

    merged = gated_merge(y_att, y_rwkv, gates, w_branch[l, 0].astype(BF16), w_branch[l, 1].astype(BF16))
    x1, h2 = out_proj(merged, w_out[l].astype(BF16), x, mod, norm2_gain[l])

    act = glu_up(h2, w_up[l].astype(BF16), conv_w[l], conv_b[l])
    return down_proj(act, w_down[l].astype(BF16), x1, mod, final_gain)
```

```python
import functools
import math

import jax
import jax.numpy as jnp
from jax import lax
from jax.experimental import pallas as pl
from jax.experimental.pallas import tpu as pltpu

F32 = jnp.float32
BF16 = jnp.bfloat16
HIGHEST = lax.Precision.HIGHEST

ATT_HEADS = 8
ATT_HEAD_DIM = 64
ROT_DIM = ATT_HEAD_DIM // 4
ROPE_THETA = 500000.0
SUBLN_EPS = 1e-5
RWKV_HEADS = 16
RWKV_HEAD_DIM = 64
DECAY_LORA = 64
ICLR_LORA = 64
GATE_LORA = 160
GN_EPS = 64e-5
NORM_EPS = 1e-6
LAMBDA_INIT = 0.8 - 0.6 * math.exp(-0.3 * 0)

LANES = 128
VMEM_LIMIT = 48 * 1024 * 1024
RWKV_CHUNK = 64
RWKV_STEP_CHUNKS = 2
RWKV_GROUP_BLOCKS = 8
LORA_PAD = 384


def _params(*sem):
    return pltpu.CompilerParams(dimension_semantics=sem, vmem_limit_bytes=VMEM_LIMIT)


def _dot(a, b, precision=None):
    return jnp.dot(a, b, preferred_element_type=F32, precision=precision)


def _dot_t(a, b, precision=None):
    return lax.dot_general(a, b, (((1,), (1,)), ((), ())),
                           preferred_element_type=F32, precision=precision)


def _mod_kernel(c_ref, w_ref, b_ref, o_ref):
    c = c_ref[...]
    s = c * jax.nn.sigmoid(c)
    o_ref[...] = _dot(s, w_ref[...], HIGHEST) + b_ref[...]


def ada_modulation(c8, w_ada, b_ada, tn=1024):
    m, d = c8.shape
    n = w_ada.shape[1]
    return pl.pallas_call(
        _mod_kernel,
        grid=(n // tn,),
        in_specs=[pl.BlockSpec((m, d), lambda j: (0, 0)),
                  pl.BlockSpec((d, tn), lambda j: (0, j)),
                  pl.BlockSpec((1, tn), lambda j: (0, j))],
        out_specs=pl.BlockSpec((m, tn), lambda j: (0, j)),
        out_shape=jax.ShapeDtypeStruct((m, n), F32),
        compiler_params=_params("arbitrary"),
        name="ada_mod",
    )(c8, w_ada, b_ada.reshape(1, n))


def _norm_mod_kernel(x_ref, mod_ref, gain_ref, o_ref, *, shift_row, scale_row):
    x = x_ref[...]
    xn = x * lax.rsqrt(jnp.mean(x * x, axis=-1, keepdims=True) + NORM_EPS)
    h = xn * gain_ref[...] * (1.0 + mod_ref[scale_row:scale_row + 1, :]) + mod_ref[shift_row:shift_row + 1, :]
    o_ref[...] = h.astype(o_ref.dtype)


def norm_mod(x, mod, gain, shift_row, scale_row, tm=1024):
    b, s, d = x.shape
    tm = min(tm, s)
    return pl.pallas_call(
        functools.partial(_norm_mod_kernel, shift_row=shift_row, scale_row=scale_row),
        grid=(b, s // tm),
        in_specs=[pl.BlockSpec((None, tm, d), lambda bi, i: (bi, i, 0)),
                  pl.BlockSpec((None, 8, d), lambda bi, i: (bi, 0, 0)),
                  pl.BlockSpec((1, d), lambda bi, i: (0, 0))],
        out_specs=pl.BlockSpec((None, tm, d), lambda bi, i: (bi, i, 0)),
        out_shape=jax.ShapeDtypeStruct((b, s, d), BF16),
        compiler_params=_params("parallel", "parallel"),
        name="norm_mod",
    )(x, mod, gain.reshape(1, d))


def _mm_kernel(a_ref, w_ref, o_ref):
    o_ref[...] = _dot(a_ref[...], w_ref[...]).astype(o_ref.dtype)


def _mm_rope_kernel(a_ref, w_ref, cos_ref, sina_ref, sinb_ref, o_ref, *, tn):
    acc = _dot(a_ref[...], w_ref[...])
    cos, sina, sinb = cos_ref[...], sina_ref[...], sinb_ref[...]
    for j in range(tn // LANES):
        t = acc[:, j * LANES:(j + 1) * LANES]
        half = ROT_DIM // 2
        r = t * cos + pltpu.roll(t, LANES - half, 1) * sina + pltpu.roll(t, half, 1) * sinb
        o_ref[:, j * LANES:(j + 1) * LANES] = r.astype(o_ref.dtype)


def matmul(a, w, out_dtype, tm=1024, tn=512, rope=None):
    b, s, k = a.shape
    n = w.shape[1]
    tm = min(tm, s)
    tn = min(tn, n)
    assert s % tm == 0 and n % tn == 0
    in_specs = [pl.BlockSpec((None, tm, k), lambda bi, i, j: (bi, i, 0)),
                pl.BlockSpec((k, tn), lambda bi, i, j: (0, j))]
    args = [a, w]
    if rope is None:
        body = _mm_kernel
        name = "matmul"
    else:
        cos, sina, sinb = rope
        body = functools.partial(_mm_rope_kernel, tn=tn)
        in_specs += [pl.BlockSpec((tm, LANES), lambda bi, i, j: (i, 0))] * 3
        args += [cos, sina, sinb]
        name = "matmul_rope"
    return pl.pallas_call(
        body,
        grid=(b, s // tm, n // tn),
        in_specs=in_specs,
        out_specs=pl.BlockSpec((None, tm, tn), lambda bi, i, j: (bi, i, j)),
        out_shape=jax.ShapeDtypeStruct((b, s, n), out_dtype),
        compiler_params=_params("parallel", "parallel", "arbitrary"),
        name=name,
    )(*args)


def _mm_t_kernel(wt_ref, a_ref, cos_ref, sina_ref, sinb_ref, o_ref):
    acc = _dot_t(wt_ref[...], a_ref[...])
    tn, tm = acc.shape
    is_q = pl.program_id(2) == 0
    cos = jnp.where(is_q, cos_ref[...], 1.0)
    sina = jnp.where(is_q, sina_ref[...], 0.0)
    sinb = jnp.where(is_q, sinb_ref[...], 0.0)
    scale = jnp.where(is_q, ATT_HEAD_DIM ** -0.5 * math.log2(math.e), 1.0).astype(F32)
    half = ROT_DIM // 2
    per_map = lambda x: x.reshape(tn // ATT_HEAD_DIM, ATT_HEAD_DIM, tm)
    r = (per_map(acc) * cos[None] + per_map(pltpu.roll(acc, tn - half, 0)) * sina[None]
         + per_map(pltpu.roll(acc, half, 0)) * sinb[None])
    o_ref[...] = (r.reshape(tn, tm) * scale).astype(o_ref.dtype)


def matmul_t(a, wt, cos_t, sina_t, sinb_t, tm=1024, tn=1024):
    b, s, k = a.shape
    n = wt.shape[0]
    tm = min(tm, s)
    table = pl.BlockSpec((ATT_HEAD_DIM, tm), lambda bi, i, j: (0, i))
    return pl.pallas_call(
        _mm_t_kernel,
        grid=(b, s // tm, n // tn),
        in_specs=[pl.BlockSpec((tn, k), lambda bi, i, j: (j, 0)),
                  pl.BlockSpec((None, tm, k), lambda bi, i, j: (bi, i, 0)),
                  table, table, table],
        out_specs=pl.BlockSpec((None, tn, tm), lambda bi, i, j: (bi, j, i)),
        out_shape=jax.ShapeDtypeStruct((b, n, s), BF16),
        compiler_params=_params("parallel", "parallel", "arbitrary"),
        name="matmul_t",
    )(wt, a, cos_t, sina_t, sinb_t)


def rope_tables(seq):
    half = ROT_DIM // 2
    pos = jnp.arange(seq, dtype=F32)
    inv = ROPE_THETA ** (-jnp.arange(0, ROT_DIM, 2, dtype=F32) / ROT_DIM)
    ang = pos[:, None] * inv[None, :]
    cos, sin = jnp.cos(ang), jnp.sin(ang)
    ones = jnp.ones((seq, ATT_HEAD_DIM - ROT_DIM), F32)
    zeros = jnp.zeros((seq, ATT_HEAD_DIM - ROT_DIM), F32)
    zh = jnp.zeros((seq, half), F32)
    cos64 = jnp.concatenate([cos, cos, ones], axis=1)
    sina64 = jnp.concatenate([-sin, zh, zeros], axis=1)
    sinb64 = jnp.concatenate([zh, sin, zeros], axis=1)
    rep = LANES // ATT_HEAD_DIM
    by_lane = (jnp.tile(cos64, (1, rep)), jnp.tile(sina64, (1, rep)), jnp.tile(sinb64, (1, rep)))
    by_row = (cos64.T, sina64.T, sinb64.T)
    return by_lane, by_row


SUBLANES = 8
ATT_ONES_ROWS = 16
ATT_GROUP = 256


def _sublane_all(x, op):
    for shift in (4, 2, 1):
        x = op(x, pltpu.roll(x, shift, 0))
    return x


def _attn_kernel(qt_ref, k_ref, vt_ref, lq1_ref, lk1_ref, lq2_ref, lk2_ref, gain_ref, o_ref,
                 acc_s, q2_s, sa_s, sb_s, pa_s, pb_s, *, tq, tk):
    qi = pl.program_id(2)
    qt = qt_ref[...]
    row = lax.broadcasted_iota(jnp.int32, qt.shape, 0)
    zero = jnp.zeros_like(qt)
    q2 = jnp.concatenate([jnp.where(row < ATT_HEAD_DIM, qt, zero),
                          jnp.where(row >= ATT_HEAD_DIM, qt, zero)], axis=1)
    n2 = 2 * tq
    rows = LANES + ATT_ONES_ROWS
    acc_s[...] = jnp.zeros(acc_s.shape, F32)
    ones_rows = jnp.ones((ATT_ONES_ROWS, tk), BF16)

    def weighted(k0, p):
        return _dot(jnp.concatenate([vt_ref[:, pl.ds(k0, tk)], ones_rows], axis=0), p)

    def update(k0, c0, nc, m_all):
        cols = slice(c0, c0 + nc)
        s = _dot(k_ref[pl.ds(k0, tk), :], q2_s[:, cols])
        kpos = k0 + lax.broadcasted_iota(jnp.int32, (tk, nc), 0)
        col = c0 + lax.broadcasted_iota(jnp.int32, (tk, nc), 1)
        qpos = qi * tq + jnp.where(col >= tq, col - tq, col)
        s = jnp.where(kpos <= qpos, s, -jnp.inf)
        s3 = s.reshape(tk // SUBLANES, SUBLANES, nc)
        m_old = m_all[:, cols]
        m_new = jnp.maximum(m_old, _sublane_all(jnp.max(s3, axis=0), jnp.maximum))
        p = jnp.exp2((s3 - m_new[None]).reshape(tk, nc).astype(BF16))
        corr = jnp.exp2(m_old - m_new)
        acc3 = acc_s[:, cols].reshape(rows // SUBLANES, SUBLANES, nc)
        acc_s[:, cols] = (acc3 * corr[None]).reshape(rows, nc) + weighted(k0, p)
        if nc == n2:
            return m_new
        return jnp.concatenate(([m_all[:, :c0]] if c0 else []) + [m_new]
                               + ([m_all[:, c0 + nc:]] if c0 + nc < n2 else []), axis=1)

    q2_s[...] = q2
    groups = n2 // ATT_GROUP

    def weighted_values(t, p_prv, corr_prv, gs):
        pv = weighted(pl.multiple_of(t * tk, tk), p_prv[:, gs])
        acc3 = acc_s[:, gs].reshape(rows // SUBLANES, SUBLANES, gs.stop - gs.start)
        acc_s[:, gs] = (acc3 * corr_prv[:, gs][None]).reshape(rows, gs.stop - gs.start) + pv

    def step(t, carry, s_cur, p_cur, s_nxt, p_prv):
        m_old, corr_prv = carry
        t_prv = jnp.maximum(t - 1, 0)
        kn = pl.multiple_of((t + 1) * tk, tk)
        m_parts, c_parts = [], []
        for g in range(groups):
            gs = slice(g * ATT_GROUP, (g + 1) * ATT_GROUP)
            weighted_values(t_prv, p_prv, corr_prv, gs)
            s_nxt[:, gs] = _dot(k_ref[pl.ds(kn, tk), :], q2_s[:, gs])
            s3 = s_cur[:, gs].reshape(tk // SUBLANES, SUBLANES, ATT_GROUP)
            m_o = m_old[:, gs]
            m_n = jnp.maximum(m_o, _sublane_all(jnp.max(s3, axis=0), jnp.maximum))
            p_cur[:, gs] = jnp.exp2((s3 - m_n[None]).reshape(tk, ATT_GROUP).astype(BF16))
            m_parts.append(m_n)
            c_parts.append(jnp.exp2(m_o - m_n))
        cat = lambda parts: jnp.concatenate(parts, axis=1)
        return cat(m_parts), cat(c_parts)

    n_full = (qi * tq) // tk
    pb_s[...] = jnp.zeros(pb_s.shape, BF16)
    sa_s[...] = _dot(k_ref[pl.ds(0, tk), :], q2)
    carry = (jnp.full((SUBLANES, n2), -jnp.inf, F32), jnp.ones((SUBLANES, n2), F32))

    def pair(i, carry):
        carry = step(2 * i, carry, sa_s, pa_s, sb_s, pb_s)
        return step(2 * i + 1, carry, sb_s, pb_s, sa_s, pa_s)

    m, corr = lax.fori_loop(0, n_full // 2, pair, carry)
    for g in range(groups):
        weighted_values(jnp.maximum(n_full - 1, 0), pb_s, corr, slice(g * ATT_GROUP, (g + 1) * ATT_GROUP))

    kd = pl.multiple_of(qi * tq, tq)
    m = update(kd, 0, n2, m)
    m = update(kd + tk, tk, tk, m)
    update(kd + tk, tq + tk, tk, m)

    lam = (jnp.exp(jnp.sum(lq1_ref[...] * lk1_ref[...], axis=-1, keepdims=True))
           - jnp.exp(jnp.sum(lq2_ref[...] * lk2_ref[...], axis=-1, keepdims=True))
           + LAMBDA_INIT)
    o = acc_s[:LANES, :] / acc_s[LANES:LANES + 1, :]
    d3 = (o[:, :tq] - lam * o[:, tq:]).reshape(LANES // SUBLANES, SUBLANES, tq)
    ms = _sublane_all(jnp.sum(d3 * d3, axis=0), jnp.add) * (1.0 / LANES)
    d3 = d3 * lax.rsqrt(ms + SUBLN_EPS)[None]
    o = d3.reshape(LANES, tq).T
    o_ref[...] = (o * gain_ref[...] * (1.0 - LAMBDA_INIT)).astype(o_ref.dtype)


def diff_attention(qvt, kk, lq1, lk1, lq2, lk2, subln_gain, tq=1024):
    b, h2, _, s = qvt.shape
    h = h2 // 2
    tq = min(tq, s)
    tk = tq // 2
    vec = lambda a: a.reshape(1, -1).astype(F32)
    small = lambda n: pl.BlockSpec((1, n), lambda bi, hi, i: (0, 0))
    return pl.pallas_call(
        functools.partial(_attn_kernel, tq=tq, tk=tk),
        grid=(b, h, s // tq),
        in_specs=[pl.BlockSpec((None, None, LANES, tq), lambda bi, hi, i: (bi, hi, 0, i)),
                  pl.BlockSpec((None, s, LANES), lambda bi, hi, i: (bi, 0, hi)),
                  pl.BlockSpec((None, None, LANES, s), lambda bi, hi, i: (bi, h + hi, 0, 0)),
                  small(ATT_HEAD_DIM), small(ATT_HEAD_DIM), small(ATT_HEAD_DIM), small(ATT_HEAD_DIM),
                  small(LANES)],
        out_specs=pl.BlockSpec((None, tq, LANES), lambda bi, hi, i: (bi, i, hi)),
        out_shape=jax.ShapeDtypeStruct((b, s, h * LANES), BF16),
        scratch_shapes=[pltpu.VMEM((LANES + ATT_ONES_ROWS, 2 * tq), F32), pltpu.VMEM((LANES, 2 * tq), BF16),
                        pltpu.VMEM((tk, 2 * tq), F32), pltpu.VMEM((tk, 2 * tq), F32),
                        pltpu.VMEM((tk, 2 * tq), BF16), pltpu.VMEM((tk, 2 * tq), BF16)],
        compiler_params=_params("parallel", "parallel", "arbitrary"),
        name="diff_attention",
    )(qvt, kk, qvt, vec(lq1), vec(lk1), vec(lq2), vec(lk2), vec(subln_gain))


def _bdot(a, b):
    return jnp.dot(a.astype(BF16), b.astype(BF16), preferred_element_type=F32)


def _split_bf16(w):
    hi = w.astype(BF16)
    return jnp.stack([hi, (w - hi.astype(F32)).astype(BF16)])


def _dot3(a, b_ref):
    a_hi = a.astype(BF16)
    a_lo = (a - a_hi.astype(F32)).astype(BF16)
    d = functools.partial(jnp.dot, preferred_element_type=F32)
    return d(a_hi, b_ref[0]) + (d(a_hi, b_ref[1]) + d(a_lo, b_ref[0]))


def _each(f, *lists):
    return [f(*xs) for xs in zip(*lists)]


def _rwkv_pairs(pairs, states, c):
    n = RWKV_HEAD_DIM
    c2 = 2 * c
    row = lax.broadcasted_iota(jnp.int32, (c2, c2), 0)
    col = lax.broadcasted_iota(jnp.int32, (c2, c2), 1)
    lower = row >= col
    strict = row > col
    eye2 = (row == col).astype(F32)
    rc = lax.broadcasted_iota(jnp.int32, (c, c), 0)
    cc = lax.broadcasted_iota(jnp.int32, (c, c), 1)
    lower_b = (rc >= cc).astype(BF16)
    li = lax.broadcasted_iota(jnp.int32, (LANES, LANES), 0)
    lj = lax.broadcasted_iota(jnp.int32, (LANES, LANES), 1)
    same_head = ((li // n) == (lj // n)).astype(BF16)
    eye_l = (li == lj).astype(F32)
    head1 = lax.broadcasted_iota(jnp.int32, (c, LANES), 1) >= n

    def stack(x):
        return jnp.concatenate([jnp.where(head1, 0.0, x), jnp.where(head1, x, 0.0)], axis=0)

    def head_sum(x):
        return jnp.dot(x.astype(BF16), same_head, preferred_element_type=F32)

    r, k, v, lw, ai, g, kk_w, ka_w, rk_w, lnw, lnb = [list(x) for x in zip(*pairs)]

    kk = _each(lambda k, w: k * w, k, kk_w)
    kk = _each(lambda kk: kk * lax.rsqrt(jnp.maximum(head_sum(kk * kk), 1e-24)), kk)
    k2 = _each(lambda k, ai, w: k * (1.0 + (ai - 1.0) * w), k, ai, ka_w)
    b = _each(lambda kk, ai: kk * ai, kk, ai)
    bonus = _each(lambda r, k2, w, v: head_sum(r * k2 * w) * v, r, k2, rk_w, v)

    def cumsum(lw):
        hi = lw.astype(BF16)
        r1 = lw - hi.astype(F32)
        mid = r1.astype(BF16)
        lo = (r1 - mid.astype(F32)).astype(BF16)
        c3 = jnp.dot(lower_b, jnp.concatenate([hi, mid, lo], axis=1), preferred_element_type=F32)
        return c3[:, :LANES] + c3[:, LANES:2 * LANES] + c3[:, 2 * LANES:]

    cum = _each(cumsum, lw)
    e_cum = _each(jnp.exp, cum)
    e_neg = _each(lambda cm: jnp.exp(-cm), cum)
    e_tail = _each(lambda cm: jnp.exp(cm[c - 1:c, :] - cm), cum)
    at = _each(lambda kk, cm, l: stack(-kk * jnp.exp(cm - l)), kk, cum, lw)
    rt = _each(lambda r, e: stack(r * e), r, e_cum)
    vs = _each(stack, v)
    qq = _each(lambda x, y: jnp.concatenate([x, y], axis=0).astype(BF16), at, rt)
    kq = _each(lambda b, k2, e: jnp.concatenate([stack(b * e), stack(k2 * e)], axis=0).astype(BF16),
               b, k2, e_neg)
    s = _each(_dot_t, qq, kq)
    a_ab = _each(lambda s: jnp.where(strict, s[:c2, :c2], 0.0), s)
    a_ak = _each(lambda s: jnp.where(strict, s[:c2, c2:], 0.0), s)
    a_rb = _each(lambda s: jnp.where(lower, s[c2:, :c2], 0.0), s)
    a_rk = _each(lambda s: jnp.where(lower, s[c2:, c2:], 0.0), s)

    def level(m):
        return ((row // (2 * m)) == (col // (2 * m))) & ((row // m) % 2 == 1) & ((col // m) % 2 == 0)

    t = _each(lambda x: eye2 + jnp.where(level(1), x, 0.0), a_ab)
    m = 2
    while m < c:
        sel = level(m)
        tn = _each(lambda t, x: _bdot(t, jnp.where(sel, x, 0.0)), t, a_ab)
        t = _each(lambda t, x: t + _bdot(x, t), t, tn)
        m *= 2

    x1 = _each(_bdot, a_ak, vs)
    wu = _each(lambda t, at, x1: _bdot(t, jnp.concatenate([at, x1], axis=1)), t, at, x1)
    bh_t = _each(lambda b, e: stack(b * e).T, b, e_tail)
    kh_t = _each(lambda k2, e: stack(k2 * e).T, k2, e_tail)
    z1 = _each(lambda a, bt, wu: _bdot(jnp.concatenate([a, bt], axis=0), wu), a_rb, bh_t, wu)
    z2 = _each(lambda a, kt, v: _bdot(jnp.concatenate([a, kt], axis=0), v), a_rk, kh_t, vs)
    pm = _each(lambda z1, rt, e: jnp.concatenate([rt + z1[:c2, :LANES],
                                                  eye_l * e[c - 1:c, :] + z1[c2:, :LANES]], axis=0),
               z1, rt, e_cum)
    yn0 = _each(lambda z1, z2: z1[:, LANES:] + z2, z1, z2)
    chunks = len(pairs) // len(states)
    h_next = list(states)
    yh = [None] * len(pairs)
    for j in range(chunks):
        for blk in range(len(states)):
            i = blk * chunks + j
            yh[i] = _bdot(pm[i], h_next[blk]) + yn0[i]
            h_next[blk] = yh[i][c2:]

    y = _each(lambda yh: yh[:c] + yh[c:c2], yh)
    d = _each(lambda y: y - head_sum(y) * (1.0 / n), y)
    var = _each(lambda d: head_sum(d * d) * (1.0 / n), d)
    outs = _each(lambda d, var, lnw, lnb, bonus, g: (d * lax.rsqrt(var + GN_EPS) * lnw + lnb + bonus) * g,
                 d, var, lnw, lnb, bonus, g)
    return outs, h_next


def _rwkv_kernel(rkv_ref, lora_ref, mu_rkv_ref, mu_lora_ref, w0_ref, wl_ref, a0_ref, al_ref, gl_ref,
                 kk_ref, ka_ref, rk_ref, lnw_ref, lnb_ref, o_ref,
                 state_s, prev_rkv_s, prev_lora_s, r_s, k_s, v_s, lw_s, ai_s, g_s, *, c):
    ci = pl.program_id(1)
    width = RWKV_HEADS * RWKV_HEAD_DIM

    @pl.when(ci == 0)
    def _():
        state_s[...] = jnp.zeros(state_s.shape, F32)
        prev_rkv_s[...] = jnp.zeros(prev_rkv_s.shape, F32)
        prev_lora_s[...] = jnp.zeros(prev_lora_s.shape, F32)

    def shift_mix(p, prev_s, mu):
        rolled = pltpu.roll(p, 1, 0)
        first = lax.broadcasted_iota(jnp.int32, p.shape, 0) == 0
        shifted = jnp.where(first, prev_s[7:8, :], rolled)
        prev_s[...] = p[p.shape[0] - 8:, :]
        return p + mu * (shifted - p)

    z = shift_mix(rkv_ref[...], prev_rkv_s, mu_rkv_ref[...])
    zl = shift_mix(lora_ref[...], prev_lora_s, mu_lora_ref[...])
    r_s[...] = z[:, :width]
    k_s[...] = z[:, width:2 * width]
    v_s[...] = z[:, 2 * width:]
    xw = zl[:, :DECAY_LORA]
    xa = zl[:, DECAY_LORA:DECAY_LORA + ICLR_LORA]
    xg = zl[:, DECAY_LORA + ICLR_LORA:DECAY_LORA + ICLR_LORA + GATE_LORA]
    lw_s[...] = -math.exp(-0.5) * jax.nn.sigmoid(w0_ref[...] + _dot3(jnp.tanh(xw), wl_ref))
    ai_s[...] = jax.nn.sigmoid(a0_ref[...] + _bdot(xa, al_ref[...]))
    g_s[...] = _bdot(jax.nn.sigmoid(xg), gl_ref[...])

    def group_body(gi, carry):
        slices = [pl.ds(pl.multiple_of((gi * RWKV_GROUP_BLOCKS + blk) * LANES, LANES), LANES)
                  for blk in range(RWKV_GROUP_BLOCKS)]
        chunk_rows = [pl.ds(j * c, c) for j in range(r_s.shape[0] // c)]
        pairs = [tuple(ref[rows, sl] for ref in (r_s, k_s, v_s, lw_s, ai_s, g_s))
                 + tuple(ref[:, sl] for ref in (kk_ref, ka_ref, rk_ref, lnw_ref, lnb_ref))
                 for sl in slices for rows in chunk_rows]
        states = [state_s[gi * RWKV_GROUP_BLOCKS + blk] for blk in range(RWKV_GROUP_BLOCKS)]
        outs, h_next = _rwkv_pairs(pairs, states, c)
        for blk, sl in enumerate(slices):
            for j, rows in enumerate(chunk_rows):
                o_ref[rows, sl] = outs[blk * len(chunk_rows) + j].astype(o_ref.dtype)
            state_s[gi * RWKV_GROUP_BLOCKS + blk] = h_next[blk]
        return carry

    lax.fori_loop(0, width // (LANES * RWKV_GROUP_BLOCKS), group_body, 0)


def rwkv7(rkv, lora, mu_rkv, mu_lora, w0, wl, a0, al, gl, k_k, k_a, r_k, lnx_w, lnx_b):
    b, s, _ = rkv.shape
    c = min(RWKV_CHUNK, s)
    rows = min(RWKV_STEP_CHUNKS * c, s)
    width = RWKV_HEADS * RWKV_HEAD_DIM
    row = lambda a: a.reshape(1, -1).astype(F32)
    const = lambda shape: pl.BlockSpec(shape, lambda bi, ci: (0,) * len(shape))
    return pl.pallas_call(
        functools.partial(_rwkv_kernel, c=c),
        grid=(b, s // rows),
        in_specs=[pl.BlockSpec((None, rows, 3 * width), lambda bi, ci: (bi, ci, 0)),
                  pl.BlockSpec((None, rows, LORA_PAD), lambda bi, ci: (bi, ci, 0)),
                  const((1, 3 * width)), const((1, LORA_PAD)),
                  const((1, width)), const((2, DECAY_LORA, width)),
                  const((1, width)), const((ICLR_LORA, width)), const((GATE_LORA, width)),
                  const((1, width)), const((1, width)), const((1, width)),
                  const((1, width)), const((1, width))],
        out_specs=pl.BlockSpec((None, rows, width), lambda bi, ci: (bi, ci, 0)),
        out_shape=jax.ShapeDtypeStruct((b, s, width), BF16),
        scratch_shapes=[pltpu.VMEM((width // LANES, LANES, LANES), F32),
                        pltpu.VMEM((8, 3 * width), F32), pltpu.VMEM((8, LORA_PAD), F32)]
                       + [pltpu.VMEM((rows, width), F32)] * 6,
        compiler_params=_params("parallel", "arbitrary"),
        name="rwkv7",
    )(rkv, lora, row(mu_rkv), row(mu_lora), row(w0), _split_bf16(wl.astype(F32)), row(a0), al.astype(BF16),
      gl.astype(BF16), row(k_k), row(k_a), row(r_k), row(lnx_w), row(lnx_b))


MERGE_CHUNK = 256


def _merge_kernel(ya_ref, yr_ref, g0_ref, g1_ref, wa_ref, wr_ref, o_ref):
    ya, yr = ya_ref[...], yr_ref[...]
    for c0 in range(0, o_ref.shape[1], MERGE_CHUNK):
        cs = slice(c0, c0 + MERGE_CHUNK)
        za = _dot(ya, wa_ref[:, cs])
        zr = _dot(yr, wr_ref[:, cs])
        ga = jax.nn.sigmoid(g0_ref[:, cs].astype(F32))
        gr = jax.nn.sigmoid(g1_ref[:, cs].astype(F32))
        o_ref[:, cs] = (ga * za + gr * zr).astype(o_ref.dtype)


def gated_merge(y_att, y_rwkv, gates, wa, wr, tm=1024, tn=1024):
    b, s, kdim = y_att.shape
    n = wa.shape[1]
    tm = min(tm, s)
    nb = n // tn
    return pl.pallas_call(
        _merge_kernel,
        grid=(b, s // tm, nb),
        in_specs=[pl.BlockSpec((None, tm, kdim), lambda bi, i, j: (bi, i, 0)),
                  pl.BlockSpec((None, tm, kdim), lambda bi, i, j: (bi, i, 0)),
                  pl.BlockSpec((None, tm, tn), lambda bi, i, j: (bi, i, j)),
                  pl.BlockSpec((None, tm, tn), lambda bi, i, j: (bi, i, nb + j)),
                  pl.BlockSpec((kdim, tn), lambda bi, i, j: (0, j)),
                  pl.BlockSpec((kdim, tn), lambda bi, i, j: (0, j))],
        out_specs=pl.BlockSpec((None, tm, tn), lambda bi, i, j: (bi, i, j)),
        out_shape=jax.ShapeDtypeStruct((b, s, n), BF16),
        compiler_params=_params("parallel", "parallel", "arbitrary"),
        name="gated_merge",
    )(y_att, y_rwkv, gates, gates, wa, wr)


def _out_proj_kernel(m_ref, w_ref, x_ref, mod_ref, gain_ref, x1_ref, h2_ref):
    x1 = x_ref[...] + mod_ref[2:3, :] * _dot(m_ref[...], w_ref[...])
    x1_ref[...] = x1
    xn = x1 * lax.rsqrt(jnp.mean(x1 * x1, axis=-1, keepdims=True) + NORM_EPS)
    h2_ref[...] = (xn * gain_ref[...] * (1.0 + mod_ref[4:5, :]) + mod_ref[3:4, :]).astype(h2_ref.dtype)


def out_proj(merged, w_out, x, mod, gain2, tm=512):
    b, s, d = x.shape
    tm = min(tm, s)
    return pl.pallas_call(
        _out_proj_kernel,
        grid=(b, s // tm),
        in_specs=[pl.BlockSpec((None, tm, d), lambda bi, i: (bi, i, 0)),
                  pl.BlockSpec((d, d), lambda bi, i: (0, 0), pipeline_mode=pl.Buffered(1)),
                  pl.BlockSpec((None, tm, d), lambda bi, i: (bi, i, 0)),
                  pl.BlockSpec((None, 8, d), lambda bi, i: (bi, 0, 0)),
                  pl.BlockSpec((1, d), lambda bi, i: (0, 0))],
        out_specs=[pl.BlockSpec((None, tm, d), lambda bi, i: (bi, i, 0)),
                   pl.BlockSpec((None, tm, d), lambda bi, i: (bi, i, 0))],
        out_shape=[jax.ShapeDtypeStruct((b, s, d), F32), jax.ShapeDtypeStruct((b, s, d), BF16)],
        compiler_params=_params("parallel", "parallel"),
        name="out_proj",
    )(merged, w_out, x, mod, gain2.reshape(1, d))


GLU_HALO = 16
GLU_CHUNK = 256


def _glu_up_kernel(h_ref, halo_ref, wg_ref, wv_ref, cw_ref, cb_ref, o_ref, hext_s):
    i = pl.program_id(1)

    @pl.when(pl.program_id(2) == 0)
    def _():
        hext_s[:GLU_HALO, :] = jnp.where(i > 0, halo_ref[...], jnp.zeros_like(halo_ref[...]))
        hext_s[GLU_HALO:, :] = h_ref[...]

    hext = hext_s[...]
    chunks = [slice(c0, c0 + GLU_CHUNK) for c0 in range(0, o_ref.shape[1], GLU_CHUNK)]

    def gate(ug, cs):
        cw = cw_ref[:, cs]
        conv = (cb_ref[:, cs] + pltpu.roll(ug, 2, 0)[GLU_HALO:] * cw[0:1, :]
                + pltpu.roll(ug, 1, 0)[GLU_HALO:] * cw[1:2, :] + ug[GLU_HALO:] * cw[2:3, :])
        return 0.5 * conv * (1.0 + lax.erf(conv * (2.0 ** -0.5)))

    ug = _dot(hext, wg_ref[:, chunks[0]])
    gates = []
    for cs, nxt in zip(chunks, chunks[1:] + [None]):
        ug_next = _dot(hext, wg_ref[:, nxt]) if nxt is not None else None
        gates.append(gate(ug, cs))
        ug = ug_next
    for cs, g in zip(chunks, gates):
        o_ref[:, cs] = (g * _dot(hext[GLU_HALO:], wv_ref[:, cs])).astype(o_ref.dtype)


def glu_up(h, w_up, conv_w, conv_b, tm=1024, tf=512):
    b, s, d = h.shape
    ff = w_up.shape[1] // 2
    tm = min(tm, s)
    nf = ff // tf
    cw8 = jnp.zeros((8, ff), F32).at[:conv_w.shape[0]].set(conv_w)
    hb = tm // GLU_HALO
    return pl.pallas_call(
        _glu_up_kernel,
        grid=(b, s // tm, nf),
        in_specs=[pl.BlockSpec((None, tm, d), lambda bi, i, f: (bi, i, 0)),
                  pl.BlockSpec((None, GLU_HALO, d), lambda bi, i, f: (bi, jnp.maximum(i * hb - 1, 0), 0)),
                  pl.BlockSpec((d, tf), lambda bi, i, f: (0, f)),
                  pl.BlockSpec((d, tf), lambda bi, i, f: (0, nf + f)),
                  pl.BlockSpec((8, tf), lambda bi, i, f: (0, f)),
                  pl.BlockSpec((1, tf), lambda bi, i, f: (0, f))],
        out_specs=pl.BlockSpec((None, tm, tf), lambda bi, i, f: (bi, i, f)),
        out_shape=jax.ShapeDtypeStruct((b, s, ff), BF16),
        scratch_shapes=[pltpu.VMEM((GLU_HALO + tm, d), BF16)],
        compiler_params=_params("parallel", "parallel", "arbitrary"),
        name="glu_up",
    )(h, h, w_up, w_up, cw8, conv_b.reshape(1, ff))


def _down_kernel(a_ref, wd_ref, x1_ref, mod_ref, gain_ref, o_ref):
    x2 = x1_ref[...] + mod_ref[5:6, :] * _dot(a_ref[...], wd_ref[...])
    o_ref[...] = x2 * lax.rsqrt(jnp.mean(x2 * x2, axis=-1, keepdims=True) + NORM_EPS) * gain_ref[...]


def down_proj(act, w_down, x1, mod, final_gain, tm=256):
    b, s, d = x1.shape
    ff = w_down.shape[0]
    tm = min(tm, s)
    return pl.pallas_call(
        _down_kernel,
        grid=(b, s // tm),
        in_specs=[pl.BlockSpec((None, tm, ff), lambda bi, i: (bi, i, 0)),
                  pl.BlockSpec((ff, d), lambda bi, i: (0, 0), pipeline_mode=pl.Buffered(1)),
                  pl.BlockSpec((None, tm, d), lambda bi, i: (bi, i, 0)),
                  pl.BlockSpec((None, 8, d), lambda bi, i: (bi, 0, 0)),
                  pl.BlockSpec((1, d), lambda bi, i: (0, 0))],
        out_specs=pl.BlockSpec((None, tm, d), lambda bi, i: (bi, i, 0)),
        out_shape=jax.ShapeDtypeStruct((b, s, d), F32),
        compiler_params=_params("parallel", "parallel"),
        name="down_proj",
    )(act, w_down, x1, mod, final_gain.reshape(1, d))


def kernel(x, c, w_ada, b_ada, norm1_gain, w_in, lambda_q1, lambda_k1, lambda_q2, lambda_k2, subln_gain, mu_shift, w0, w_lora_up, a0, a_lora_up, g_lora_up, k_k, k_a, r_k, lnx_w, lnx_b, w_branch, w_out, norm2_gain, w_up, conv_w, conv_b, w_down, final_gain):
    b, s, d = x.shape
    depth = w_in.shape[0]
    assert depth == 1
    l = 0
    qk_cols = 2 * ATT_HEADS * 2 * ATT_HEAD_DIM
    v_cols = ATT_HEADS * 2 * ATT_HEAD_DIM
    width = RWKV_HEADS * RWKV_HEAD_DIM
    n_lora = DECAY_LORA + ICLR_LORA + GATE_LORA
    rkv0 = qk_cols + v_cols
    lora0 = rkv0 + 3 * width
    gate0 = lora0 + n_lora

    c8 = jnp.zeros((8, d), F32).at[:b].set(c)
    mod = ada_modulation(c8, w_ada[l], b_ada[l])[:b]
    mod = jnp.pad(mod.reshape(b, 6, d), ((0, 0), (0, 2), (0, 0)))

    h = norm_mod(x, mod, norm1_gain[l], shift_row=0, scale_row=1)

    w_in_l = w_in[l]
    q_cols = qk_cols // 2
    w_qv_t = jnp.concatenate([w_in_l[:, :q_cols], w_in_l[:, qk_cols:rkv0]], axis=1).T.astype(BF16)
    w_k = w_in_l[:, q_cols:qk_cols].astype(BF16)
    w_gate = w_in_l[:, gate0:].astype(BF16)
    w_rkv = w_in_l[:, rkv0:lora0].astype(BF16)
    w_lora = jnp.pad(w_in_l[:, lora0:gate0], ((0, 0), (0, LORA_PAD - n_lora))).astype(BF16)

    rope_lanes, rope_rows = rope_tables(s)
    qv_t = matmul_t(h, w_qv_t, *rope_rows)
    kk = matmul(h, w_k, BF16, tn=1024, rope=rope_lanes)
    gates = matmul(h, w_gate, BF16, tm=2048, tn=1024)
    rkv = matmul(h, w_rkv, F32, tn=1024)
    lora = matmul(h, w_lora, F32, tm=2048, tn=LORA_PAD)

    y_att = diff_attention(qv_t.reshape(b, 2 * ATT_HEADS, LANES, s), kk,
                           lambda_q1[l], lambda_k1[l], lambda_q2[l], lambda_k2[l], subln_gain[l])

    mu = mu_shift[l]
    mu_lora = jnp.pad(mu[3 * width:], (0, LORA_PAD - n_lora))
    y_rwkv = rwkv7(rkv, lora, mu[:3 * width], mu_lora, w0[l], w_lora_up[l], a0[l], a_lora_up[l],
                   g_lora_up[l], k_k[l], k_a[l], r_k[l].reshape(-1), lnx_w[l], lnx_b[l])

    merged = gated_merge(y_att, y_rwkv, gates, w_branch[l, 0].astype(BF16), w_branch[l, 1].astype(BF16))
    x1, h2 = out_proj(merged, w_out[l].astype(BF16), x, mod, norm2_gain[l])

    act = glu_up(h2, w_up[l].astype(BF16), conv_w[l], conv_b[l])
    return down_proj(act, w_down[l].astype(BF16), x1, mod, final_gain)
```

```python
import functools
import math

import jax
import jax.numpy as jnp
from jax import lax
from jax.experimental import pallas as pl
from jax.experimental.pallas import tpu as pltpu

F32 = jnp.float32
BF16 = jnp.bfloat16
HIGHEST = lax.Precision.HIGHEST

ATT_HEADS = 8
ATT_HEAD_DIM = 64
ROT_DIM = ATT_HEAD_DIM // 4
ROPE_THETA = 500000.0
SUBLN_EPS = 1e-5
RWKV_HEADS = 16
RWKV_HEAD_DIM = 64
DECAY_LORA = 64
ICLR_LORA = 64
GATE_LORA = 160
GN_EPS = 64e-5
NORM_EPS = 1e-6
LAMBDA_INIT = 0.8 - 0.6 * math.exp(-0.3 * 0)

LANES = 128
VMEM_LIMIT = 48 * 1024 * 1024
RWKV_CHUNK = 64
RWKV_STEP_CHUNKS = 2
RWKV_GROUP_BLOCKS = 8
LORA_PAD = 384


def _params(*sem):
    return pltpu.CompilerParams(dimension_semantics=sem, vmem_limit_bytes=VMEM_LIMIT)


def _dot(a, b, precision=None):
    return jnp.dot(a, b, preferred_element_type=F32, precision=precision)


def _dot_t(a, b, precision=None):
    return lax.dot_general(a, b, (((1,), (1,)), ((), ())),
                           preferred_element_type=F32, precision=precision)


def _mod_kernel(c_ref, w_ref, b_ref, o_ref):
    c = c_ref[...]
    s = c * jax.nn.sigmoid(c)
    o_ref[...] = _dot(s, w_ref[...], HIGHEST) + b_ref[...]


def ada_modulation(c8, w_ada, b_ada, tn=1024):
    m, d = c8.shape
    n = w_ada.shape[1]
    return pl.pallas_call(
        _mod_kernel,
        grid=(n // tn,),
        in_specs=[pl.BlockSpec((m, d), lambda j: (0, 0)),
                  pl.BlockSpec((d, tn), lambda j: (0, j)),
                  pl.BlockSpec((1, tn), lambda j: (0, j))],
        out_specs=pl.BlockSpec((m, tn), lambda j: (0, j)),
        out_shape=jax.ShapeDtypeStruct((m, n), F32),
        compiler_params=_params("arbitrary"),
        name="ada_mod",
    )(c8, w_ada, b_ada.reshape(1, n))


def _norm_mod_kernel(x_ref, mod_ref, gain_ref, o_ref, *, shift_row, scale_row):
    x = x_ref[...]
    xn = x * lax.rsqrt(jnp.mean(x * x, axis=-1, keepdims=True) + NORM_EPS)
    h = xn * gain_ref[...] * (1.0 + mod_ref[scale_row:scale_row + 1, :]) + mod_ref[shift_row:shift_row + 1, :]
    o_ref[...] = h.astype(o_ref.dtype)


def norm_mod(x, mod, gain, shift_row, scale_row, tm=1024):
    b, s, d = x.shape
    tm = min(tm, s)
    return pl.pallas_call(
        functools.partial(_norm_mod_kernel, shift_row=shift_row, scale_row=scale_row),
        grid=(b, s // tm),
        in_specs=[pl.BlockSpec((None, tm, d), lambda bi, i: (bi, i, 0)),
                  pl.BlockSpec((None, 8, d), lambda bi, i: (bi, 0, 0)),
                  pl.BlockSpec((1, d), lambda bi, i: (0, 0))],
        out_specs=pl.BlockSpec((None, tm, d), lambda bi, i: (bi, i, 0)),
        out_shape=jax.ShapeDtypeStruct((b, s, d), BF16),
        compiler_params=_params("parallel", "parallel"),
        name="norm_mod",
    )(x, mod, gain.reshape(1, d))


def _mm_kernel(a_ref, w_ref, o_ref):
    o_ref[...] = _dot(a_ref[...], w_ref[...]).astype(o_ref.dtype)


def _mm_rope_kernel(a_ref, w_ref, cos_ref, sina_ref, sinb_ref, o_ref, *, tn):
    acc = _dot(a_ref[...], w_ref[...])
    cos, sina, sinb = cos_ref[...], sina_ref[...], sinb_ref[...]
    for j in range(tn // LANES):
        t = acc[:, j * LANES:(j + 1) * LANES]
        half = ROT_DIM // 2
        r = t * cos + pltpu.roll(t, LANES - half, 1) * sina + pltpu.roll(t, half, 1) * sinb
        o_ref[:, j * LANES:(j + 1) * LANES] = r.astype(o_ref.dtype)


def matmul(a, w, out_dtype, tm=1024, tn=512, rope=None):
    b, s, k = a.shape
    n = w.shape[1]
    tm = min(tm, s)
    tn = min(tn, n)
    assert s % tm == 0 and n % tn == 0
    in_specs = [pl.BlockSpec((None, tm, k), lambda bi, i, j: (bi, i, 0)),
                pl.BlockSpec((k, tn), lambda bi, i, j: (0, j))]
    args = [a, w]
    if rope is None:
        body = _mm_kernel
        name = "matmul"
    else:
        cos, sina, sinb = rope
        body = functools.partial(_mm_rope_kernel, tn=tn)
        in_specs += [pl.BlockSpec((tm, LANES), lambda bi, i, j: (i, 0))] * 3
        args += [cos, sina, sinb]
        name = "matmul_rope"
    return pl.pallas_call(
        body,
        grid=(b, s // tm, n // tn),
        in_specs=in_specs,
        out_specs=pl.BlockSpec((None, tm, tn), lambda bi, i, j: (bi, i, j)),
        out_shape=jax.ShapeDtypeStruct((b, s, n), out_dtype),
        compiler_params=_params("parallel", "parallel", "arbitrary"),
        name=name,
    )(*args)


def _mm_t_kernel(wt_ref, a_ref, cos_ref, sina_ref, sinb_ref, o_ref):
    acc = _dot_t(wt_ref[...], a_ref[...])
    tn, tm = acc.shape
    is_q = pl.program_id(2) == 0
    cos = jnp.where(is_q, cos_ref[...], 1.0)
    sina = jnp.where(is_q, sina_ref[...], 0.0)
    sinb = jnp.where(is_q, sinb_ref[...], 0.0)
    scale = jnp.where(is_q, ATT_HEAD_DIM ** -0.5 * math.log2(math.e), 1.0).astype(F32)
    half = ROT_DIM // 2
    per_map = lambda x: x.reshape(tn // ATT_HEAD_DIM, ATT_HEAD_DIM, tm)
    r = (per_map(acc) * cos[None] + per_map(pltpu.roll(acc, tn - half, 0)) * sina[None]
         + per_map(pltpu.roll(acc, half, 0)) * sinb[None])
    o_ref[...] = (r.reshape(tn, tm) * scale).astype(o_ref.dtype)


def matmul_t(a, wt, cos_t, sina_t, sinb_t, tm=1024, tn=1024):
    b, s, k = a.shape
    n = wt.shape[0]
    tm = min(tm, s)
    table = pl.BlockSpec((ATT_HEAD_DIM, tm), lambda bi, i, j: (0, i))
    return pl.pallas_call(
        _mm_t_kernel,
        grid=(b, s // tm, n // tn),
        in_specs=[pl.BlockSpec((tn, k), lambda bi, i, j: (j, 0)),
                  pl.BlockSpec((None, tm, k), lambda bi, i, j: (bi, i, 0)),
                  table, table, table],
        out_specs=pl.BlockSpec((None, tn, tm), lambda bi, i, j: (bi, j, i)),
        out_shape=jax.ShapeDtypeStruct((b, n, s), BF16),
        compiler_params=_params("parallel", "parallel", "arbitrary"),
        name="matmul_t",
    )(wt, a, cos_t, sina_t, sinb_t)


def rope_tables(seq):
    half = ROT_DIM // 2
    pos = jnp.arange(seq, dtype=F32)
    inv = ROPE_THETA ** (-jnp.arange(0, ROT_DIM, 2, dtype=F32) / ROT_DIM)
    ang = pos[:, None] * inv[None, :]
    cos, sin = jnp.cos(ang), jnp.sin(ang)
    ones = jnp.ones((seq, ATT_HEAD_DIM - ROT_DIM), F32)
    zeros = jnp.zeros((seq, ATT_HEAD_DIM - ROT_DIM), F32)
    zh = jnp.zeros((seq, half), F32)
    cos64 = jnp.concatenate([cos, cos, ones], axis=1)
    sina64 = jnp.concatenate([-sin, zh, zeros], axis=1)
    sinb64 = jnp.concatenate([zh, sin, zeros], axis=1)
    rep = LANES // ATT_HEAD_DIM
    by_lane = (jnp.tile(cos64, (1, rep)), jnp.tile(sina64, (1, rep)), jnp.tile(sinb64, (1, rep)))
    by_row = (cos64.T, sina64.T, sinb64.T)
    return by_lane, by_row


SUBLANES = 8
ATT_GROUP = 256


def _sublane_all(x, op):
    for shift in (4, 2, 1):
        x = op(x, pltpu.roll(x, shift, 0))
    return x


def _attn_kernel(qt_ref, k_ref, vt_ref, lq1_ref, lk1_ref, lq2_ref, lk2_ref, gain_ref, o_ref,
                 acc_s, q2_s, sa_s, sb_s, pa_s, pb_s, *, tq, tk):
    qi = pl.program_id(2)
    qt = qt_ref[...]
    row = lax.broadcasted_iota(jnp.int32, qt.shape, 0)
    zero = jnp.zeros_like(qt)
    q2 = jnp.concatenate([jnp.where(row < ATT_HEAD_DIM, qt, zero),
                          jnp.where(row >= ATT_HEAD_DIM, qt, zero)], axis=1)
    n2 = 2 * tq
    acc_s[...] = jnp.zeros(acc_s.shape, F32)

    def update(k0, nk, c0, nc, m_all, l_all, masked, scores_ref=None):
        cols = slice(c0, c0 + nc)
        vtb = vt_ref[:, pl.ds(k0, nk)]
        s = _dot(k_ref[pl.ds(k0, nk), :], q2_s[:, cols]) if scores_ref is None else scores_ref[...]
        if masked:
            kpos = k0 + lax.broadcasted_iota(jnp.int32, (nk, nc), 0)
            col = c0 + lax.broadcasted_iota(jnp.int32, (nk, nc), 1)
            qpos = qi * tq + jnp.where(col >= tq, col - tq, col)
            s = jnp.where(kpos <= qpos, s, -jnp.inf)
        s3 = s.reshape(nk // SUBLANES, SUBLANES, nc)
        m_old = m_all[:, cols]
        m_new = jnp.maximum(m_old, _sublane_all(jnp.max(s3, axis=0), jnp.maximum))
        p3 = jnp.exp2(s3 - m_new[None])
        corr = jnp.exp2(m_old - m_new)
        l_new = l_all[:, cols] * corr + jnp.sum(p3, axis=0)
        pv = _dot(vtb, p3.reshape(nk, nc).astype(BF16))
        acc3 = acc_s[:, cols].reshape(LANES // SUBLANES, SUBLANES, nc)
        acc_s[:, cols] = (acc3 * corr[None]).reshape(LANES, nc) + pv
        if nc == n2:
            return m_new, l_new
        put = lambda full, part: jnp.concatenate(
            ([full[:, :c0]] if c0 else []) + [part] + ([full[:, c0 + nc:]] if c0 + nc < n2 else []), axis=1)
        return put(m_all, m_new), put(l_all, l_new)

    q2_s[...] = q2
    groups = n2 // ATT_GROUP

    def weighted_values(t, p_prv, corr_prv, gs):
        pv = _dot(vt_ref[:, pl.ds(pl.multiple_of(t * tk, tk), tk)], p_prv[:, gs])
        acc3 = acc_s[:, gs].reshape(LANES // SUBLANES, SUBLANES, gs.stop - gs.start)
        acc_s[:, gs] = (acc3 * corr_prv[:, gs][None]).reshape(LANES, gs.stop - gs.start) + pv

    def step(t, carry, s_cur, p_cur, s_nxt, p_prv):
        m_old, l_old, corr_prv = carry
        t_prv = jnp.maximum(t - 1, 0)
        kn = pl.multiple_of((t + 1) * tk, tk)
        m_parts, l_parts, c_parts = [], [], []
        for g in range(groups):
            gs = slice(g * ATT_GROUP, (g + 1) * ATT_GROUP)
            weighted_values(t_prv, p_prv, corr_prv, gs)
            s_nxt[:, gs] = _dot(k_ref[pl.ds(kn, tk), :], q2_s[:, gs])
            s3 = s_cur[:, gs].reshape(tk // SUBLANES, SUBLANES, ATT_GROUP)
            m_o = m_old[:, gs]
            m_n = jnp.maximum(m_o, _sublane_all(jnp.max(s3, axis=0), jnp.maximum))
            p3 = jnp.exp2(s3 - m_n[None])
            corr = jnp.exp2(m_o - m_n)
            p_cur[:, gs] = p3.reshape(tk, ATT_GROUP).astype(BF16)
            m_parts.append(m_n)
            l_parts.append(l_old[:, gs] * corr + jnp.sum(p3, axis=0))
            c_parts.append(corr)
        cat = lambda parts: jnp.concatenate(parts, axis=1)
        return cat(m_parts), cat(l_parts), cat(c_parts)

    n_full = (qi * tq) // tk
    pb_s[...] = jnp.zeros(pb_s.shape, BF16)
    sa_s[...] = _dot(k_ref[pl.ds(0, tk), :], q2)
    carry = (jnp.full((SUBLANES, n2), -jnp.inf, F32), jnp.zeros((SUBLANES, n2), F32),
             jnp.ones((SUBLANES, n2), F32))

    def pair(i, carry):
        carry = step(2 * i, carry, sa_s, pa_s, sb_s, pb_s)
        return step(2 * i + 1, carry, sb_s, pb_s, sa_s, pa_s)

    m, l, corr = lax.fori_loop(0, n_full // 2, pair, carry)
    for g in range(groups):
        weighted_values(jnp.maximum(n_full - 1, 0), pb_s, corr, slice(g * ATT_GROUP, (g + 1) * ATT_GROUP))

    half = tq // 2
    kd = pl.multiple_of(qi * tq, tq)
    m, l = update(kd, half, 0, n2, m, l, True, scores_ref=sa_s)
    m, l = update(kd + half, half, half, half, m, l, True)
    m, l = update(kd + half, half, tq + half, half, m, l, True)

    lam = (jnp.exp(jnp.sum(lq1_ref[...] * lk1_ref[...], axis=-1, keepdims=True))
           - jnp.exp(jnp.sum(lq2_ref[...] * lk2_ref[...], axis=-1, keepdims=True))
           + LAMBDA_INIT)
    l = _sublane_all(l, jnp.add)
    o3 = acc_s[...].reshape(LANES // SUBLANES, SUBLANES, n2) / l[None]
    d3 = o3[:, :, :tq] - lam * o3[:, :, tq:]
    ms = _sublane_all(jnp.sum(d3 * d3, axis=0), jnp.add) * (1.0 / LANES)
    d3 = d3 * lax.rsqrt(ms + SUBLN_EPS)[None]
    o = d3.reshape(LANES, tq).T
    o_ref[...] = (o * gain_ref[...] * (1.0 - LAMBDA_INIT)).astype(o_ref.dtype)


def diff_attention(qvt, kk, lq1, lk1, lq2, lk2, subln_gain, tq=1024):
    b, h2, _, s = qvt.shape
    h = h2 // 2
    tq = min(tq, s)
    tk = tq // 2
    vec = lambda a: a.reshape(1, -1).astype(F32)
    small = lambda n: pl.BlockSpec((1, n), lambda bi, hi, i: (0, 0))
    return pl.pallas_call(
        functools.partial(_attn_kernel, tq=tq, tk=tk),
        grid=(b, h, s // tq),
        in_specs=[pl.BlockSpec((None, None, LANES, tq), lambda bi, hi, i: (bi, hi, 0, i)),
                  pl.BlockSpec((None, s, LANES), lambda bi, hi, i: (bi, 0, hi)),
                  pl.BlockSpec((None, None, LANES, s), lambda bi, hi, i: (bi, h + hi, 0, 0)),
                  small(ATT_HEAD_DIM), small(ATT_HEAD_DIM), small(ATT_HEAD_DIM), small(ATT_HEAD_DIM),
                  small(LANES)],
        out_specs=pl.BlockSpec((None, tq, LANES), lambda bi, hi, i: (bi, i, hi)),
        out_shape=jax.ShapeDtypeStruct((b, s, h * LANES), BF16),
        scratch_shapes=[pltpu.VMEM((LANES, 2 * tq), F32), pltpu.VMEM((LANES, 2 * tq), BF16),
                        pltpu.VMEM((tk, 2 * tq), F32), pltpu.VMEM((tk, 2 * tq), F32),
                        pltpu.VMEM((tk, 2 * tq), BF16), pltpu.VMEM((tk, 2 * tq), BF16)],
        compiler_params=_params("parallel", "parallel", "arbitrary"),
        name="diff_attention",
    )(qvt, kk, qvt, vec(lq1), vec(lk1), vec(lq2), vec(lk2), vec(subln_gain))


def _bdot(a, b):
    return jnp.dot(a.astype(BF16), b.astype(BF16), preferred_element_type=F32)


def _split_bf16(w):
    hi = w.astype(BF16)
    return jnp.stack([hi, (w - hi.astype(F32)).astype(BF16)])


def _dot3(a, b_ref):
    a_hi = a.astype(BF16)
    a_lo = (a - a_hi.astype(F32)).astype(BF16)
    d = functools.partial(jnp.dot, preferred_element_type=F32)
    return d(a_hi, b_ref[0]) + (d(a_hi, b_ref[1]) + d(a_lo, b_ref[0]))


def _each(f, *lists):
    return [f(*xs) for xs in zip(*lists)]


def _rwkv_pairs(pairs, states, c):
    n = RWKV_HEAD_DIM
    c2 = 2 * c
    row = lax.broadcasted_iota(jnp.int32, (c2, c2), 0)
    col = lax.broadcasted_iota(jnp.int32, (c2, c2), 1)
    lower = row >= col
    strict = row > col
    eye2 = (row == col).astype(F32)
    rc = lax.broadcasted_iota(jnp.int32, (c, c), 0)
    cc = lax.broadcasted_iota(jnp.int32, (c, c), 1)
    lower_b = (rc >= cc).astype(BF16)
    li = lax.broadcasted_iota(jnp.int32, (LANES, LANES), 0)
    lj = lax.broadcasted_iota(jnp.int32, (LANES, LANES), 1)
    same_head = ((li // n) == (lj // n)).astype(BF16)
    eye_l = (li == lj).astype(F32)
    head1 = lax.broadcasted_iota(jnp.int32, (c, LANES), 1) >= n

    def stack(x):
        return jnp.concatenate([jnp.where(head1, 0.0, x), jnp.where(head1, x, 0.0)], axis=0)

    def head_sum(x):
        return jnp.dot(x.astype(BF16), same_head, preferred_element_type=F32)

    r, k, v, lw, ai, g, kk_w, ka_w, rk_w, lnw, lnb = [list(x) for x in zip(*pairs)]

    kk = _each(lambda k, w: k * w, k, kk_w)
    kk = _each(lambda kk: kk * lax.rsqrt(jnp.maximum(head_sum(kk * kk), 1e-24)), kk)
    k2 = _each(lambda k, ai, w: k * (1.0 + (ai - 1.0) * w), k, ai, ka_w)
    b = _each(lambda kk, ai: kk * ai, kk, ai)
    bonus = _each(lambda r, k2, w, v: head_sum(r * k2 * w) * v, r, k2, rk_w, v)

    def cumsum(lw):
        hi = lw.astype(BF16)
        r1 = lw - hi.astype(F32)
        mid = r1.astype(BF16)
        lo = (r1 - mid.astype(F32)).astype(BF16)
        c3 = jnp.dot(lower_b, jnp.concatenate([hi, mid, lo], axis=1), preferred_element_type=F32)
        return c3[:, :LANES] + c3[:, LANES:2 * LANES] + c3[:, 2 * LANES:]

    cum = _each(cumsum, lw)
    e_cum = _each(jnp.exp, cum)
    e_neg = _each(lambda cm: jnp.exp(-cm), cum)
    e_tail = _each(lambda cm: jnp.exp(cm[c - 1:c, :] - cm), cum)
    at = _each(lambda kk, cm, l: stack(-kk * jnp.exp(cm - l)), kk, cum, lw)
    rt = _each(lambda r, e: stack(r * e), r, e_cum)
    vs = _each(stack, v)
    qq = _each(lambda x, y: jnp.concatenate([x, y], axis=0).astype(BF16), at, rt)
    kq = _each(lambda b, k2, e: jnp.concatenate([stack(b * e), stack(k2 * e)], axis=0).astype(BF16),
               b, k2, e_neg)
    s = _each(_dot_t, qq, kq)
    a_ab = _each(lambda s: jnp.where(strict, s[:c2, :c2], 0.0), s)
    a_ak = _each(lambda s: jnp.where(strict, s[:c2, c2:], 0.0), s)
    a_rb = _each(lambda s: jnp.where(lower, s[c2:, :c2], 0.0), s)
    a_rk = _each(lambda s: jnp.where(lower, s[c2:, c2:], 0.0), s)

    def level(m):
        return ((row // (2 * m)) == (col // (2 * m))) & ((row // m) % 2 == 1) & ((col // m) % 2 == 0)

    t = _each(lambda x: eye2 + jnp.where(level(1), x, 0.0), a_ab)
    m = 2
    while m < c:
        sel = level(m)
        tn = _each(lambda t, x: _bdot(t, jnp.where(sel, x, 0.0)), t, a_ab)
        t = _each(lambda t, x: t + _bdot(x, t), t, tn)
        m *= 2

    x1 = _each(_bdot, a_ak, vs)
    wu = _each(lambda t, at, x1: _bdot(t, jnp.concatenate([at, x1], axis=1)), t, at, x1)
    bh_t = _each(lambda b, e: stack(b * e).T, b, e_tail)
    kh_t = _each(lambda k2, e: stack(k2 * e).T, k2, e_tail)
    z1 = _each(lambda a, bt, wu: _bdot(jnp.concatenate([a, bt], axis=0), wu), a_rb, bh_t, wu)
    z2 = _each(lambda a, kt, v: _bdot(jnp.concatenate([a, kt], axis=0), v), a_rk, kh_t, vs)
    pm = _each(lambda z1, rt, e: jnp.concatenate([rt + z1[:c2, :LANES],
                                                  eye_l * e[c - 1:c, :] + z1[c2:, :LANES]], axis=0),
               z1, rt, e_cum)
    yn0 = _each(lambda z1, z2: z1[:, LANES:] + z2, z1, z2)
    chunks = len(pairs) // len(states)
    h_next = list(states)
    yh = [None] * len(pairs)
    for j in range(chunks):
        for blk in range(len(states)):
            i = blk * chunks + j
            yh[i] = _bdot(pm[i], h_next[blk]) + yn0[i]
            h_next[blk] = yh[i][c2:]

    y = _each(lambda yh: yh[:c] + yh[c:c2], yh)
    d = _each(lambda y: y - head_sum(y) * (1.0 / n), y)
    var = _each(lambda d: head_sum(d * d) * (1.0 / n), d)
    outs = _each(lambda d, var, lnw, lnb, bonus, g: (d * lax.rsqrt(var + GN_EPS) * lnw + lnb + bonus) * g,
                 d, var, lnw, lnb, bonus, g)
    return outs, h_next


def _rwkv_kernel(rkv_ref, lora_ref, mu_rkv_ref, mu_lora_ref, w0_ref, wl_ref, a0_ref, al_ref, gl_ref,
                 kk_ref, ka_ref, rk_ref, lnw_ref, lnb_ref, o_ref,
                 state_s, prev_rkv_s, prev_lora_s, r_s, k_s, v_s, lw_s, ai_s, g_s, *, c):
    ci = pl.program_id(1)
    width = RWKV_HEADS * RWKV_HEAD_DIM

    @pl.when(ci == 0)
    def _():
        state_s[...] = jnp.zeros(state_s.shape, F32)
        prev_rkv_s[...] = jnp.zeros(prev_rkv_s.shape, F32)
        prev_lora_s[...] = jnp.zeros(prev_lora_s.shape, F32)

    def shift_mix(p, prev_s, mu):
        rolled = pltpu.roll(p, 1, 0)
        first = lax.broadcasted_iota(jnp.int32, p.shape, 0) == 0
        shifted = jnp.where(first, prev_s[7:8, :], rolled)
        prev_s[...] = p[p.shape[0] - 8:, :]
        return p + mu * (shifted - p)

    z = shift_mix(rkv_ref[...], prev_rkv_s, mu_rkv_ref[...])
    zl = shift_mix(lora_ref[...], prev_lora_s, mu_lora_ref[...])
    r_s[...] = z[:, :width]
    k_s[...] = z[:, width:2 * width]
    v_s[...] = z[:, 2 * width:]
    xw = zl[:, :DECAY_LORA]
    xa = zl[:, DECAY_LORA:DECAY_LORA + ICLR_LORA]
    xg = zl[:, DECAY_LORA + ICLR_LORA:DECAY_LORA + ICLR_LORA + GATE_LORA]
    lw_s[...] = -math.exp(-0.5) * jax.nn.sigmoid(w0_ref[...] + _dot3(jnp.tanh(xw), wl_ref))
    ai_s[...] = jax.nn.sigmoid(a0_ref[...] + _bdot(xa, al_ref[...]))
    g_s[...] = _bdot(jax.nn.sigmoid(xg), gl_ref[...])

    def group_body(gi, carry):
        slices = [pl.ds(pl.multiple_of((gi * RWKV_GROUP_BLOCKS + blk) * LANES, LANES), LANES)
                  for blk in range(RWKV_GROUP_BLOCKS)]
        chunk_rows = [pl.ds(j * c, c) for j in range(r_s.shape[0] // c)]
        pairs = [tuple(ref[rows, sl] for ref in (r_s, k_s, v_s, lw_s, ai_s, g_s))
                 + tuple(ref[:, sl] for ref in (kk_ref, ka_ref, rk_ref, lnw_ref, lnb_ref))
                 for sl in slices for rows in chunk_rows]
        states = [state_s[gi * RWKV_GROUP_BLOCKS + blk] for blk in range(RWKV_GROUP_BLOCKS)]
        outs, h_next = _rwkv_pairs(pairs, states, c)
        for blk, sl in enumerate(slices):
            for j, rows in enumerate(chunk_rows):
                o_ref[rows, sl] = outs[blk * len(chunk_rows) + j].astype(o_ref.dtype)
            state_s[gi * RWKV_GROUP_BLOCKS + blk] = h_next[blk]
        return carry

    lax.fori_loop(0, width // (LANES * RWKV_GROUP_BLOCKS), group_body, 0)


def rwkv7(rkv, lora, mu_rkv, mu_lora, w0, wl, a0, al, gl, k_k, k_a, r_k, lnx_w, lnx_b):
    b, s, _ = rkv.shape
    c = min(RWKV_CHUNK, s)
    rows = min(RWKV_STEP_CHUNKS * c, s)
    width = RWKV_HEADS * RWKV_HEAD_DIM
    row = lambda a: a.reshape(1, -1).astype(F32)
    const = lambda shape: pl.BlockSpec(shape, lambda bi, ci: (0,) * len(shape))
    return pl.pallas_call(
        functools.partial(_rwkv_kernel, c=c),
        grid=(b, s // rows),
        in_specs=[pl.BlockSpec((None, rows, 3 * width), lambda bi, ci: (bi, ci, 0)),
                  pl.BlockSpec((None, rows, LORA_PAD), lambda bi, ci: (bi, ci, 0)),
                  const((1, 3 * width)), const((1, LORA_PAD)),
                  const((1, width)), const((2, DECAY_LORA, width)),
                  const((1, width)), const((ICLR_LORA, width)), const((GATE_LORA, width)),
                  const((1, width)), const((1, width)), const((1, width)),
                  const((1, width)), const((1, width))],
        out_specs=pl.BlockSpec((None, rows, width), lambda bi, ci: (bi, ci, 0)),
        out_shape=jax.ShapeDtypeStruct((b, s, width), BF16),
        scratch_shapes=[pltpu.VMEM((width // LANES, LANES, LANES), F32),
                        pltpu.VMEM((8, 3 * width), F32), pltpu.VMEM((8, LORA_PAD), F32)]
                       + [pltpu.VMEM((rows, width), F32)] * 6,
        compiler_params=_params("parallel", "arbitrary"),
        name="rwkv7",
    )(rkv, lora, row(mu_rkv), row(mu_lora), row(w0), _split_bf16(wl.astype(F32)), row(a0), al.astype(BF16),
      gl.astype(BF16), row(k_k), row(k_a), row(r_k), row(lnx_w), row(lnx_b))


MERGE_CHUNK = 256


def _merge_kernel(ya_ref, yr_ref, g0_ref, g1_ref, wa_ref, wr_ref, o_ref):
    ya, yr = ya_ref[...], yr_ref[...]
    for c0 in range(0, o_ref.shape[1], MERGE_CHUNK):
        cs = slice(c0, c0 + MERGE_CHUNK)
        za = _dot(ya, wa_ref[:, cs])
        zr = _dot(yr, wr_ref[:, cs])
        ga = jax.nn.sigmoid(g0_ref[:, cs].astype(F32))
        gr = jax.nn.sigmoid(g1_ref[:, cs].astype(F32))
        o_ref[:, cs] = (ga * za + gr * zr).astype(o_ref.dtype)


def gated_merge(y_att, y_rwkv, gates, wa, wr, tm=1024, tn=1024):
    b, s, kdim = y_att.shape
    n = wa.shape[1]
    tm = min(tm, s)
    nb = n // tn
    return pl.pallas_call(
        _merge_kernel,
        grid=(b, s // tm, nb),
        in_specs=[pl.BlockSpec((None, tm, kdim), lambda bi, i, j: (bi, i, 0)),
                  pl.BlockSpec((None, tm, kdim), lambda bi, i, j: (bi, i, 0)),
                  pl.BlockSpec((None, tm, tn), lambda bi, i, j: (bi, i, j)),
                  pl.BlockSpec((None, tm, tn), lambda bi, i, j: (bi, i, nb + j)),
                  pl.BlockSpec((kdim, tn), lambda bi, i, j: (0, j)),
                  pl.BlockSpec((kdim, tn), lambda bi, i, j: (0, j))],
        out_specs=pl.BlockSpec((None, tm, tn), lambda bi, i, j: (bi, i, j)),
        out_shape=jax.ShapeDtypeStruct((b, s, n), BF16),
        compiler_params=_params("parallel", "parallel", "arbitrary"),
        name="gated_merge",
    )(y_att, y_rwkv, gates, gates, wa, wr)


def _out_proj_kernel(m_ref, w_ref, x_ref, mod_ref, gain_ref, x1_ref, h2_ref):
    x1 = x_ref[...] + mod_ref[2:3, :] * _dot(m_ref[...], w_ref[...])
    x1_ref[...] = x1
    xn = x1 * lax.rsqrt(jnp.mean(x1 * x1, axis=-1, keepdims=True) + NORM_EPS)
    h2_ref[...] = (xn * gain_ref[...] * (1.0 + mod_ref[4:5, :]) + mod_ref[3:4, :]).astype(h2_ref.dtype)


def out_proj(merged, w_out, x, mod, gain2, tm=512):
    b, s, d = x.shape
    tm = min(tm, s)
    return pl.pallas_call(
        _out_proj_kernel,
        grid=(b, s // tm),
        in_specs=[pl.BlockSpec((None, tm, d), lambda bi, i: (bi, i, 0)),
                  pl.BlockSpec((d, d), lambda bi, i: (0, 0), pipeline_mode=pl.Buffered(1)),
                  pl.BlockSpec((None, tm, d), lambda bi, i: (bi, i, 0)),
                  pl.BlockSpec((None, 8, d), lambda bi, i: (bi, 0, 0)),
                  pl.BlockSpec((1, d), lambda bi, i: (0, 0))],
        out_specs=[pl.BlockSpec((None, tm, d), lambda bi, i: (bi, i, 0)),
                   pl.BlockSpec((None, tm, d), lambda bi, i: (bi, i, 0))],
        out_shape=[jax.ShapeDtypeStruct((b, s, d), F32), jax.ShapeDtypeStruct((b, s, d), BF16)],
        compiler_params=_params("parallel", "parallel"),
        name="out_proj",
    )(merged, w_out, x, mod, gain2.reshape(1, d))


GLU_HALO = 16
GLU_CHUNK = 256


def _glu_up_kernel(h_ref, halo_ref, wg_ref, wv_ref, cw_ref, cb_ref, o_ref, hext_s):
    i = pl.program_id(1)

    @pl.when(pl.program_id(2) == 0)
    def _():
        hext_s[:GLU_HALO, :] = jnp.where(i > 0, halo_ref[...], jnp.zeros_like(halo_ref[...]))
        hext_s[GLU_HALO:, :] = h_ref[...]

    hext = hext_s[...]
    chunks = [slice(c0, c0 + GLU_CHUNK) for c0 in range(0, o_ref.shape[1], GLU_CHUNK)]

    def gate(ug, cs):
        cw = cw_ref[:, cs]
        conv = (cb_ref[:, cs] + pltpu.roll(ug, 2, 0)[GLU_HALO:] * cw[0:1, :]
                + pltpu.roll(ug, 1, 0)[GLU_HALO:] * cw[1:2, :] + ug[GLU_HALO:] * cw[2:3, :])
        return 0.5 * conv * (1.0 + lax.erf(conv * (2.0 ** -0.5)))

    ug = _dot(hext, wg_ref[:, chunks[0]])
    gates = []
    for cs, nxt in zip(chunks, chunks[1:] + [None]):
        ug_next = _dot(hext, wg_ref[:, nxt]) if nxt is not None else None
        gates.append(gate(ug, cs))
        ug = ug_next
    for cs, g in zip(chunks, gates):
        o_ref[:, cs] = (g * _dot(hext[GLU_HALO:], wv_ref[:, cs])).astype(o_ref.dtype)


def glu_up(h, w_up, conv_w, conv_b, tm=1024, tf=512):
    b, s, d = h.shape
    ff = w_up.shape[1] // 2
    tm = min(tm, s)
    nf = ff // tf
    cw8 = jnp.zeros((8, ff), F32).at[:conv_w.shape[0]].set(conv_w)
    hb = tm // GLU_HALO
    return pl.pallas_call(
        _glu_up_kernel,
        grid=(b, s // tm, nf),
        in_specs=[pl.BlockSpec((None, tm, d), lambda bi, i, f: (bi, i, 0)),
                  pl.BlockSpec((None, GLU_HALO, d), lambda bi, i, f: (bi, jnp.maximum(i * hb - 1, 0), 0)),
                  pl.BlockSpec((d, tf), lambda bi, i, f: (0, f)),
                  pl.BlockSpec((d, tf), lambda bi, i, f: (0, nf + f)),
                  pl.BlockSpec((8, tf), lambda bi, i, f: (0, f)),
                  pl.BlockSpec((1, tf), lambda bi, i, f: (0, f))],
        out_specs=pl.BlockSpec((None, tm, tf), lambda bi, i, f: (bi, i, f)),
        out_shape=jax.ShapeDtypeStruct((b, s, ff), BF16),
        scratch_shapes=[pltpu.VMEM((GLU_HALO + tm, d), BF16)],
        compiler_params=_params("parallel", "parallel", "arbitrary"),
        name="glu_up",
    )(h, h, w_up, w_up, cw8, conv_b.reshape(1, ff))


def _down_kernel(a_ref, wd_ref, x1_ref, mod_ref, gain_ref, o_ref):
    x2 = x1_ref[...] + mod_ref[5:6, :] * _dot(a_ref[...], wd_ref[...])
    o_ref[...] = x2 * lax.rsqrt(jnp.mean(x2 * x2, axis=-1, keepdims=True) + NORM_EPS) * gain_ref[...]


def down_proj(act, w_down, x1, mod, final_gain, tm=256):
    b, s, d = x1.shape
    ff = w_down.shape[0]
    tm = min(tm, s)
    return pl.pallas_call(
        _down_kernel,
        grid=(b, s // tm),
        in_specs=[pl.BlockSpec((None, tm, ff), lambda bi, i: (bi, i, 0)),
                  pl.BlockSpec((ff, d), lambda bi, i: (0, 0), pipeline_mode=pl.Buffered(1)),
                  pl.BlockSpec((None, tm, d), lambda bi, i: (bi, i, 0)),
                  pl.BlockSpec((None, 8, d), lambda bi, i: (bi, 0, 0)),
                  pl.BlockSpec((1, d), lambda bi, i: (0, 0))],
        out_specs=pl.BlockSpec((None, tm, d), lambda bi, i: (bi, i, 0)),
        out_shape=jax.ShapeDtypeStruct((b, s, d), F32),
        compiler_params=_params("parallel", "parallel"),
        name="down_proj",
    )(act, w_down, x1, mod, final_gain.reshape(1, d))


def kernel(x, c, w_ada, b_ada, norm1_gain, w_in, lambda_q1, lambda_k1, lambda_q2, lambda_k2, subln_gain, mu_shift, w0, w_lora_up, a0, a_lora_up, g_lora_up, k_k, k_a, r_k, lnx_w, lnx_b, w_branch, w_out, norm2_gain, w_up, conv_w, conv_b, w_down, final_gain):
    b, s, d = x.shape
    depth = w_in.shape[0]
    assert depth == 1
    l = 0
    qk_cols = 2 * ATT_HEADS * 2 * ATT_HEAD_DIM
    v_cols = ATT_HEADS * 2 * ATT_HEAD_DIM
    width = RWKV_HEADS * RWKV_HEAD_DIM
    n_lora = DECAY_LORA + ICLR_LORA + GATE_LORA
    rkv0 = qk_cols + v_cols
    lora0 = rkv0 + 3 * width
    gate0 = lora0 + n_lora

    c8 = jnp.zeros((8, d), F32).at[:b].set(c)
    mod = ada_modulation(c8, w_ada[l], b_ada[l])[:b]
    mod = jnp.pad(mod.reshape(b, 6, d), ((0, 0), (0, 2), (0, 0)))

    h = norm_mod(x, mod, norm1_gain[l], shift_row=0, scale_row=1)

    w_in_l = w_in[l]
    q_cols = qk_cols // 2
    w_qv_t = jnp.concatenate([w_in_l[:, :q_cols], w_in_l[:, qk_cols:rkv0]], axis=1).T.astype(BF16)
    w_k = w_in_l[:, q_cols:qk_cols].astype(BF16)
    w_gate = w_in_l[:, gate0:].astype(BF16)
    w_rkv = w_in_l[:, rkv0:lora0].astype(BF16)
    w_lora = jnp.pad(w_in_l[:, lora0:gate0], ((0, 0), (0, LORA_PAD - n_lora))).astype(BF16)

    rope_lanes, rope_rows = rope_tables(s)
    qv_t = matmul_t(h, w_qv_t, *rope_rows)
    kk = matmul(h, w_k, BF16, tn=1024, rope=rope_lanes)
    gates = matmul(h, w_gate, BF16, tm=2048, tn=1024)
    rkv = matmul(h, w_rkv, F32, tn=1024)
    lora = matmul(h, w_lora, F32, tm=2048, tn=LORA_PAD)

    y_att = diff_attention(qv_t.reshape(b, 2 * ATT_HEADS, LANES, s), kk,
                           lambda_q1[l], lambda_k1[l], lambda_q2[l], lambda_k2[l], subln_gain[l])

    mu = mu_shift[l]
    mu_lora = jnp.pad(mu[3 * width:], (0, LORA_PAD - n_lora))
    y_rwkv = rwkv7(rkv, lora, mu[:3 * width], mu_lora, w0[l], w_lora_up[l], a0[l], a_lora_up[l],
                   g_lora_up[l], k_k[l], k_a[l], r_k[l].reshape(-1), lnx_w[l], lnx_b[l])

    merged = gated_merge(y_att, y_rwkv, gates, w_branch[l, 0].astype(BF16), w_branch[l, 1].astype(BF16))
    x1, h2 = out_proj(merged, w_out[l].astype(BF16), x, mod, norm2_gain[l])

    act = glu_up(h2, w_up[l].astype(BF16), conv_w[l], conv_b[l])
    return down_proj(act, w_down[l].astype(BF16), x1, mod, final_gain)
```

```python
import functools
import math

import jax
import jax.numpy as jnp
from jax import lax
from jax.experimental import pallas as pl
from jax.experimental.pallas import tpu as pltpu

F32 = jnp.float32
BF16 = jnp.bfloat16
HIGHEST = lax.Precision.HIGHEST

ATT_HEADS = 8
ATT_HEAD_DIM = 64
ROT_DIM = ATT_HEAD_DIM // 4
ROPE_THETA = 500000.0
SUBLN_EPS = 1e-5
RWKV_HEADS = 16
RWKV_HEAD_DIM = 64
DECAY_LORA = 64
ICLR_LORA = 64
GATE_LORA = 160
GN_EPS = 64e-5
NORM_EPS = 1e-6
LAMBDA_INIT = 0.8 - 0.6 * math.exp(-0.3 * 0)

LANES = 128
VMEM_LIMIT = 48 * 1024 * 1024
RWKV_CHUNK = 64
RWKV_STEP_CHUNKS = 2
RWKV_GROUP_BLOCKS = 8
LORA_PAD = 384


def _params(*sem):
    return pltpu.CompilerParams(dimension_semantics=sem, vmem_limit_bytes=VMEM_LIMIT)


def _dot(a, b, precision=None):
    return jnp.dot(a, b, preferred_element_type=F32, precision=precision)


def _dot_t(a, b, precision=None):
    return lax.dot_general(a, b, (((1,), (1,)), ((), ())),
                           preferred_element_type=F32, precision=precision)


def _mod_kernel(c_ref, w_ref, b_ref, o_ref):
    c = c_ref[...]
    s = c * jax.nn.sigmoid(c)
    o_ref[...] = _dot(s, w_ref[...], HIGHEST) + b_ref[...]


def ada_modulation(c8, w_ada, b_ada, tn=1024):
    m, d = c8.shape
    n = w_ada.shape[1]
    return pl.pallas_call(
        _mod_kernel,
        grid=(n // tn,),
        in_specs=[pl.BlockSpec((m, d), lambda j: (0, 0)),
                  pl.BlockSpec((d, tn), lambda j: (0, j)),
                  pl.BlockSpec((1, tn), lambda j: (0, j))],
        out_specs=pl.BlockSpec((m, tn), lambda j: (0, j)),
        out_shape=jax.ShapeDtypeStruct((m, n), F32),
        compiler_params=_params("arbitrary"),
        name="ada_mod",
    )(c8, w_ada, b_ada.reshape(1, n))


def _norm_mod_kernel(x_ref, mod_ref, gain_ref, o_ref, *, shift_row, scale_row):
    x = x_ref[...]
    xn = x * lax.rsqrt(jnp.mean(x * x, axis=-1, keepdims=True) + NORM_EPS)
    h = xn * gain_ref[...] * (1.0 + mod_ref[scale_row:scale_row + 1, :]) + mod_ref[shift_row:shift_row + 1, :]
    o_ref[...] = h.astype(o_ref.dtype)


def norm_mod(x, mod, gain, shift_row, scale_row, tm=1024):
    b, s, d = x.shape
    tm = min(tm, s)
    return pl.pallas_call(
        functools.partial(_norm_mod_kernel, shift_row=shift_row, scale_row=scale_row),
        grid=(b, s // tm),
        in_specs=[pl.BlockSpec((None, tm, d), lambda bi, i: (bi, i, 0)),
                  pl.BlockSpec((None, 8, d), lambda bi, i: (bi, 0, 0)),
                  pl.BlockSpec((1, d), lambda bi, i: (0, 0))],
        out_specs=pl.BlockSpec((None, tm, d), lambda bi, i: (bi, i, 0)),
        out_shape=jax.ShapeDtypeStruct((b, s, d), BF16),
        compiler_params=_params("parallel", "parallel"),
        name="norm_mod",
    )(x, mod, gain.reshape(1, d))


def _mm_kernel(a_ref, w_ref, o_ref):
    o_ref[...] = _dot(a_ref[...], w_ref[...]).astype(o_ref.dtype)


def _mm_rope_kernel(a_ref, w_ref, cos_ref, sina_ref, sinb_ref, o_ref, *, tn):
    acc = _dot(a_ref[...], w_ref[...])
    cos, sina, sinb = cos_ref[...], sina_ref[...], sinb_ref[...]
    for j in range(tn // LANES):
        t = acc[:, j * LANES:(j + 1) * LANES]
        half = ROT_DIM // 2
        r = t * cos + pltpu.roll(t, LANES - half, 1) * sina + pltpu.roll(t, half, 1) * sinb
        o_ref[:, j * LANES:(j + 1) * LANES] = r.astype(o_ref.dtype)


def matmul(a, w, out_dtype, tm=1024, tn=512, rope=None):
    b, s, k = a.shape
    n = w.shape[1]
    tm = min(tm, s)
    tn = min(tn, n)
    assert s % tm == 0 and n % tn == 0
    in_specs = [pl.BlockSpec((None, tm, k), lambda bi, i, j: (bi, i, 0)),
                pl.BlockSpec((k, tn), lambda bi, i, j: (0, j))]
    args = [a, w]
    if rope is None:
        body = _mm_kernel
        name = "matmul"
    else:
        cos, sina, sinb = rope
        body = functools.partial(_mm_rope_kernel, tn=tn)
        in_specs += [pl.BlockSpec((tm, LANES), lambda bi, i, j: (i, 0))] * 3
        args += [cos, sina, sinb]
        name = "matmul_rope"
    return pl.pallas_call(
        body,
        grid=(b, s // tm, n // tn),
        in_specs=in_specs,
        out_specs=pl.BlockSpec((None, tm, tn), lambda bi, i, j: (bi, i, j)),
        out_shape=jax.ShapeDtypeStruct((b, s, n), out_dtype),
        compiler_params=_params("parallel", "parallel", "arbitrary"),
        name=name,
    )(*args)


def _mm_t_kernel(wt_ref, a_ref, cos_ref, sina_ref, sinb_ref, o_ref):
    acc = _dot_t(wt_ref[...], a_ref[...])
    tn, tm = acc.shape
    is_q = pl.program_id(2) == 0
    cos = jnp.where(is_q, cos_ref[...], 1.0)
    sina = jnp.where(is_q, sina_ref[...], 0.0)
    sinb = jnp.where(is_q, sinb_ref[...], 0.0)
    scale = jnp.where(is_q, ATT_HEAD_DIM ** -0.5 * math.log2(math.e), 1.0).astype(F32)
    half = ROT_DIM // 2
    per_map = lambda x: x.reshape(tn // ATT_HEAD_DIM, ATT_HEAD_DIM, tm)
    r = (per_map(acc) * cos[None] + per_map(pltpu.roll(acc, tn - half, 0)) * sina[None]
         + per_map(pltpu.roll(acc, half, 0)) * sinb[None])
    o_ref[...] = (r.reshape(tn, tm) * scale).astype(o_ref.dtype)


def matmul_t(a, wt, cos_t, sina_t, sinb_t, tm=1024, tn=1024):
    b, s, k = a.shape
    n = wt.shape[0]
    tm = min(tm, s)
    table = pl.BlockSpec((ATT_HEAD_DIM, tm), lambda bi, i, j: (0, i))
    return pl.pallas_call(
        _mm_t_kernel,
        grid=(b, s // tm, n // tn),
        in_specs=[pl.BlockSpec((tn, k), lambda bi, i, j: (j, 0)),
                  pl.BlockSpec((None, tm, k), lambda bi, i, j: (bi, i, 0)),
                  table, table, table],
        out_specs=pl.BlockSpec((None, tn, tm), lambda bi, i, j: (bi, j, i)),
        out_shape=jax.ShapeDtypeStruct((b, n, s), BF16),
        compiler_params=_params("parallel", "parallel", "arbitrary"),
        name="matmul_t",
    )(wt, a, cos_t, sina_t, sinb_t)


def rope_tables(seq):
    half = ROT_DIM // 2
    pos = jnp.arange(seq, dtype=F32)
    inv = ROPE_THETA ** (-jnp.arange(0, ROT_DIM, 2, dtype=F32) / ROT_DIM)
    ang = pos[:, None] * inv[None, :]
    cos, sin = jnp.cos(ang), jnp.sin(ang)
    ones = jnp.ones((seq, ATT_HEAD_DIM - ROT_DIM), F32)
    zeros = jnp.zeros((seq, ATT_HEAD_DIM - ROT_DIM), F32)
    zh = jnp.zeros((seq, half), F32)
    cos64 = jnp.concatenate([cos, cos, ones], axis=1)
    sina64 = jnp.concatenate([-sin, zh, zeros], axis=1)
    sinb64 = jnp.concatenate([zh, sin, zeros], axis=1)
    rep = LANES // ATT_HEAD_DIM
    by_lane = (jnp.tile(cos64, (1, rep)), jnp.tile(sina64, (1, rep)), jnp.tile(sinb64, (1, rep)))
    by_row = (cos64.T, sina64.T, sinb64.T)
    return by_lane, by_row


SUBLANES = 8
ATT_GROUP = 256


def _sublane_all(x, op):
    for shift in (4, 2, 1):
        x = op(x, pltpu.roll(x, shift, 0))
    return x


def _attn_kernel(qt_ref, k_ref, vt_ref, lq1_ref, lk1_ref, lq2_ref, lk2_ref, gain_ref, o_ref,
                 acc_s, q2_s, sa_s, sb_s, pa_s, pb_s, *, tq, tk):
    qi = pl.program_id(2)
    qt = qt_ref[...]
    row = lax.broadcasted_iota(jnp.int32, qt.shape, 0)
    zero = jnp.zeros_like(qt)
    q2 = jnp.concatenate([jnp.where(row < ATT_HEAD_DIM, qt, zero),
                          jnp.where(row >= ATT_HEAD_DIM, qt, zero)], axis=1)
    n2 = 2 * tq
    acc_s[...] = jnp.zeros(acc_s.shape, F32)

    def update(k0, nk, c0, nc, m_all, l_all, masked):
        cols = slice(c0, c0 + nc)
        kb = k_ref[pl.ds(k0, nk), :]
        vtb = vt_ref[:, pl.ds(k0, nk)]
        s = _dot(kb, q2_s[:, cols])
        if masked:
            kpos = k0 + lax.broadcasted_iota(jnp.int32, (nk, nc), 0)
            col = c0 + lax.broadcasted_iota(jnp.int32, (nk, nc), 1)
            qpos = qi * tq + jnp.where(col >= tq, col - tq, col)
            s = jnp.where(kpos <= qpos, s, -jnp.inf)
        s3 = s.reshape(nk // SUBLANES, SUBLANES, nc)
        m_old = m_all[:, cols]
        m_new = jnp.maximum(m_old, _sublane_all(jnp.max(s3, axis=0), jnp.maximum))
        p3 = jnp.exp2(s3 - m_new[None])
        corr = jnp.exp2(m_old - m_new)
        l_new = l_all[:, cols] * corr + jnp.sum(p3, axis=0)
        pv = _dot(vtb, p3.reshape(nk, nc).astype(BF16))
        acc3 = acc_s[:, cols].reshape(LANES // SUBLANES, SUBLANES, nc)
        acc_s[:, cols] = (acc3 * corr[None]).reshape(LANES, nc) + pv
        if nc == n2:
            return m_new, l_new
        put = lambda full, part: jnp.concatenate(
            ([full[:, :c0]] if c0 else []) + [part] + ([full[:, c0 + nc:]] if c0 + nc < n2 else []), axis=1)
        return put(m_all, m_new), put(l_all, l_new)

    q2_s[...] = q2
    groups = n2 // ATT_GROUP

    def weighted_values(t, p_prv, corr_prv, gs):
        pv = _dot(vt_ref[:, pl.ds(pl.multiple_of(t * tk, tk), tk)], p_prv[:, gs])
        acc3 = acc_s[:, gs].reshape(LANES // SUBLANES, SUBLANES, gs.stop - gs.start)
        acc_s[:, gs] = (acc3 * corr_prv[:, gs][None]).reshape(LANES, gs.stop - gs.start) + pv

    def step(t, carry, s_cur, p_cur, s_nxt, p_prv):
        m_old, l_old, corr_prv = carry
        t_prv = jnp.maximum(t - 1, 0)
        kn = pl.multiple_of((t + 1) * tk, tk)
        m_parts, l_parts, c_parts = [], [], []
        for g in range(groups):
            gs = slice(g * ATT_GROUP, (g + 1) * ATT_GROUP)
            weighted_values(t_prv, p_prv, corr_prv, gs)
            s_nxt[:, gs] = _dot(k_ref[pl.ds(kn, tk), :], q2_s[:, gs])
            s3 = s_cur[:, gs].reshape(tk // SUBLANES, SUBLANES, ATT_GROUP)
            m_o = m_old[:, gs]
            m_n = jnp.maximum(m_o, _sublane_all(jnp.max(s3, axis=0), jnp.maximum))
            p3 = jnp.exp2(s3 - m_n[None])
            corr = jnp.exp2(m_o - m_n)
            p_cur[:, gs] = p3.reshape(tk, ATT_GROUP).astype(BF16)
            m_parts.append(m_n)
            l_parts.append(l_old[:, gs] * corr + jnp.sum(p3, axis=0))
            c_parts.append(corr)
        cat = lambda parts: jnp.concatenate(parts, axis=1)
        return cat(m_parts), cat(l_parts), cat(c_parts)

    n_full = (qi * tq) // tk
    pb_s[...] = jnp.zeros(pb_s.shape, BF16)
    sa_s[...] = _dot(k_ref[pl.ds(0, tk), :], q2)
    carry = (jnp.full((SUBLANES, n2), -jnp.inf, F32), jnp.zeros((SUBLANES, n2), F32),
             jnp.ones((SUBLANES, n2), F32))

    def pair(i, carry):
        carry = step(2 * i, carry, sa_s, pa_s, sb_s, pb_s)
        return step(2 * i + 1, carry, sb_s, pb_s, sa_s, pa_s)

    m, l, corr = lax.fori_loop(0, n_full // 2, pair, carry)
    for g in range(groups):
        weighted_values(jnp.maximum(n_full - 1, 0), pb_s, corr, slice(g * ATT_GROUP, (g + 1) * ATT_GROUP))

    half = tq // 2
    kd = pl.multiple_of(qi * tq, tq)
    m, l = update(kd, half, 0, n2, m, l, True)
    m, l = update(kd + half, half, half, half, m, l, True)
    m, l = update(kd + half, half, tq + half, half, m, l, True)

    lam = (jnp.exp(jnp.sum(lq1_ref[...] * lk1_ref[...], axis=-1, keepdims=True))
           - jnp.exp(jnp.sum(lq2_ref[...] * lk2_ref[...], axis=-1, keepdims=True))
           + LAMBDA_INIT)
    l = _sublane_all(l, jnp.add)
    o3 = acc_s[...].reshape(LANES // SUBLANES, SUBLANES, n2) / l[None]
    d3 = o3[:, :, :tq] - lam * o3[:, :, tq:]
    ms = _sublane_all(jnp.sum(d3 * d3, axis=0), jnp.add) * (1.0 / LANES)
    d3 = d3 * lax.rsqrt(ms + SUBLN_EPS)[None]
    o = d3.reshape(LANES, tq).T
    o_ref[...] = (o * gain_ref[...] * (1.0 - LAMBDA_INIT)).astype(o_ref.dtype)


def diff_attention(qvt, kk, lq1, lk1, lq2, lk2, subln_gain, tq=1024):
    b, h2, _, s = qvt.shape
    h = h2 // 2
    tq = min(tq, s)
    tk = tq // 2
    vec = lambda a: a.reshape(1, -1).astype(F32)
    small = lambda n: pl.BlockSpec((1, n), lambda bi, hi, i: (0, 0))
    return pl.pallas_call(
        functools.partial(_attn_kernel, tq=tq, tk=tk),
        grid=(b, h, s // tq),
        in_specs=[pl.BlockSpec((None, None, LANES, tq), lambda bi, hi, i: (bi, hi, 0, i)),
                  pl.BlockSpec((None, s, LANES), lambda bi, hi, i: (bi, 0, hi)),
                  pl.BlockSpec((None, None, LANES, s), lambda bi, hi, i: (bi, h + hi, 0, 0)),
                  small(ATT_HEAD_DIM), small(ATT_HEAD_DIM), small(ATT_HEAD_DIM), small(ATT_HEAD_DIM),
                  small(LANES)],
        out_specs=pl.BlockSpec((None, tq, LANES), lambda bi, hi, i: (bi, i, hi)),
        out_shape=jax.ShapeDtypeStruct((b, s, h * LANES), BF16),
        scratch_shapes=[pltpu.VMEM((LANES, 2 * tq), F32), pltpu.VMEM((LANES, 2 * tq), BF16),
                        pltpu.VMEM((tk, 2 * tq), F32), pltpu.VMEM((tk, 2 * tq), F32),
                        pltpu.VMEM((tk, 2 * tq), BF16), pltpu.VMEM((tk, 2 * tq), BF16)],
        compiler_params=_params("parallel", "parallel", "arbitrary"),
        name="diff_attention",
    )(qvt, kk, qvt, vec(lq1), vec(lk1), vec(lq2), vec(lk2), vec(subln_gain))


def _bdot(a, b):
    return jnp.dot(a.astype(BF16), b.astype(BF16), preferred_element_type=F32)


def _split_bf16(w):
    hi = w.astype(BF16)
    return jnp.stack([hi, (w - hi.astype(F32)).astype(BF16)])


def _dot3(a, b_ref):
    a_hi = a.astype(BF16)
    a_lo = (a - a_hi.astype(F32)).astype(BF16)
    d = functools.partial(jnp.dot, preferred_element_type=F32)
    return d(a_hi, b_ref[0]) + (d(a_hi, b_ref[1]) + d(a_lo, b_ref[0]))


def _each(f, *lists):
    return [f(*xs) for xs in zip(*lists)]


RWKV_PREP_STAGES = 5


def _run_skewed(stage_gens, lead):
    results = [None] * len(stage_gens)

    def advance(i):
        try:
            next(stage_gens[i])
        except StopIteration as stop:
            results[i] = stop.value

    for _ in range(lead):
        advance(0)
    while any(r is None for r in results):
        for i in range(len(stage_gens)):
            if results[i] is None:
                advance(i)
    return results


def _rwkv_pairs(pairs, states, c):
    n = RWKV_HEAD_DIM
    c2 = 2 * c
    row = lax.broadcasted_iota(jnp.int32, (c2, c2), 0)
    col = lax.broadcasted_iota(jnp.int32, (c2, c2), 1)
    lower = row >= col
    strict = row > col
    eye2 = (row == col).astype(F32)
    rc = lax.broadcasted_iota(jnp.int32, (c, c), 0)
    cc = lax.broadcasted_iota(jnp.int32, (c, c), 1)
    lower_b = (rc >= cc).astype(BF16)
    li = lax.broadcasted_iota(jnp.int32, (LANES, LANES), 0)
    lj = lax.broadcasted_iota(jnp.int32, (LANES, LANES), 1)
    same_head = ((li // n) == (lj // n)).astype(BF16)
    eye_l = (li == lj).astype(F32)
    head1 = lax.broadcasted_iota(jnp.int32, (c, LANES), 1) >= n

    def stack(x):
        return jnp.concatenate([jnp.where(head1, 0.0, x), jnp.where(head1, x, 0.0)], axis=0)

    def head_sum(x):
        return jnp.dot(x.astype(BF16), same_head, preferred_element_type=F32)

    r, k, v, lw, ai, g, kk_w, ka_w, rk_w, lnw, lnb = [list(x) for x in zip(*pairs)]

    kk = _each(lambda k, w: k * w, k, kk_w)
    kk = _each(lambda kk: kk * lax.rsqrt(jnp.maximum(head_sum(kk * kk), 1e-24)), kk)
    k2 = _each(lambda k, ai, w: k * (1.0 + (ai - 1.0) * w), k, ai, ka_w)
    b = _each(lambda kk, ai: kk * ai, kk, ai)
    bonus = _each(lambda r, k2, w, v: head_sum(r * k2 * w) * v, r, k2, rk_w, v)
    yield

    def cumsum(lw):
        hi = lw.astype(BF16)
        r1 = lw - hi.astype(F32)
        mid = r1.astype(BF16)
        lo = (r1 - mid.astype(F32)).astype(BF16)
        c3 = jnp.dot(lower_b, jnp.concatenate([hi, mid, lo], axis=1), preferred_element_type=F32)
        return c3[:, :LANES] + c3[:, LANES:2 * LANES] + c3[:, 2 * LANES:]

    cum = _each(cumsum, lw)
    yield
    e_cum = _each(jnp.exp, cum)
    e_neg = _each(lambda cm: jnp.exp(-cm), cum)
    e_tail = _each(lambda cm: jnp.exp(cm[c - 1:c, :] - cm), cum)
    yield
    at = _each(lambda kk, cm, l: stack(-kk * jnp.exp(cm - l)), kk, cum, lw)
    rt = _each(lambda r, e: stack(r * e), r, e_cum)
    vs = _each(stack, v)
    yield
    qq = _each(lambda x, y: jnp.concatenate([x, y], axis=0).astype(BF16), at, rt)
    kq = _each(lambda b, k2, e: jnp.concatenate([stack(b * e), stack(k2 * e)], axis=0).astype(BF16),
               b, k2, e_neg)
    yield
    s = _each(_dot_t, qq, kq)
    a_ab = _each(lambda s: jnp.where(strict, s[:c2, :c2], 0.0), s)
    a_ak = _each(lambda s: jnp.where(strict, s[:c2, c2:], 0.0), s)
    a_rb = _each(lambda s: jnp.where(lower, s[c2:, :c2], 0.0), s)
    a_rk = _each(lambda s: jnp.where(lower, s[c2:, c2:], 0.0), s)
    yield

    def level(m):
        return ((row // (2 * m)) == (col // (2 * m))) & ((row // m) % 2 == 1) & ((col // m) % 2 == 0)

    t = _each(lambda x: eye2 + jnp.where(level(1), x, 0.0), a_ab)
    m = 2
    while m < c:
        sel = level(m)
        tn = _each(lambda t, x: _bdot(t, jnp.where(sel, x, 0.0)), t, a_ab)
        t = _each(lambda t, x: t + _bdot(x, t), t, tn)
        m *= 2
        yield

    x1 = _each(_bdot, a_ak, vs)
    wu = _each(lambda t, at, x1: _bdot(t, jnp.concatenate([at, x1], axis=1)), t, at, x1)
    bh_t = _each(lambda b, e: stack(b * e).T, b, e_tail)
    kh_t = _each(lambda k2, e: stack(k2 * e).T, k2, e_tail)
    yield
    z1 = _each(lambda a, bt, wu: _bdot(jnp.concatenate([a, bt], axis=0), wu), a_rb, bh_t, wu)
    z2 = _each(lambda a, kt, v: _bdot(jnp.concatenate([a, kt], axis=0), v), a_rk, kh_t, vs)
    pm = _each(lambda z1, rt, e: jnp.concatenate([rt + z1[:c2, :LANES],
                                                  eye_l * e[c - 1:c, :] + z1[c2:, :LANES]], axis=0),
               z1, rt, e_cum)
    yn0 = _each(lambda z1, z2: z1[:, LANES:] + z2, z1, z2)
    yield
    chunks = len(pairs) // len(states)
    h_next = list(states)
    yh = [None] * len(pairs)
    for j in range(chunks):
        for blk in range(len(states)):
            i = blk * chunks + j
            yh[i] = _bdot(pm[i], h_next[blk]) + yn0[i]
            h_next[blk] = yh[i][c2:]
    yield

    y = _each(lambda yh: yh[:c] + yh[c:c2], yh)
    d = _each(lambda y: y - head_sum(y) * (1.0 / n), y)
    var = _each(lambda d: head_sum(d * d) * (1.0 / n), d)
    outs = _each(lambda d, var, lnw, lnb, bonus, g: (d * lax.rsqrt(var + GN_EPS) * lnw + lnb + bonus) * g,
                 d, var, lnw, lnb, bonus, g)
    return outs, h_next


def _rwkv_kernel(rkv_ref, lora_ref, mu_rkv_ref, mu_lora_ref, w0_ref, wl_ref, a0_ref, al_ref, gl_ref,
                 kk_ref, ka_ref, rk_ref, lnw_ref, lnb_ref, o_ref,
                 state_s, prev_rkv_s, prev_lora_s, r_s, k_s, v_s, lw_s, ai_s, g_s, *, c):
    ci = pl.program_id(1)
    width = RWKV_HEADS * RWKV_HEAD_DIM

    @pl.when(ci == 0)
    def _():
        state_s[...] = jnp.zeros(state_s.shape, F32)
        prev_rkv_s[...] = jnp.zeros(prev_rkv_s.shape, F32)
        prev_lora_s[...] = jnp.zeros(prev_lora_s.shape, F32)

    def shift_mix(p, prev_s, mu):
        rolled = pltpu.roll(p, 1, 0)
        first = lax.broadcasted_iota(jnp.int32, p.shape, 0) == 0
        shifted = jnp.where(first, prev_s[7:8, :], rolled)
        prev_s[...] = p[p.shape[0] - 8:, :]
        return p + mu * (shifted - p)

    z = shift_mix(rkv_ref[...], prev_rkv_s, mu_rkv_ref[...])
    zl = shift_mix(lora_ref[...], prev_lora_s, mu_lora_ref[...])
    r_s[...] = z[:, :width]
    k_s[...] = z[:, width:2 * width]
    v_s[...] = z[:, 2 * width:]
    xw = zl[:, :DECAY_LORA]
    xa = zl[:, DECAY_LORA:DECAY_LORA + ICLR_LORA]
    xg = zl[:, DECAY_LORA + ICLR_LORA:DECAY_LORA + ICLR_LORA + GATE_LORA]
    lw_s[...] = -math.exp(-0.5) * jax.nn.sigmoid(w0_ref[...] + _dot3(jnp.tanh(xw), wl_ref))
    ai_s[...] = jax.nn.sigmoid(a0_ref[...] + _bdot(xa, al_ref[...]))
    g_s[...] = _bdot(jax.nn.sigmoid(xg), gl_ref[...])

    def group_body(gi, carry):
        slices = [pl.ds(pl.multiple_of((gi * RWKV_GROUP_BLOCKS + blk) * LANES, LANES), LANES)
                  for blk in range(RWKV_GROUP_BLOCKS)]
        chunk_rows = [pl.ds(j * c, c) for j in range(r_s.shape[0] // c)]
        pairs = [tuple(ref[rows, sl] for ref in (r_s, k_s, v_s, lw_s, ai_s, g_s))
                 + tuple(ref[:, sl] for ref in (kk_ref, ka_ref, rk_ref, lnw_ref, lnb_ref))
                 for sl in slices for rows in chunk_rows]
        states = [state_s[gi * RWKV_GROUP_BLOCKS + blk] for blk in range(RWKV_GROUP_BLOCKS)]
        per = len(chunk_rows)
        hb = RWKV_GROUP_BLOCKS // 2
        (outs_a, h_a), (outs_b, h_b) = _run_skewed(
            [_rwkv_pairs(pairs[:hb * per], states[:hb], c), _rwkv_pairs(pairs[hb * per:], states[hb:], c)],
            RWKV_PREP_STAGES)
        outs, h_next = outs_a + outs_b, h_a + h_b
        for blk, sl in enumerate(slices):
            for j, rows in enumerate(chunk_rows):
                o_ref[rows, sl] = outs[blk * len(chunk_rows) + j].astype(o_ref.dtype)
            state_s[gi * RWKV_GROUP_BLOCKS + blk] = h_next[blk]
        return carry

    lax.fori_loop(0, width // (LANES * RWKV_GROUP_BLOCKS), group_body, 0)


def rwkv7(rkv, lora, mu_rkv, mu_lora, w0, wl, a0, al, gl, k_k, k_a, r_k, lnx_w, lnx_b):
    b, s, _ = rkv.shape
    c = min(RWKV_CHUNK, s)
    rows = min(RWKV_STEP_CHUNKS * c, s)
    width = RWKV_HEADS * RWKV_HEAD_DIM
    row = lambda a: a.reshape(1, -1).astype(F32)
    const = lambda shape: pl.BlockSpec(shape, lambda bi, ci: (0,) * len(shape))
    return pl.pallas_call(
        functools.partial(_rwkv_kernel, c=c),
        grid=(b, s // rows),
        in_specs=[pl.BlockSpec((None, rows, 3 * width), lambda bi, ci: (bi, ci, 0)),
                  pl.BlockSpec((None, rows, LORA_PAD), lambda bi, ci: (bi, ci, 0)),
                  const((1, 3 * width)), const((1, LORA_PAD)),
                  const((1, width)), const((2, DECAY_LORA, width)),
                  const((1, width)), const((ICLR_LORA, width)), const((GATE_LORA, width)),
                  const((1, width)), const((1, width)), const((1, width)),
                  const((1, width)), const((1, width))],
        out_specs=pl.BlockSpec((None, rows, width), lambda bi, ci: (bi, ci, 0)),
        out_shape=jax.ShapeDtypeStruct((b, s, width), BF16),
        scratch_shapes=[pltpu.VMEM((width // LANES, LANES, LANES), F32),
                        pltpu.VMEM((8, 3 * width), F32), pltpu.VMEM((8, LORA_PAD), F32)]
                       + [pltpu.VMEM((rows, width), F32)] * 6,
        compiler_params=_params("parallel", "arbitrary"),
        name="rwkv7",
    )(rkv, lora, row(mu_rkv), row(mu_lora), row(w0), _split_bf16(wl.astype(F32)), row(a0), al.astype(BF16),
      gl.astype(BF16), row(k_k), row(k_a), row(r_k), row(lnx_w), row(lnx_b))


MERGE_CHUNK = 256


def _merge_kernel(ya_ref, yr_ref, g0_ref, g1_ref, wa_ref, wr_ref, o_ref):
    ya, yr = ya_ref[...], yr_ref[...]
    for c0 in range(0, o_ref.shape[1], MERGE_CHUNK):
        cs = slice(c0, c0 + MERGE_CHUNK)
        za = _dot(ya, wa_ref[:, cs])
        zr = _dot(yr, wr_ref[:, cs])
        ga = jax.nn.sigmoid(g0_ref[:, cs].astype(F32))
        gr = jax.nn.sigmoid(g1_ref[:, cs].astype(F32))
        o_ref[:, cs] = (ga * za + gr * zr).astype(o_ref.dtype)


def gated_merge(y_att, y_rwkv, gates, wa, wr, tm=1024, tn=1024):
    b, s, kdim = y_att.shape
    n = wa.shape[1]
    tm = min(tm, s)
    nb = n // tn
    return pl.pallas_call(
        _merge_kernel,
        grid=(b, s // tm, nb),
        in_specs=[pl.BlockSpec((None, tm, kdim), lambda bi, i, j: (bi, i, 0)),
                  pl.BlockSpec((None, tm, kdim), lambda bi, i, j: (bi, i, 0)),
                  pl.BlockSpec((None, tm, tn), lambda bi, i, j: (bi, i, j)),
                  pl.BlockSpec((None, tm, tn), lambda bi, i, j: (bi, i, nb + j)),
                  pl.BlockSpec((kdim, tn), lambda bi, i, j: (0, j)),
                  pl.BlockSpec((kdim, tn), lambda bi, i, j: (0, j))],
        out_specs=pl.BlockSpec((None, tm, tn), lambda bi, i, j: (bi, i, j)),
        out_shape=jax.ShapeDtypeStruct((b, s, n), BF16),
        compiler_params=_params("parallel", "parallel", "arbitrary"),
        name="gated_merge",
    )(y_att, y_rwkv, gates, gates, wa, wr)


def _out_proj_kernel(m_ref, w_ref, x_ref, mod_ref, gain_ref, x1_ref, h2_ref):
    x1 = x_ref[...] + mod_ref[2:3, :] * _dot(m_ref[...], w_ref[...])
    x1_ref[...] = x1
    xn = x1 * lax.rsqrt(jnp.mean(x1 * x1, axis=-1, keepdims=True) + NORM_EPS)
    h2_ref[...] = (xn * gain_ref[...] * (1.0 + mod_ref[4:5, :]) + mod_ref[3:4, :]).astype(h2_ref.dtype)


def out_proj(merged, w_out, x, mod, gain2, tm=512):
    b, s, d = x.shape
    tm = min(tm, s)
    return pl.pallas_call(
        _out_proj_kernel,
        grid=(b, s // tm),
        in_specs=[pl.BlockSpec((None, tm, d), lambda bi, i: (bi, i, 0)),
                  pl.BlockSpec((d, d), lambda bi, i: (0, 0), pipeline_mode=pl.Buffered(1)),
                  pl.BlockSpec((None, tm, d), lambda bi, i: (bi, i, 0)),
                  pl.BlockSpec((None, 8, d), lambda bi, i: (bi, 0, 0)),
                  pl.BlockSpec((1, d), lambda bi, i: (0, 0))],
        out_specs=[pl.BlockSpec((None, tm, d), lambda bi, i: (bi, i, 0)),
                   pl.BlockSpec((None, tm, d), lambda bi, i: (bi, i, 0))],
        out_shape=[jax.ShapeDtypeStruct((b, s, d), F32), jax.ShapeDtypeStruct((b, s, d), BF16)],
        compiler_params=_params("parallel", "parallel"),
        name="out_proj",
    )(merged, w_out, x, mod, gain2.reshape(1, d))


GLU_HALO = 16
GLU_CHUNK = 256


def _glu_up_kernel(h_ref, halo_ref, wg_ref, wv_ref, cw_ref, cb_ref, o_ref, hext_s):
    i = pl.program_id(1)

    @pl.when(pl.program_id(2) == 0)
    def _():
        hext_s[:GLU_HALO, :] = jnp.where(i > 0, halo_ref[...], jnp.zeros_like(halo_ref[...]))
        hext_s[GLU_HALO:, :] = h_ref[...]

    hext = hext_s[...]
    chunks = [slice(c0, c0 + GLU_CHUNK) for c0 in range(0, o_ref.shape[1], GLU_CHUNK)]

    def gate(ug, cs):
        cw = cw_ref[:, cs]
        conv = (cb_ref[:, cs] + pltpu.roll(ug, 2, 0)[GLU_HALO:] * cw[0:1, :]
                + pltpu.roll(ug, 1, 0)[GLU_HALO:] * cw[1:2, :] + ug[GLU_HALO:] * cw[2:3, :])
        return 0.5 * conv * (1.0 + lax.erf(conv * (2.0 ** -0.5)))

    ug = _dot(hext, wg_ref[:, chunks[0]])
    gates = []
    for cs, nxt in zip(chunks, chunks[1:] + [None]):
        ug_next = _dot(hext, wg_ref[:, nxt]) if nxt is not None else None
        gates.append(gate(ug, cs))
        ug = ug_next
    for cs, g in zip(chunks, gates):
        o_ref[:, cs] = (g * _dot(hext[GLU_HALO:], wv_ref[:, cs])).astype(o_ref.dtype)


def glu_up(h, w_up, conv_w, conv_b, tm=1024, tf=512):
    b, s, d = h.shape
    ff = w_up.shape[1] // 2
    tm = min(tm, s)
    nf = ff // tf
    cw8 = jnp.zeros((8, ff), F32).at[:conv_w.shape[0]].set(conv_w)
    hb = tm // GLU_HALO
    return pl.pallas_call(
        _glu_up_kernel,
        grid=(b, s // tm, nf),
        in_specs=[pl.BlockSpec((None, tm, d), lambda bi, i, f: (bi, i, 0)),
                  pl.BlockSpec((None, GLU_HALO, d), lambda bi, i, f: (bi, jnp.maximum(i * hb - 1, 0), 0)),
                  pl.BlockSpec((d, tf), lambda bi, i, f: (0, f)),
                  pl.BlockSpec((d, tf), lambda bi, i, f: (0, nf + f)),
                  pl.BlockSpec((8, tf), lambda bi, i, f: (0, f)),
                  pl.BlockSpec((1, tf), lambda bi, i, f: (0, f))],
        out_specs=pl.BlockSpec((None, tm, tf), lambda bi, i, f: (bi, i, f)),
        out_shape=jax.ShapeDtypeStruct((b, s, ff), BF16),
        scratch_shapes=[pltpu.VMEM((GLU_HALO + tm, d), BF16)],
        compiler_params=_params("parallel", "parallel", "arbitrary"),
        name="glu_up",
    )(h, h, w_up, w_up, cw8, conv_b.reshape(1, ff))


def _down_kernel(a_ref, wd_ref, x1_ref, mod_ref, gain_ref, o_ref):
    x2 = x1_ref[...] + mod_ref[5:6, :] * _dot(a_ref[...], wd_ref[...])
    o_ref[...] = x2 * lax.rsqrt(jnp.mean(x2 * x2, axis=-1, keepdims=True) + NORM_EPS) * gain_ref[...]


def down_proj(act, w_down, x1, mod, final_gain, tm=256):
    b, s, d = x1.shape
    ff = w_down.shape[0]
    tm = min(tm, s)
    return pl.pallas_call(
        _down_kernel,
        grid=(b, s // tm),
        in_specs=[pl.BlockSpec((None, tm, ff), lambda bi, i: (bi, i, 0)),
                  pl.BlockSpec((ff, d), lambda bi, i: (0, 0), pipeline_mode=pl.Buffered(1)),
                  pl.BlockSpec((None, tm, d), lambda bi, i: (bi, i, 0)),
                  pl.BlockSpec((None, 8, d), lambda bi, i: (bi, 0, 0)),
                  pl.BlockSpec((1, d), lambda bi, i: (0, 0))],
        out_specs=pl.BlockSpec((None, tm, d), lambda bi, i: (bi, i, 0)),
        out_shape=jax.ShapeDtypeStruct((b, s, d), F32),
        compiler_params=_params("parallel", "parallel"),
        name="down_proj",
    )(act, w_down, x1, mod, final_gain.reshape(1, d))


def kernel(x, c, w_ada, b_ada, norm1_gain, w_in, lambda_q1, lambda_k1, lambda_q2, lambda_k2, subln_gain, mu_shift, w0, w_lora_up, a0, a_lora_up, g_lora_up, k_k, k_a, r_k, lnx_w, lnx_b, w_branch, w_out, norm2_gain, w_up, conv_w, conv_b, w_down, final_gain):
    b, s, d = x.shape
    depth = w_in.shape[0]
    assert depth == 1
    l = 0
    qk_cols = 2 * ATT_HEADS * 2 * ATT_HEAD_DIM
    v_cols = ATT_HEADS * 2 * ATT_HEAD_DIM
    width = RWKV_HEADS * RWKV_HEAD_DIM
    n_lora = DECAY_LORA + ICLR_LORA + GATE_LORA
    rkv0 = qk_cols + v_cols
    lora0 = rkv0 + 3 * width
    gate0 = lora0 + n_lora

    c8 = jnp.zeros((8, d), F32).at[:b].set(c)
    mod = ada_modulation(c8, w_ada[l], b_ada[l])[:b]
    mod = jnp.pad(mod.reshape(b, 6, d), ((0, 0), (0, 2), (0, 0)))

    h = norm_mod(x, mod, norm1_gain[l], shift_row=0, scale_row=1)

    w_in_l = w_in[l]
    q_cols = qk_cols // 2
    w_qv_t = jnp.concatenate([w_in_l[:, :q_cols], w_in_l[:, qk_cols:rkv0]], axis=1).T.astype(BF16)
    w_k = w_in_l[:, q_cols:qk_cols].astype(BF16)
    w_gate = w_in_l[:, gate0:].astype(BF16)
    w_rkv = w_in_l[:, rkv0:lora0].astype(BF16)
    w_lora = jnp.pad(w_in_l[:, lora0:gate0], ((0, 0), (0, LORA_PAD - n_lora))).astype(BF16)

    rope_lanes, rope_rows = rope_tables(s)
    qv_t = matmul_t(h, w_qv_t, *rope_rows)
    kk = matmul(h, w_k, BF16, tn=1024, rope=rope_lanes)
    gates = matmul(h, w_gate, BF16, tm=2048, tn=1024)
    rkv = matmul(h, w_rkv, F32, tn=1024)
    lora = matmul(h, w_lora, F32, tm=2048, tn=LORA_PAD)

    y_att = diff_attention(qv_t.reshape(b, 2 * ATT_HEADS, LANES, s), kk,
                           lambda_q1[l], lambda_k1[l], lambda_q2[l], lambda_k2[l], subln_gain[l])

    mu = mu_shift[l]
    mu_lora = jnp.pad(mu[3 * width:], (0, LORA_PAD - n_lora))
    y_rwkv = rwkv7(rkv, lora, mu[:3 * width], mu_lora, w0[l], w_lora_up[l], a0[l], a_lora_up[l],
                   g_lora_up[l], k_k[l], k_a[l], r_k[l].reshape(-1), lnx_w[l], lnx_b[l])

    merged = gated_merge(y_att, y_rwkv, gates, w_branch[l, 0].astype(BF16), w_branch[l, 1].astype(BF16))
    x1, h2 = out_proj(merged, w_out[l].astype(BF16), x, mod, norm2_gain[l])

    act = glu_up(h2, w_up[l].astype(BF16), conv_w[l], conv_b[l])
    return down_proj(act, w_down[l].astype(BF16), x1, mod, final_gain)
```

```python
import functools
import math

import jax
import jax.numpy as jnp
from jax import lax
from jax.experimental import pallas as pl
from jax.experimental.pallas import tpu as pltpu

F32 = jnp.float32
BF16 = jnp.bfloat16
HIGHEST = lax.Precision.HIGHEST

ATT_HEADS = 8
ATT_HEAD_DIM = 64
ROT_DIM = ATT_HEAD_DIM // 4
ROPE_THETA = 500000.0
SUBLN_EPS = 1e-5
RWKV_HEADS = 16
RWKV_HEAD_DIM = 64
DECAY_LORA = 64
ICLR_LORA = 64
GATE_LORA = 160
GN_EPS = 64e-5
NORM_EPS = 1e-6
LAMBDA_INIT = 0.8 - 0.6 * math.exp(-0.3 * 0)

LANES = 128
VMEM_LIMIT = 48 * 1024 * 1024
RWKV_CHUNK = 64
RWKV_STEP_CHUNKS = 2
RWKV_GROUP_BLOCKS = 8
LORA_PAD = 384


def _params(*sem):
    return pltpu.CompilerParams(dimension_semantics=sem, vmem_limit_bytes=VMEM_LIMIT)


def _dot(a, b, precision=None):
    return jnp.dot(a, b, preferred_element_type=F32, precision=precision)


def _dot_t(a, b, precision=None):
    return lax.dot_general(a, b, (((1,), (1,)), ((), ())),
                           preferred_element_type=F32, precision=precision)


def _mod_kernel(c_ref, w_ref, b_ref, o_ref):
    c = c_ref[...]
    s = c * jax.nn.sigmoid(c)
    o_ref[...] = _dot(s, w_ref[...], HIGHEST) + b_ref[...]


def ada_modulation(c8, w_ada, b_ada, tn=1024):
    m, d = c8.shape
    n = w_ada.shape[1]
    return pl.pallas_call(
        _mod_kernel,
        grid=(n // tn,),
        in_specs=[pl.BlockSpec((m, d), lambda j: (0, 0)),
                  pl.BlockSpec((d, tn), lambda j: (0, j)),
                  pl.BlockSpec((1, tn), lambda j: (0, j))],
        out_specs=pl.BlockSpec((m, tn), lambda j: (0, j)),
        out_shape=jax.ShapeDtypeStruct((m, n), F32),
        compiler_params=_params("arbitrary"),
        name="ada_mod",
    )(c8, w_ada, b_ada.reshape(1, n))


def _norm_mod_kernel(x_ref, mod_ref, gain_ref, o_ref, *, shift_row, scale_row):
    x = x_ref[...]
    xn = x * lax.rsqrt(jnp.mean(x * x, axis=-1, keepdims=True) + NORM_EPS)
    h = xn * gain_ref[...] * (1.0 + mod_ref[scale_row:scale_row + 1, :]) + mod_ref[shift_row:shift_row + 1, :]
    o_ref[...] = h.astype(o_ref.dtype)


def norm_mod(x, mod, gain, shift_row, scale_row, tm=1024):
    b, s, d = x.shape
    tm = min(tm, s)
    return pl.pallas_call(
        functools.partial(_norm_mod_kernel, shift_row=shift_row, scale_row=scale_row),
        grid=(b, s // tm),
        in_specs=[pl.BlockSpec((None, tm, d), lambda bi, i: (bi, i, 0)),
                  pl.BlockSpec((None, 8, d), lambda bi, i: (bi, 0, 0)),
                  pl.BlockSpec((1, d), lambda bi, i: (0, 0))],
        out_specs=pl.BlockSpec((None, tm, d), lambda bi, i: (bi, i, 0)),
        out_shape=jax.ShapeDtypeStruct((b, s, d), BF16),
        compiler_params=_params("parallel", "parallel"),
        name="norm_mod",
    )(x, mod, gain.reshape(1, d))


def _mm_kernel(a_ref, w_ref, o_ref):
    o_ref[...] = _dot(a_ref[...], w_ref[...]).astype(o_ref.dtype)


def _mm_rope_kernel(a_ref, w_ref, cos_ref, sina_ref, sinb_ref, o_ref, *, tn):
    acc = _dot(a_ref[...], w_ref[...])
    cos, sina, sinb = cos_ref[...], sina_ref[...], sinb_ref[...]
    for j in range(tn // LANES):
        t = acc[:, j * LANES:(j + 1) * LANES]
        half = ROT_DIM // 2
        r = t * cos + pltpu.roll(t, LANES - half, 1) * sina + pltpu.roll(t, half, 1) * sinb
        o_ref[:, j * LANES:(j + 1) * LANES] = r.astype(o_ref.dtype)


def matmul(a, w, out_dtype, tm=1024, tn=512, rope=None):
    b, s, k = a.shape
    n = w.shape[1]
    tm = min(tm, s)
    tn = min(tn, n)
    assert s % tm == 0 and n % tn == 0
    in_specs = [pl.BlockSpec((None, tm, k), lambda bi, i, j: (bi, i, 0)),
                pl.BlockSpec((k, tn), lambda bi, i, j: (0, j))]
    args = [a, w]
    if rope is None:
        body = _mm_kernel
        name = "matmul"
    else:
        cos, sina, sinb = rope
        body = functools.partial(_mm_rope_kernel, tn=tn)
        in_specs += [pl.BlockSpec((tm, LANES), lambda bi, i, j: (i, 0))] * 3
        args += [cos, sina, sinb]
        name = "matmul_rope"
    return pl.pallas_call(
        body,
        grid=(b, s // tm, n // tn),
        in_specs=in_specs,
        out_specs=pl.BlockSpec((None, tm, tn), lambda bi, i, j: (bi, i, j)),
        out_shape=jax.ShapeDtypeStruct((b, s, n), out_dtype),
        compiler_params=_params("parallel", "parallel", "arbitrary"),
        name=name,
    )(*args)


def _mm_t_kernel(wt_ref, a_ref, cos_ref, sina_ref, sinb_ref, o_ref):
    acc = _dot_t(wt_ref[...], a_ref[...])
    tn, tm = acc.shape
    is_q = pl.program_id(2) == 0
    cos = jnp.where(is_q, cos_ref[...], 1.0)
    sina = jnp.where(is_q, sina_ref[...], 0.0)
    sinb = jnp.where(is_q, sinb_ref[...], 0.0)
    scale = jnp.where(is_q, ATT_HEAD_DIM ** -0.5 * math.log2(math.e), 1.0).astype(F32)
    half = ROT_DIM // 2
    per_map = lambda x: x.reshape(tn // ATT_HEAD_DIM, ATT_HEAD_DIM, tm)
    r = (per_map(acc) * cos[None] + per_map(pltpu.roll(acc, tn - half, 0)) * sina[None]
         + per_map(pltpu.roll(acc, half, 0)) * sinb[None])
    o_ref[...] = (r.reshape(tn, tm) * scale).astype(o_ref.dtype)


def matmul_t(a, wt, cos_t, sina_t, sinb_t, tm=1024, tn=1024):
    b, s, k = a.shape
    n = wt.shape[0]
    tm = min(tm, s)
    table = pl.BlockSpec((ATT_HEAD_DIM, tm), lambda bi, i, j: (0, i))
    return pl.pallas_call(
        _mm_t_kernel,
        grid=(b, s // tm, n // tn),
        in_specs=[pl.BlockSpec((tn, k), lambda bi, i, j: (j, 0)),
                  pl.BlockSpec((None, tm, k), lambda bi, i, j: (bi, i, 0)),
                  table, table, table],
        out_specs=pl.BlockSpec((None, tn, tm), lambda bi, i, j: (bi, j, i)),
        out_shape=jax.ShapeDtypeStruct((b, n, s), BF16),
        compiler_params=_params("parallel", "parallel", "arbitrary"),
        name="matmul_t",
    )(wt, a, cos_t, sina_t, sinb_t)


def rope_tables(seq):
    half = ROT_DIM // 2
    pos = jnp.arange(seq, dtype=F32)
    inv = ROPE_THETA ** (-jnp.arange(0, ROT_DIM, 2, dtype=F32) / ROT_DIM)
    ang = pos[:, None] * inv[None, :]
    cos, sin = jnp.cos(ang), jnp.sin(ang)
    ones = jnp.ones((seq, ATT_HEAD_DIM - ROT_DIM), F32)
    zeros = jnp.zeros((seq, ATT_HEAD_DIM - ROT_DIM), F32)
    zh = jnp.zeros((seq, half), F32)
    cos64 = jnp.concatenate([cos, cos, ones], axis=1)
    sina64 = jnp.concatenate([-sin, zh, zeros], axis=1)
    sinb64 = jnp.concatenate([zh, sin, zeros], axis=1)
    rep = LANES // ATT_HEAD_DIM
    by_lane = (jnp.tile(cos64, (1, rep)), jnp.tile(sina64, (1, rep)), jnp.tile(sinb64, (1, rep)))
    by_row = (cos64.T, sina64.T, sinb64.T)
    return by_lane, by_row


SUBLANES = 8
ATT_GROUP = 256


def _sublane_all(x, op):
    for shift in (4, 2, 1):
        x = op(x, pltpu.roll(x, shift, 0))
    return x


def _attn_kernel(qt_ref, k_ref, vt_ref, lq1_ref, lk1_ref, lq2_ref, lk2_ref, gain_ref, o_ref,
                 acc_s, q2_s, sa_s, sb_s, pa_s, pb_s, *, tq, tk):
    qi = pl.program_id(2)
    qt = qt_ref[...]
    row = lax.broadcasted_iota(jnp.int32, qt.shape, 0)
    zero = jnp.zeros_like(qt)
    q2 = jnp.concatenate([jnp.where(row < ATT_HEAD_DIM, qt, zero),
                          jnp.where(row >= ATT_HEAD_DIM, qt, zero)], axis=1)
    n2 = 2 * tq
    acc_s[...] = jnp.zeros(acc_s.shape, F32)

    def update(k0, nk, c0, nc, m_all, l_all, masked):
        cols = slice(c0, c0 + nc)
        kb = k_ref[pl.ds(k0, nk), :]
        vtb = vt_ref[:, pl.ds(k0, nk)]
        s = _dot(kb, q2_s[:, cols])
        if masked:
            kpos = k0 + lax.broadcasted_iota(jnp.int32, (nk, nc), 0)
            col = c0 + lax.broadcasted_iota(jnp.int32, (nk, nc), 1)
            qpos = qi * tq + jnp.where(col >= tq, col - tq, col)
            s = jnp.where(kpos <= qpos, s, -jnp.inf)
        s3 = s.reshape(nk // SUBLANES, SUBLANES, nc)
        m_old = m_all[:, cols]
        m_new = jnp.maximum(m_old, _sublane_all(jnp.max(s3, axis=0), jnp.maximum))
        p3 = jnp.exp2(s3 - m_new[None])
        corr = jnp.exp2(m_old - m_new)
        l_new = l_all[:, cols] * corr + jnp.sum(p3, axis=0)
        pv = _dot(vtb, p3.reshape(nk, nc).astype(BF16))
        acc3 = acc_s[:, cols].reshape(LANES // SUBLANES, SUBLANES, nc)
        acc_s[:, cols] = (acc3 * corr[None]).reshape(LANES, nc) + pv
        if nc == n2:
            return m_new, l_new
        put = lambda full, part: jnp.concatenate(
            ([full[:, :c0]] if c0 else []) + [part] + ([full[:, c0 + nc:]] if c0 + nc < n2 else []), axis=1)
        return put(m_all, m_new), put(l_all, l_new)

    q2_s[...] = q2
    groups = n2 // ATT_GROUP

    def weighted_values(t, p_prv, corr_prv, gs):
        pv = _dot(vt_ref[:, pl.ds(pl.multiple_of(t * tk, tk), tk)], p_prv[:, gs])
        acc3 = acc_s[:, gs].reshape(LANES // SUBLANES, SUBLANES, gs.stop - gs.start)
        acc_s[:, gs] = (acc3 * corr_prv[:, gs][None]).reshape(LANES, gs.stop - gs.start) + pv

    def step(t, carry, s_cur, p_cur, s_nxt, p_prv):
        m_old, l_old, corr_prv = carry
        t_prv = jnp.maximum(t - 1, 0)
        kn = pl.multiple_of((t + 1) * tk, tk)
        m_parts, l_parts, c_parts = [], [], []
        for g in range(groups):
            gs = slice(g * ATT_GROUP, (g + 1) * ATT_GROUP)
            weighted_values(t_prv, p_prv, corr_prv, gs)
            s_nxt[:, gs] = _dot(k_ref[pl.ds(kn, tk), :], q2_s[:, gs])
            s3 = s_cur[:, gs].reshape(tk // SUBLANES, SUBLANES, ATT_GROUP)
            m_o = m_old[:, gs]
            m_n = jnp.maximum(m_o, _sublane_all(jnp.max(s3, axis=0), jnp.maximum))
            p3 = jnp.exp2(s3 - m_n[None])
            corr = jnp.exp2(m_o - m_n)
            p_cur[:, gs] = p3.reshape(tk, ATT_GROUP).astype(BF16)
            m_parts.append(m_n)
            l_parts.append(l_old[:, gs] * corr + jnp.sum(p3, axis=0))
            c_parts.append(corr)
        cat = lambda parts: jnp.concatenate(parts, axis=1)
        return cat(m_parts), cat(l_parts), cat(c_parts)

    n_full = (qi * tq) // tk
    pb_s[...] = jnp.zeros(pb_s.shape, BF16)
    sa_s[...] = _dot(k_ref[pl.ds(0, tk), :], q2)
    carry = (jnp.full((SUBLANES, n2), -jnp.inf, F32), jnp.zeros((SUBLANES, n2), F32),
             jnp.ones((SUBLANES, n2), F32))

    def pair(i, carry):
        carry = step(2 * i, carry, sa_s, pa_s, sb_s, pb_s)
        return step(2 * i + 1, carry, sb_s, pb_s, sa_s, pa_s)

    m, l, corr = lax.fori_loop(0, n_full // 2, pair, carry)
    for g in range(groups):
        weighted_values(jnp.maximum(n_full - 1, 0), pb_s, corr, slice(g * ATT_GROUP, (g + 1) * ATT_GROUP))

    half = tq // 2
    kd = pl.multiple_of(qi * tq, tq)
    m, l = update(kd, half, 0, n2, m, l, True)
    m, l = update(kd + half, half, half, half, m, l, True)
    m, l = update(kd + half, half, tq + half, half, m, l, True)

    lam = (jnp.exp(jnp.sum(lq1_ref[...] * lk1_ref[...], axis=-1, keepdims=True))
           - jnp.exp(jnp.sum(lq2_ref[...] * lk2_ref[...], axis=-1, keepdims=True))
           + LAMBDA_INIT)
    l = _sublane_all(l, jnp.add)
    o3 = acc_s[...].reshape(LANES // SUBLANES, SUBLANES, n2) / l[None]
    d3 = o3[:, :, :tq] - lam * o3[:, :, tq:]
    ms = _sublane_all(jnp.sum(d3 * d3, axis=0), jnp.add) * (1.0 / LANES)
    d3 = d3 * lax.rsqrt(ms + SUBLN_EPS)[None]
    o = d3.reshape(LANES, tq).T
    o_ref[...] = (o * gain_ref[...] * (1.0 - LAMBDA_INIT)).astype(o_ref.dtype)


def diff_attention(qvt, kk, lq1, lk1, lq2, lk2, subln_gain, tq=1024):
    b, h2, _, s = qvt.shape
    h = h2 // 2
    tq = min(tq, s)
    tk = tq // 2
    vec = lambda a: a.reshape(1, -1).astype(F32)
    small = lambda n: pl.BlockSpec((1, n), lambda bi, hi, i: (0, 0))
    return pl.pallas_call(
        functools.partial(_attn_kernel, tq=tq, tk=tk),
        grid=(b, h, s // tq),
        in_specs=[pl.BlockSpec((None, None, LANES, tq), lambda bi, hi, i: (bi, hi, 0, i)),
                  pl.BlockSpec((None, s, LANES), lambda bi, hi, i: (bi, 0, hi)),
                  pl.BlockSpec((None, None, LANES, s), lambda bi, hi, i: (bi, h + hi, 0, 0)),
                  small(ATT_HEAD_DIM), small(ATT_HEAD_DIM), small(ATT_HEAD_DIM), small(ATT_HEAD_DIM),
                  small(LANES)],
        out_specs=pl.BlockSpec((None, tq, LANES), lambda bi, hi, i: (bi, i, hi)),
        out_shape=jax.ShapeDtypeStruct((b, s, h * LANES), BF16),
        scratch_shapes=[pltpu.VMEM((LANES, 2 * tq), F32), pltpu.VMEM((LANES, 2 * tq), BF16),
                        pltpu.VMEM((tk, 2 * tq), F32), pltpu.VMEM((tk, 2 * tq), F32),
                        pltpu.VMEM((tk, 2 * tq), BF16), pltpu.VMEM((tk, 2 * tq), BF16)],
        compiler_params=_params("parallel", "parallel", "arbitrary"),
        name="diff_attention",
    )(qvt, kk, qvt, vec(lq1), vec(lk1), vec(lq2), vec(lk2), vec(subln_gain))


def _bdot(a, b):
    return jnp.dot(a.astype(BF16), b.astype(BF16), preferred_element_type=F32)


def _split_bf16(w):
    hi = w.astype(BF16)
    return jnp.stack([hi, (w - hi.astype(F32)).astype(BF16)])


def _dot3(a, b_ref):
    a_hi = a.astype(BF16)
    a_lo = (a - a_hi.astype(F32)).astype(BF16)
    d = functools.partial(jnp.dot, preferred_element_type=F32)
    return d(a_hi, b_ref[0]) + (d(a_hi, b_ref[1]) + d(a_lo, b_ref[0]))


def _each(f, *lists):
    return [f(*xs) for xs in zip(*lists)]


def _rwkv_pairs(pairs, states, c):
    n = RWKV_HEAD_DIM
    c2 = 2 * c
    row = lax.broadcasted_iota(jnp.int32, (c2, c2), 0)
    col = lax.broadcasted_iota(jnp.int32, (c2, c2), 1)
    lower = row >= col
    strict = row > col
    eye2 = (row == col).astype(F32)
    rc = lax.broadcasted_iota(jnp.int32, (c, c), 0)
    cc = lax.broadcasted_iota(jnp.int32, (c, c), 1)
    lower_b = (rc >= cc).astype(BF16)
    li = lax.broadcasted_iota(jnp.int32, (LANES, LANES), 0)
    lj = lax.broadcasted_iota(jnp.int32, (LANES, LANES), 1)
    same_head = ((li // n) == (lj // n)).astype(BF16)
    eye_l = (li == lj).astype(F32)
    head1 = lax.broadcasted_iota(jnp.int32, (c, LANES), 1) >= n

    def stack(x):
        return jnp.concatenate([jnp.where(head1, 0.0, x), jnp.where(head1, x, 0.0)], axis=0)

    def head_sum(x):
        return jnp.dot(x.astype(BF16), same_head, preferred_element_type=F32)

    r, k, v, lw, ai, g, kk_w, ka_w, rk_w, lnw, lnb = [list(x) for x in zip(*pairs)]

    kk = _each(lambda k, w: k * w, k, kk_w)
    kk = _each(lambda kk: kk * lax.rsqrt(jnp.maximum(head_sum(kk * kk), 1e-24)), kk)
    k2 = _each(lambda k, ai, w: k * (1.0 + (ai - 1.0) * w), k, ai, ka_w)
    b = _each(lambda kk, ai: kk * ai, kk, ai)
    bonus = _each(lambda r, k2, w, v: head_sum(r * k2 * w) * v, r, k2, rk_w, v)

    def cumsum(lw):
        hi = lw.astype(BF16)
        r1 = lw - hi.astype(F32)
        mid = r1.astype(BF16)
        lo = (r1 - mid.astype(F32)).astype(BF16)
        c3 = jnp.dot(lower_b, jnp.concatenate([hi, mid, lo], axis=1), preferred_element_type=F32)
        return c3[:, :LANES] + c3[:, LANES:2 * LANES] + c3[:, 2 * LANES:]

    cum = _each(cumsum, lw)
    e_cum = _each(jnp.exp, cum)
    e_neg = _each(lambda cm: jnp.exp(-cm), cum)
    e_tail = _each(lambda cm: jnp.exp(cm[c - 1:c, :] - cm), cum)
    at = _each(lambda kk, cm, l: stack(-kk * jnp.exp(cm - l)), kk, cum, lw)
    rt = _each(lambda r, e: stack(r * e), r, e_cum)
    vs = _each(stack, v)
    qq = _each(lambda x, y: jnp.concatenate([x, y], axis=0).astype(BF16), at, rt)
    kq = _each(lambda b, k2, e: jnp.concatenate([stack(b * e), stack(k2 * e)], axis=0).astype(BF16),
               b, k2, e_neg)
    s = _each(_dot_t, qq, kq)
    a_ab = _each(lambda s: jnp.where(strict, s[:c2, :c2], 0.0), s)
    a_ak = _each(lambda s: jnp.where(strict, s[:c2, c2:], 0.0), s)
    a_rb = _each(lambda s: jnp.where(lower, s[c2:, :c2], 0.0), s)
    a_rk = _each(lambda s: jnp.where(lower, s[c2:, c2:], 0.0), s)

    def level(m):
        return ((row // (2 * m)) == (col // (2 * m))) & ((row // m) % 2 == 1) & ((col // m) % 2 == 0)

    t = _each(lambda x: eye2 + jnp.where(level(1), x, 0.0), a_ab)
    m = 2
    while m < c:
        sel = level(m)
        tn = _each(lambda t, x: _bdot(t, jnp.where(sel, x, 0.0)), t, a_ab)
        t = _each(lambda t, x: t + _bdot(x, t), t, tn)
        m *= 2

    x1 = _each(_bdot, a_ak, vs)
    wu = _each(lambda t, at, x1: _bdot(t, jnp.concatenate([at, x1], axis=1)), t, at, x1)
    bh_t = _each(lambda b, e: stack(b * e).T, b, e_tail)
    kh_t = _each(lambda k2, e: stack(k2 * e).T, k2, e_tail)
    z1 = _each(lambda a, bt, wu: _bdot(jnp.concatenate([a, bt], axis=0), wu), a_rb, bh_t, wu)
    z2 = _each(lambda a, kt, v: _bdot(jnp.concatenate([a, kt], axis=0), v), a_rk, kh_t, vs)
    pm = _each(lambda z1, rt, e: jnp.concatenate([rt + z1[:c2, :LANES],
                                                  eye_l * e[c - 1:c, :] + z1[c2:, :LANES]], axis=0),
               z1, rt, e_cum)
    yn0 = _each(lambda z1, z2: z1[:, LANES:] + z2, z1, z2)
    chunks = len(pairs) // len(states)
    h_next = list(states)
    yh = [None] * len(pairs)
    for j in range(chunks):
        for blk in range(len(states)):
            i = blk * chunks + j
            yh[i] = _bdot(pm[i], h_next[blk]) + yn0[i]
            h_next[blk] = yh[i][c2:]

    y = _each(lambda yh: yh[:c] + yh[c:c2], yh)
    d = _each(lambda y: y - head_sum(y) * (1.0 / n), y)
    var = _each(lambda d: head_sum(d * d) * (1.0 / n), d)
    outs = _each(lambda d, var, lnw, lnb, bonus, g: (d * lax.rsqrt(var + GN_EPS) * lnw + lnb + bonus) * g,
                 d, var, lnw, lnb, bonus, g)
    return outs, h_next


def _rwkv_kernel(rkv_ref, lora_ref, mu_rkv_ref, mu_lora_ref, w0_ref, wl_ref, a0_ref, al_ref, gl_ref,
                 kk_ref, ka_ref, rk_ref, lnw_ref, lnb_ref, o_ref,
                 state_s, prev_rkv_s, prev_lora_s, r_s, k_s, v_s, lw_s, ai_s, g_s, *, c):
    ci = pl.program_id(1)
    width = RWKV_HEADS * RWKV_HEAD_DIM

    @pl.when(ci == 0)
    def _():
        state_s[...] = jnp.zeros(state_s.shape, F32)
        prev_rkv_s[...] = jnp.zeros(prev_rkv_s.shape, F32)
        prev_lora_s[...] = jnp.zeros(prev_lora_s.shape, F32)

    def shift_mix(src_ref, prev_s, mu_ref, cs):
        p = src_ref[:, cs]
        rolled = pltpu.roll(p, 1, 0)
        first = lax.broadcasted_iota(jnp.int32, p.shape, 0) == 0
        shifted = jnp.where(first, prev_s[7:8, cs], rolled)
        prev_s[:, cs] = p[p.shape[0] - 8:, :]
        return p + mu_ref[:, cs] * (shifted - p)

    blocks = width // LANES
    for j in range(3 * blocks):
        dst = (r_s, k_s, v_s)[j // blocks]
        dst[:, (j % blocks) * LANES:(j % blocks + 1) * LANES] = shift_mix(
            rkv_ref, prev_rkv_s, mu_rkv_ref, slice(j * LANES, (j + 1) * LANES))
    zl = jnp.concatenate([shift_mix(lora_ref, prev_lora_s, mu_lora_ref, slice(j * LANES, (j + 1) * LANES))
                          for j in range(LORA_PAD // LANES)], axis=1)
    xw = zl[:, :DECAY_LORA]
    xa = zl[:, DECAY_LORA:DECAY_LORA + ICLR_LORA]
    xg = zl[:, DECAY_LORA + ICLR_LORA:DECAY_LORA + ICLR_LORA + GATE_LORA]
    lw_s[...] = -math.exp(-0.5) * jax.nn.sigmoid(w0_ref[...] + _dot3(jnp.tanh(xw), wl_ref))
    ai_s[...] = jax.nn.sigmoid(a0_ref[...] + _bdot(xa, al_ref[...]))
    g_s[...] = _bdot(jax.nn.sigmoid(xg), gl_ref[...])

    def group_body(gi, carry):
        slices = [pl.ds(pl.multiple_of((gi * RWKV_GROUP_BLOCKS + blk) * LANES, LANES), LANES)
                  for blk in range(RWKV_GROUP_BLOCKS)]
        chunk_rows = [pl.ds(j * c, c) for j in range(r_s.shape[0] // c)]
        pairs = [tuple(ref[rows, sl] for ref in (r_s, k_s, v_s, lw_s, ai_s, g_s))
                 + tuple(ref[:, sl] for ref in (kk_ref, ka_ref, rk_ref, lnw_ref, lnb_ref))
                 for sl in slices for rows in chunk_rows]
        states = [state_s[gi * RWKV_GROUP_BLOCKS + blk] for blk in range(RWKV_GROUP_BLOCKS)]
        outs, h_next = _rwkv_pairs(pairs, states, c)
        for blk, sl in enumerate(slices):
            for j, rows in enumerate(chunk_rows):
                o_ref[rows, sl] = outs[blk * len(chunk_rows) + j].astype(o_ref.dtype)
            state_s[gi * RWKV_GROUP_BLOCKS + blk] = h_next[blk]
        return carry

    lax.fori_loop(0, width // (LANES * RWKV_GROUP_BLOCKS), group_body, 0)


def rwkv7(rkv, lora, mu_rkv, mu_lora, w0, wl, a0, al, gl, k_k, k_a, r_k, lnx_w, lnx_b):
    b, s, _ = rkv.shape
    c = min(RWKV_CHUNK, s)
    rows = min(RWKV_STEP_CHUNKS * c, s)
    width = RWKV_HEADS * RWKV_HEAD_DIM
    row = lambda a: a.reshape(1, -1).astype(F32)
    const = lambda shape: pl.BlockSpec(shape, lambda bi, ci: (0,) * len(shape))
    return pl.pallas_call(
        functools.partial(_rwkv_kernel, c=c),
        grid=(b, s // rows),
        in_specs=[pl.BlockSpec((None, rows, 3 * width), lambda bi, ci: (bi, ci, 0)),
                  pl.BlockSpec((None, rows, LORA_PAD), lambda bi, ci: (bi, ci, 0)),
                  const((1, 3 * width)), const((1, LORA_PAD)),
                  const((1, width)), const((2, DECAY_LORA, width)),
                  const((1, width)), const((ICLR_LORA, width)), const((GATE_LORA, width)),
                  const((1, width)), const((1, width)), const((1, width)),
                  const((1, width)), const((1, width))],
        out_specs=pl.BlockSpec((None, rows, width), lambda bi, ci: (bi, ci, 0)),
        out_shape=jax.ShapeDtypeStruct((b, s, width), BF16),
        scratch_shapes=[pltpu.VMEM((width // LANES, LANES, LANES), F32),
                        pltpu.VMEM((8, 3 * width), F32), pltpu.VMEM((8, LORA_PAD), F32)]
                       + [pltpu.VMEM((rows, width), F32)] * 6,
        compiler_params=_params("parallel", "arbitrary"),
        name="rwkv7",
    )(rkv, lora, row(mu_rkv), row(mu_lora), row(w0), _split_bf16(wl.astype(F32)), row(a0), al.astype(BF16),
      gl.astype(BF16), row(k_k), row(k_a), row(r_k), row(lnx_w), row(lnx_b))


MERGE_CHUNK = 256


def _merge_kernel(ya_ref, yr_ref, g0_ref, g1_ref, wa_ref, wr_ref, o_ref):
    ya, yr = ya_ref[...], yr_ref[...]
    for c0 in range(0, o_ref.shape[1], MERGE_CHUNK):
        cs = slice(c0, c0 + MERGE_CHUNK)
        za = _dot(ya, wa_ref[:, cs])
        zr = _dot(yr, wr_ref[:, cs])
        ga = jax.nn.sigmoid(g0_ref[:, cs].astype(F32))
        gr = jax.nn.sigmoid(g1_ref[:, cs].astype(F32))
        o_ref[:, cs] = (ga * za + gr * zr).astype(o_ref.dtype)


def gated_merge(y_att, y_rwkv, gates, wa, wr, tm=1024, tn=1024):
    b, s, kdim = y_att.shape
    n = wa.shape[1]
    tm = min(tm, s)
    nb = n // tn
    return pl.pallas_call(
        _merge_kernel,
        grid=(b, s // tm, nb),
        in_specs=[pl.BlockSpec((None, tm, kdim), lambda bi, i, j: (bi, i, 0)),
                  pl.BlockSpec((None, tm, kdim), lambda bi, i, j: (bi, i, 0)),
                  pl.BlockSpec((None, tm, tn), lambda bi, i, j: (bi, i, j)),
                  pl.BlockSpec((None, tm, tn), lambda bi, i, j: (bi, i, nb + j)),
                  pl.BlockSpec((kdim, tn), lambda bi, i, j: (0, j)),
                  pl.BlockSpec((kdim, tn), lambda bi, i, j: (0, j))],
        out_specs=pl.BlockSpec((None, tm, tn), lambda bi, i, j: (bi, i, j)),
        out_shape=jax.ShapeDtypeStruct((b, s, n), BF16),
        compiler_params=_params("parallel", "parallel", "arbitrary"),
        name="gated_merge",
    )(y_att, y_rwkv, gates, gates, wa, wr)


def _out_proj_kernel(m_ref, w_ref, x_ref, mod_ref, gain_ref, x1_ref, h2_ref):
    x1 = x_ref[...] + mod_ref[2:3, :] * _dot(m_ref[...], w_ref[...])
    x1_ref[...] = x1
    xn = x1 * lax.rsqrt(jnp.mean(x1 * x1, axis=-1, keepdims=True) + NORM_EPS)
    h2_ref[...] = (xn * gain_ref[...] * (1.0 + mod_ref[4:5, :]) + mod_ref[3:4, :]).astype(h2_ref.dtype)


def out_proj(merged, w_out, x, mod, gain2, tm=512):
    b, s, d = x.shape
    tm = min(tm, s)
    return pl.pallas_call(
        _out_proj_kernel,
        grid=(b, s // tm),
        in_specs=[pl.BlockSpec((None, tm, d), lambda bi, i: (bi, i, 0)),
                  pl.BlockSpec((d, d), lambda bi, i: (0, 0), pipeline_mode=pl.Buffered(1)),
                  pl.BlockSpec((None, tm, d), lambda bi, i: (bi, i, 0)),
                  pl.BlockSpec((None, 8, d), lambda bi, i: (bi, 0, 0)),
                  pl.BlockSpec((1, d), lambda bi, i: (0, 0))],
        out_specs=[pl.BlockSpec((None, tm, d), lambda bi, i: (bi, i, 0)),
                   pl.BlockSpec((None, tm, d), lambda bi, i: (bi, i, 0))],
        out_shape=[jax.ShapeDtypeStruct((b, s, d), F32), jax.ShapeDtypeStruct((b, s, d), BF16)],
        compiler_params=_params("parallel", "parallel"),
        name="out_proj",
    )(merged, w_out, x, mod, gain2.reshape(1, d))


GLU_HALO = 16
GLU_CHUNK = 256


def _glu_up_kernel(h_ref, halo_ref, wg_ref, wv_ref, cw_ref, cb_ref, o_ref, hext_s):
    i = pl.program_id(1)

    @pl.when(pl.program_id(2) == 0)
    def _():
        hext_s[:GLU_HALO, :] = jnp.where(i > 0, halo_ref[...], jnp.zeros_like(halo_ref[...]))
        hext_s[GLU_HALO:, :] = h_ref[...]

    hext = hext_s[...]
    chunks = [slice(c0, c0 + GLU_CHUNK) for c0 in range(0, o_ref.shape[1], GLU_CHUNK)]

    def gate(ug, cs):
        cw = cw_ref[:, cs]
        conv = (cb_ref[:, cs] + pltpu.roll(ug, 2, 0)[GLU_HALO:] * cw[0:1, :]
                + pltpu.roll(ug, 1, 0)[GLU_HALO:] * cw[1:2, :] + ug[GLU_HALO:] * cw[2:3, :])
        return 0.5 * conv * (1.0 + lax.erf(conv * (2.0 ** -0.5)))

    ug = _dot(hext, wg_ref[:, chunks[0]])
    gates = []
    for cs, nxt in zip(chunks, chunks[1:] + [None]):
        ug_next = _dot(hext, wg_ref[:, nxt]) if nxt is not None else None
        gates.append(gate(ug, cs))
        ug = ug_next
    for cs, g in zip(chunks, gates):
        o_ref[:, cs] = (g * _dot(hext[GLU_HALO:], wv_ref[:, cs])).astype(o_ref.dtype)


def glu_up(h, w_up, conv_w, conv_b, tm=1024, tf=512):
    b, s, d = h.shape
    ff = w_up.shape[1] // 2
    tm = min(tm, s)
    nf = ff // tf
    cw8 = jnp.zeros((8, ff), F32).at[:conv_w.shape[0]].set(conv_w)
    hb = tm // GLU_HALO
    return pl.pallas_call(
        _glu_up_kernel,
        grid=(b, s // tm, nf),
        in_specs=[pl.BlockSpec((None, tm, d), lambda bi, i, f: (bi, i, 0)),
                  pl.BlockSpec((None, GLU_HALO, d), lambda bi, i, f: (bi, jnp.maximum(i * hb - 1, 0), 0)),
                  pl.BlockSpec((d, tf), lambda bi, i, f: (0, f)),
                  pl.BlockSpec((d, tf), lambda bi, i, f: (0, nf + f)),
                  pl.BlockSpec((8, tf), lambda bi, i, f: (0, f)),
                  pl.BlockSpec((1, tf), lambda bi, i, f: (0, f))],
        out_specs=pl.BlockSpec((None, tm, tf), lambda bi, i, f: (bi, i, f)),
        out_shape=jax.ShapeDtypeStruct((b, s, ff), BF16),
        scratch_shapes=[pltpu.VMEM((GLU_HALO + tm, d), BF16)],
        compiler_params=_params("parallel", "parallel", "arbitrary"),
        name="glu_up",
    )(h, h, w_up, w_up, cw8, conv_b.reshape(1, ff))


def _down_kernel(a_ref, wd_ref, x1_ref, mod_ref, gain_ref, o_ref):
    x2 = x1_ref[...] + mod_ref[5:6, :] * _dot(a_ref[...], wd_ref[...])
    o_ref[...] = x2 * lax.rsqrt(jnp.mean(x2 * x2, axis=-1, keepdims=True) + NORM_EPS) * gain_ref[...]


def down_proj(act, w_down, x1, mod, final_gain, tm=256):
    b, s, d = x1.shape
    ff = w_down.shape[0]
    tm = min(tm, s)
    return pl.pallas_call(
        _down_kernel,
        grid=(b, s // tm),
        in_specs=[pl.BlockSpec((None, tm, ff), lambda bi, i: (bi, i, 0)),
                  pl.BlockSpec((ff, d), lambda bi, i: (0, 0), pipeline_mode=pl.Buffered(1)),
                  pl.BlockSpec((None, tm, d), lambda bi, i: (bi, i, 0)),
                  pl.BlockSpec((None, 8, d), lambda bi, i: (bi, 0, 0)),
                  pl.BlockSpec((1, d), lambda bi, i: (0, 0))],
        out_specs=pl.BlockSpec((None, tm, d), lambda bi, i: (bi, i, 0)),
        out_shape=jax.ShapeDtypeStruct((b, s, d), F32),
        compiler_params=_params("parallel", "parallel"),
        name="down_proj",
    )(act, w_down, x1, mod, final_gain.reshape(1, d))


def kernel(x, c, w_ada, b_ada, norm1_gain, w_in, lambda_q1, lambda_k1, lambda_q2, lambda_k2, subln_gain, mu_shift, w0, w_lora_up, a0, a_lora_up, g_lora_up, k_k, k_a, r_k, lnx_w, lnx_b, w_branch, w_out, norm2_gain, w_up, conv_w, conv_b, w_down, final_gain):
    b, s, d = x.shape
    depth = w_in.shape[0]
    assert depth == 1
    l = 0
    qk_cols = 2 * ATT_HEADS * 2 * ATT_HEAD_DIM
    v_cols = ATT_HEADS * 2 * ATT_HEAD_DIM
    width = RWKV_HEADS * RWKV_HEAD_DIM
    n_lora = DECAY_LORA + ICLR_LORA + GATE_LORA
    rkv0 = qk_cols + v_cols
    lora0 = rkv0 + 3 * width
    gate0 = lora0 + n_lora

    c8 = jnp.zeros((8, d), F32).at[:b].set(c)
    mod = ada_modulation(c8, w_ada[l], b_ada[l])[:b]
    mod = jnp.pad(mod.reshape(b, 6, d), ((0, 0), (0, 2), (0, 0)))

    h = norm_mod(x, mod, norm1_gain[l], shift_row=0, scale_row=1)

    w_in_l = w_in[l]
    q_cols = qk_cols // 2
    w_qv_t = jnp.concatenate([w_in_l[:, :q_cols], w_in_l[:, qk_cols:rkv0]], axis=1).T.astype(BF16)
    w_k = w_in_l[:, q_cols:qk_cols].astype(BF16)
    w_gate = w_in_l[:, gate0:].astype(BF16)
    w_rkv = w_in_l[:, rkv0:lora0].astype(BF16)
    w_lora = jnp.pad(w_in_l[:, lora0:gate0], ((0, 0), (0, LORA_PAD - n_lora))).astype(BF16)

    rope_lanes, rope_rows = rope_tables(s)
    qv_t = matmul_t(h, w_qv_t, *rope_rows)
    kk = matmul(h, w_k, BF16, tn=1024, rope=rope_lanes)
    gates = matmul(h, w_gate, BF16, tm=2048, tn=1024)
    rkv = matmul(h, w_rkv, F32, tn=1024)
    lora = matmul(h, w_lora, F32, tm=2048, tn=LORA_PAD)

    y_att = diff_attention(qv_t.reshape(b, 2 * ATT_HEADS, LANES, s), kk,
                           lambda_q1[l], lambda_k1[l], lambda_q2[l], lambda_k2[l], subln_gain[l])

    mu = mu_shift[l]
    mu_lora = jnp.pad(mu[3 * width:], (0, LORA_PAD - n_lora))
    y_rwkv = rwkv7(rkv, lora, mu[:3 * width], mu_lora, w0[l], w_lora_up[l], a0[l], a_lora_up[l],
                   g_lora_up[l], k_k[l], k_a[l], r_k[l].reshape(-1), lnx_w[l], lnx_b[l])

    merged = gated_merge(y_att, y_rwkv, gates, w_branch[l, 0].astype(BF16), w_branch[l, 1].astype(BF16))
    x1, h2 = out_proj(merged, w_out[l].astype(BF16), x, mod, norm2_gain[l])

    act = glu_up(h2, w_up[l].astype(BF16), conv_w[l], conv_b[l])
    return down_proj(act, w_down[l].astype(BF16), x1, mod, final_gain)
```
